```python
import jax, jax.numpy as jnp
from jax import lax
import numpy as np

D_MODEL = 1024
BATCH = 32
SEQ = 2048
DEPTH = 4
DEC_BATCH = 16
DEC_SEQ = 32
PAST_LEN = 1024

CHUNK = 64
N_HEADS = 16
HEAD_DIM = 64
N_KV_HEADS = 4
GQA_GROUP = N_HEADS // N_KV_HEADS
ATTN_WIDTH = N_HEADS * HEAD_DIM
KV_WIDTH = N_KV_HEADS * HEAD_DIM
WINDOW = 128
WINDOW_CHUNKS = WINDOW // CHUNK
POOL_WIDTH = D_MODEL
POOL_WINDOWS = (2, 4, 8, 16)
N_POOL_GROUPS = len(POOL_WINDOWS)
POOL_GROUP_WIDTH = POOL_WIDTH // N_POOL_GROUPS
POOL_HIST = max(POOL_WINDOWS) - 1
N_POOL_LAYERS = (DEPTH + 1) // 2
N_ATTN_LAYERS = DEPTH // 2
NORM_EPS = 1e-6
MASK_VALUE = -1e30

kernel_name = "chunk_streaming_pool_swa_hybrid_step"


def rms_norm(x, g):
    xf = x.astype(jnp.float32)
    y = xf * lax.rsqrt(jnp.mean(xf * xf, axis=-1, keepdims=True) + NORM_EPS)
    return (y * g.astype(jnp.float32)).astype(x.dtype)


def alibi_slopes():
    h = jnp.arange(1, N_HEADS + 1, dtype=jnp.float32)
    return jnp.exp2(-8.0 * h / N_HEADS)


def modulation(c, w, b):
    m = jnp.einsum('bd,de->be', jax.nn.silu(c), w) + b
    return jnp.split(m, 3, axis=-1)


def multiscale_pool(u, hist, hist_pos):
    B, T, E = u.shape
    u_ext = jnp.concatenate([hist.astype(u.dtype), u], axis=1)
    valid = jnp.concatenate([(hist_pos >= 0).astype(jnp.float32), jnp.ones((T,), jnp.float32)])
    cs = jnp.concatenate([jnp.zeros((B, 1, E), jnp.float32), jnp.cumsum(u_ext.astype(jnp.float32), axis=1)], axis=1)
    cv = jnp.concatenate([jnp.zeros((1,), jnp.float32), jnp.cumsum(valid)])
    uf = u.astype(jnp.float32)
    outs = []
    for g, w in enumerate(POOL_WINDOWS):
        sl = slice(g * POOL_GROUP_WIDTH, (g + 1) * POOL_GROUP_WIDTH)
        lo = POOL_HIST + 1 - w
        s = cs[:, POOL_HIST + 1:, sl] - cs[:, lo:lo + T, sl]
        n = cv[POOL_HIST + 1:] - cv[lo:lo + T]
        outs.append(s / n[None, :, None] - uf[..., sl])
    return jnp.stack(outs, axis=2), u_ext[:, -POOL_HIST:]


def pool_mixer(h, hist, hist_pos, w_in, w_grp, scale, w_out):
    B, T, _ = h.shape
    u, gate = jnp.split(h @ w_in, 2, axis=-1)
    pooled, new_hist = multiscale_pool(u, hist, hist_pos)
    z = jnp.einsum('btgc,gcd->btgd', pooled, w_grp.astype(jnp.float32)).reshape(B, T, POOL_WIDTH)
    z = z * scale.astype(jnp.float32)
    y = (z.astype(h.dtype) * jax.nn.silu(gate)) @ w_out
    return y, new_hist


def qkv_gate(h, w_in):
    B, T, _ = h.shape
    p = h @ w_in
    q, k, v, gate = jnp.split(p, [ATTN_WIDTH, ATTN_WIDTH + KV_WIDTH, ATTN_WIDTH + 2 * KV_WIDTH], axis=-1)
    return (q.reshape(B, T, N_KV_HEADS, GQA_GROUP, HEAD_DIM),
            k.reshape(B, T, N_KV_HEADS, HEAD_DIM),
            v.reshape(B, T, N_KV_HEADS, HEAD_DIM), gate)


def band_attend(q, k, v, qpos, kpos, sinks):
    s = jnp.einsum('bnqkgd,bnskd->bnkgqs', q.astype(jnp.float32), k.astype(jnp.float32)) * (HEAD_DIM ** -0.5)
    slopes = alibi_slopes().reshape(N_KV_HEADS, GQA_GROUP)
    dist = jnp.abs(qpos[:, :, None] - kpos[:, None, :]).astype(jnp.float32)
    qc = qpos[:, :, None] // CHUNK
    kc = kpos[:, None, :] // CHUNK
    allowed = (kpos[:, None, :] >= 0) & (kc <= qc) & (kc >= qc - WINDOW_CHUNKS)
    s = s - slopes[None, None, :, :, None, None] * dist[None, :, None, None]
    s = jnp.where(allowed[None, :, None, None], s, MASK_VALUE)
    sink = sinks.astype(jnp.float32).reshape(N_KV_HEADS, GQA_GROUP)[None, None, :, :, None]
    m = jnp.maximum(jnp.max(s, axis=-1), sink)
    p = jnp.exp(s - m[..., None])
    denom = jnp.sum(p, axis=-1) + jnp.exp(sink - m)
    return jnp.einsum('bnkgqs,bnskd->bnqkgd', p / denom[..., None], v.astype(jnp.float32))


def attn_prompt(h, w_in, sinks, w_out):
    B, T, _ = h.shape
    NC = T // CHUNK
    q, k, v, gate = qkv_gate(h, w_in)
    pad = WINDOW_CHUNKS * CHUNK

    def band(a):
        ap = jnp.pad(a, ((0, 0), (pad, 0), (0, 0), (0, 0))).reshape(B, NC + WINDOW_CHUNKS, CHUNK, N_KV_HEADS, HEAD_DIM)
        return jnp.concatenate([ap[:, j:j + NC] for j in range(WINDOW_CHUNKS + 1)], axis=2)

    qb = q.reshape(B, NC, CHUNK, N_KV_HEADS, GQA_GROUP, HEAD_DIM)
    qpos = jnp.arange(T, dtype=jnp.int32).reshape(NC, CHUNK)
    kpos = jnp.concatenate([qpos + (j - WINDOW_CHUNKS) * CHUNK for j in range(WINDOW_CHUNKS + 1)], axis=1)
    o = band_attend(qb, band(k), band(v), qpos, kpos, sinks).reshape(B, T, ATTN_WIDTH)
    y = (o.astype(h.dtype) * jax.nn.silu(gate)) @ w_out
    return y, k[:, -WINDOW:], v[:, -WINDOW:]


def attn_sample(h, k_cache, v_cache, w_in, sinks, w_out):
    B, T, _ = h.shape
    L = k_cache.shape[1]
    q, k, v, gate = qkv_gate(h, w_in)
    k_all = jnp.concatenate([k_cache.astype(k.dtype), k], axis=1)
    v_all = jnp.concatenate([v_cache.astype(v.dtype), v], axis=1)
    qpos = PAST_LEN + jnp.arange(T, dtype=jnp.int32)
    kpos = jnp.concatenate([PAST_LEN - L + jnp.arange(L, dtype=jnp.int32), qpos])
    o = band_attend(q[:, None], k_all[:, None], v_all[:, None], qpos[None], kpos[None], sinks)
    o = o[:, 0].reshape(B, T, ATTN_WIDTH)
    y = (o.astype(h.dtype) * jax.nn.silu(gate)) @ w_out
    return y, k_all[:, -L:], v_all[:, -L:]


def setup_inputs(seed: int = 0) -> dict:
    key = jax.random.key(seed)
    ks = jax.random.split(key, 18)
    nrm = jax.random.normal
    f32 = jnp.float32
    return {
        "x_prompt": nrm(ks[0], (BATCH, SEQ, D_MODEL), f32),
        "x_sample": nrm(ks[1], (DEC_BATCH, DEC_SEQ, D_MODEL), f32),
        "c_prompt": nrm(ks[2], (BATCH, D_MODEL), f32),
        "c_sample": nrm(ks[3], (DEC_BATCH, D_MODEL), f32),
        "cache_pool": nrm(ks[4], (N_POOL_LAYERS, DEC_BATCH, POOL_HIST, POOL_WIDTH), f32),
        "cache_k": nrm(ks[5], (N_ATTN_LAYERS, DEC_BATCH, min(WINDOW, PAST_LEN), N_KV_HEADS, HEAD_DIM), f32),
        "cache_v": nrm(ks[6], (N_ATTN_LAYERS, DEC_BATCH, min(WINDOW, PAST_LEN), N_KV_HEADS, HEAD_DIM), f32),
        "norm_g": 1.0 + 0.1 * nrm(ks[7], (DEPTH, D_MODEL), f32),
        "ada_w": 0.5 * D_MODEL ** -0.5 * nrm(ks[8], (DEPTH, D_MODEL, 3 * D_MODEL), f32),
        "ada_b": 0.02 * nrm(ks[9], (DEPTH, 3 * D_MODEL), f32),
        "pool_w_in": D_MODEL ** -0.5 * nrm(ks[10], (N_POOL_LAYERS, D_MODEL, 2 * POOL_WIDTH), f32),
        "pool_w_grp": POOL_GROUP_WIDTH ** -0.5 * nrm(ks[11], (N_POOL_LAYERS, N_POOL_GROUPS, POOL_GROUP_WIDTH, POOL_GROUP_WIDTH), f32),
        "pool_scale": 1.0 + 0.1 * nrm(ks[12], (N_POOL_LAYERS, POOL_WIDTH), f32),
        "pool_w_out": POOL_WIDTH ** -0.5 * nrm(ks[13], (N_POOL_LAYERS, POOL_WIDTH, D_MODEL), f32),
        "attn_w_in": D_MODEL ** -0.5 * nrm(ks[14], (N_ATTN_LAYERS, D_MODEL, 2 * ATTN_WIDTH + 2 * KV_WIDTH), f32),
        "attn_sinks": nrm(ks[15], (N_ATTN_LAYERS, N_HEADS), f32),
        "attn_w_out": ATTN_WIDTH ** -0.5 * nrm(ks[16], (N_ATTN_LAYERS, ATTN_WIDTH, D_MODEL), f32),
        "final_g": 1.0 + 0.1 * nrm(ks[17], (D_MODEL,), f32),
    }


def reference(x_prompt, x_sample, c_prompt, c_sample, cache_pool, cache_k, cache_v,
              norm_g, ada_w, ada_b, pool_w_in, pool_w_grp, pool_scale, pool_w_out,
              attn_w_in, attn_sinks, attn_w_out, final_g):

    def trunk(x, c, sample):
        B = x.shape[0]
        pool_states, k_states, v_states = [], [], []
        for i in range(DEPTH):
            j = i // 2
            shift, scale, gate = modulation(c, ada_w[i], ada_b[i])
            h = rms_norm(x, norm_g[i]) * (1 + scale[:, None]) + shift[:, None]
            if i % 2 == 0:
                if sample:
                    hist = cache_pool[j]
                    hist_pos = PAST_LEN - POOL_HIST + jnp.arange(POOL_HIST, dtype=jnp.int32)
                else:
                    hist = jnp.zeros((B, POOL_HIST, POOL_WIDTH), h.dtype)
                    hist_pos = jnp.arange(POOL_HIST, dtype=jnp.int32) - POOL_HIST
                y, st = pool_mixer(h, hist, hist_pos, pool_w_in[j], pool_w_grp[j], pool_scale[j], pool_w_out[j])
                pool_states.append(st)
            else:
                if sample:
                    y, kw, vw = attn_sample(h, cache_k[j], cache_v[j], attn_w_in[j], attn_sinks[j], attn_w_out[j])
                else:
                    y, kw, vw = attn_prompt(h, attn_w_in[j], attn_sinks[j], attn_w_out[j])
                k_states.append(kw)
                v_states.append(vw)
            x = x + gate[:, None] * y
        return rms_norm(x, final_g), jnp.stack(pool_states), jnp.stack(k_states), jnp.stack(v_states)

    y_prompt, pool_p, k_p, v_p = trunk(x_prompt, c_prompt, False)
    y_sample, pool_s, k_s, v_s = trunk(x_sample, c_sample, True)
    return (y_prompt, y_sample, pool_p, k_p, v_p, pool_s, k_s, v_s)
```

```python
import functools

import numpy as np
import jax
import jax.numpy as jnp
from jax import lax
from jax.experimental import pallas as pl
from jax.experimental.pallas import tpu as pltpu

D_MODEL = 1024
DEPTH = 4
PAST_LEN = 1024
CHUNK = 64
N_HEADS = 16
HEAD_DIM = 64
N_KV_HEADS = 4
GQA_GROUP = N_HEADS // N_KV_HEADS
ATTN_WIDTH = N_HEADS * HEAD_DIM
KV_WIDTH = N_KV_HEADS * HEAD_DIM
WINDOW = 128
WINDOW_CHUNKS = WINDOW // CHUNK
POOL_WIDTH = D_MODEL
POOL_WINDOWS = (2, 4, 8, 16)
POOL_GROUP_WIDTH = POOL_WIDTH // len(POOL_WINDOWS)
POOL_HIST = max(POOL_WINDOWS) - 1
NORM_EPS = 1e-6
MASK_VALUE = -1e30

LANES = 128
SUBLANES = 8
VMEM_LIMIT_BYTES = 56 * 1024 * 1024

HIST_ROWS = 2 * SUBLANES
HEAD_PAIR = LANES // HEAD_DIM
BAND_KEYS = 2 * WINDOW
PROMPT_TILE = 512
QUERY_GROUP = 2 * CHUNK

_F32 = jnp.float32
_BF16 = jnp.bfloat16


def _alibi_slopes():
    h = np.arange(1, N_HEADS + 1, dtype=np.float32)
    return [float(s) for s in np.exp2(np.float32(-8.0) * h / np.float32(N_HEADS))]


def _rms(x):
    return x * lax.rsqrt(jnp.mean(x * x, axis=-1, keepdims=True) + NORM_EPS)


def _modulated_norm(x, g, shift, scale):
    return (_rms(x) * g) * (1.0 + scale) + shift


def _silu(x):
    return x * jax.nn.sigmoid(x)


def _dot(a, b):
    return jnp.dot(a, b, preferred_element_type=_F32)


def _dot_nt(a, b):
    return lax.dot_general(a, b, (((1,), (1,)), ((), ())), preferred_element_type=_F32)


def _mod_kernel(c_ref, w_ref, b_ref, o_ref):
    c = c_ref[...]
    o_ref[0] = _dot(_silu(c).astype(_BF16), w_ref[0].astype(_BF16)) + b_ref[0]


def _modulation(c_all, ada_w, ada_b):
    rows = c_all.shape[0]
    n_col_tiles = 3
    return pl.pallas_call(
        _mod_kernel,
        grid=(DEPTH, n_col_tiles),
        in_specs=[
            pl.BlockSpec((rows, D_MODEL), lambda i, n: (0, 0)),
            pl.BlockSpec((1, D_MODEL, D_MODEL), lambda i, n: (i, 0, n)),
            pl.BlockSpec((1, 1, D_MODEL), lambda i, n: (i, 0, n)),
        ],
        out_specs=pl.BlockSpec((1, rows, D_MODEL), lambda i, n: (i, 0, n)),
        out_shape=jax.ShapeDtypeStruct((DEPTH, rows, 3 * D_MODEL), _F32),
        compiler_params=pltpu.CompilerParams(
            dimension_semantics=("arbitrary", "arbitrary"), vmem_limit_bytes=VMEM_LIMIT_BYTES),
        name="adaln_modulation",
    )(c_all, ada_w, ada_b.reshape(DEPTH, 1, 3 * D_MODEL))


def _pool_kernel(*refs, tm, nt, sample):
    if sample:
        (x_ref, mod_ref, g_ref, win_ref, wgrp_ref, psc_ref, wout_ref, hist_ref,
         xo_ref, st_ref, ubuf, sgbuf, pbuf, zbuf) = refs
    else:
        (x_ref, mod_ref, g_ref, win_ref, wgrp_ref, psc_ref, wout_ref,
         xo_ref, st_ref, ubuf, sgbuf, pbuf, zbuf) = refs
    t = pl.program_id(1)

    @pl.when(t == 0)
    def _load_history():
        if sample:
            ubuf[0:HIST_ROWS, :] = hist_ref[0]
        else:
            ubuf[0:HIST_ROWS, :] = jnp.zeros((HIST_ROWS, POOL_WIDTH), _F32)

    x = x_ref[0]
    shift, scale, gmod = mod_ref[0, 0:1, :], mod_ref[0, 1:2, :], mod_ref[0, 2:3, :]
    hb = _modulated_norm(x, g_ref[...], shift, scale).astype(_BF16)
    ubuf[HIST_ROWS:HIST_ROWS + tm, :] = _dot(hb, win_ref[:, 0:POOL_WIDTH])
    sgbuf[...] = _silu(_dot(hb, win_ref[:, POOL_WIDTH:2 * POOL_WIDTH]))

    rc = min(CHUNK, tm)
    for c in range(tm // rc):
        base = HIST_ROWS + c * rc
        for g, w in enumerate(POOL_WINDOWS):
            cols = slice(g * POOL_GROUP_WIDTH, (g + 1) * POOL_GROUP_WIDTH)
            ext = ubuf[base - HIST_ROWS:base + rc, cols]
            s = ext
            k = 1
            while k < w:
                s = s + pltpu.roll(s, k, axis=0)
                k *= 2
            s, u = s[HIST_ROWS:], ext[HIST_ROWS:]
            if sample or c > 0:
                pooled = s * (1.0 / w) - u
            else:
                pos = t * tm + lax.broadcasted_iota(jnp.int32, (rc, POOL_GROUP_WIDTH), 0)
                pooled = s / jnp.minimum(pos + 1, w).astype(_F32) - u
            pbuf[c * rc:(c + 1) * rc, cols] = pooled.astype(_BF16)

    for g in range(len(POOL_WINDOWS)):
        cols = slice(g * POOL_GROUP_WIDTH, (g + 1) * POOL_GROUP_WIDTH)
        z = _dot(pbuf[:, cols], wgrp_ref[g]) * psc_ref[:, cols]
        zbuf[:, cols] = (z * sgbuf[:, cols]).astype(_BF16)
    y = _dot(zbuf[...], wout_ref[...])
    xo_ref[0] = x_ref[0] + gmod * y

    @pl.when(t == nt - 1)
    def _emit_state():
        st_ref[0] = ubuf[tm:tm + HIST_ROWS, :]

    ubuf[0:HIST_ROWS, :] = ubuf[tm:tm + HIST_ROWS, :]


def _pool_layer(x, mod, g, win, wgrp, psc, wout, hist, tm):
    B, T, D = x.shape
    nt = T // tm
    sample = hist is not None
    const = lambda *shape: pl.BlockSpec(shape, lambda b, t: (0,) * len(shape))
    in_specs = [
        pl.BlockSpec((1, tm, D), lambda b, t: (b, t, 0)),
        pl.BlockSpec((1, 3, D), lambda b, t: (b, 0, 0)),
        const(1, D),
        const(D, 2 * POOL_WIDTH),
        const(len(POOL_WINDOWS), POOL_GROUP_WIDTH, POOL_GROUP_WIDTH),
        const(1, POOL_WIDTH),
        const(POOL_WIDTH, D),
    ]
    args = [x, mod, g, win, wgrp, psc, wout]
    if sample:
        in_specs.append(pl.BlockSpec((1, HIST_ROWS, POOL_WIDTH), lambda b, t: (b, 0, 0)))
        args.append(hist)
    xo, st = pl.pallas_call(
        functools.partial(_pool_kernel, tm=tm, nt=nt, sample=sample),
        grid=(B, nt),
        in_specs=in_specs,
        out_specs=[
            pl.BlockSpec((1, tm, D), lambda b, t: (b, t, 0)),
            pl.BlockSpec((1, HIST_ROWS, POOL_WIDTH), lambda b, t: (b, 0, 0)),
        ],
        out_shape=[
            jax.ShapeDtypeStruct((B, T, D), _F32),
            jax.ShapeDtypeStruct((B, HIST_ROWS, POOL_WIDTH), _F32),
        ],
        scratch_shapes=[
            pltpu.VMEM((HIST_ROWS + tm, POOL_WIDTH), _F32),
            pltpu.VMEM((tm, POOL_WIDTH), _F32),
            pltpu.VMEM((tm, POOL_WIDTH), _BF16),
            pltpu.VMEM((tm, POOL_WIDTH), _BF16),
        ],
        compiler_params=pltpu.CompilerParams(
            dimension_semantics=("arbitrary", "arbitrary"), vmem_limit_bytes=VMEM_LIMIT_BYTES),
        name="pool_layer_sample" if sample else "pool_layer_prompt",
    )(*args)
    return xo, st[:, HIST_ROWS - POOL_HIST:]


def _store_split_heads(dst_lo, dst_hi, rows, val):
    n = val.shape[0]
    lo = lax.broadcasted_iota(jnp.int32, (n, LANES), 1) < HEAD_DIM
    zero = jnp.zeros((n, LANES), _F32)
    for j in range(KV_WIDTH // LANES):
        a = val[:, j * LANES:(j + 1) * LANES]
        ar = pltpu.roll(a, HEAD_DIM, axis=1)
        dst_lo[HEAD_PAIR * j, rows, :] = jnp.where(lo, a, zero).astype(_BF16)
        dst_hi[HEAD_PAIR * j, rows, :] = jnp.where(lo, zero, ar).astype(_BF16)
        dst_lo[HEAD_PAIR * j + 1, rows, :] = jnp.where(lo, ar, zero).astype(_BF16)
        dst_hi[HEAD_PAIR * j + 1, rows, :] = jnp.where(lo, zero, a).astype(_BF16)


def _attn_kernel(*refs, tm, nt, sample, final_norm):
    refs = list(refs)
    sink_ref, x_ref, mod_ref, g_ref, win_ref, wout_ref = refs[:6]
    del refs[:6]
    if sample:
        ck_ref, cv_ref = refs[:2]
        del refs[:2]
    if final_norm:
        fg_ref = refs.pop(0)
    xo_ref, kw_ref, vw_ref, qbuf, k_lo, k_hi, v_lo, v_hi, sgbuf, zbuf = refs
    t = pl.program_id(1)
    kv_bufs = (k_lo, k_hi, v_lo, v_hi)
    new_rows = slice(WINDOW, WINDOW + tm)

    @pl.when(t == 0)
    def _load_window():
        if sample:
            _store_split_heads(k_lo, k_hi, slice(0, WINDOW), ck_ref[0])
            _store_split_heads(v_lo, v_hi, slice(0, WINDOW), cv_ref[0])
            for buf in kv_bufs:
                buf[:, WINDOW + tm:BAND_KEYS, :] = jnp.zeros(
                    (N_KV_HEADS, BAND_KEYS - WINDOW - tm, LANES), _BF16)
        else:
            for buf in kv_bufs:
                buf[:, 0:WINDOW, :] = jnp.zeros((N_KV_HEADS, WINDOW, LANES), _BF16)

    x = x_ref[0]
    shift, scale, gmod = mod_ref[0, 0:1, :], mod_ref[0, 1:2, :], mod_ref[0, 2:3, :]
    hb = _modulated_norm(x, g_ref[...], shift, scale).astype(_BF16)
    q0, k0, v0, g0 = 0, ATTN_WIDTH, ATTN_WIDTH + KV_WIDTH, ATTN_WIDTH + 2 * KV_WIDTH
    qbuf[...] = (_dot(hb, win_ref[:, q0:k0]) * (HEAD_DIM ** -0.5)).astype(_BF16)
    k = _dot(hb, win_ref[:, k0:v0])
    v = _dot(hb, win_ref[:, v0:g0])
    sgbuf[...] = _silu(_dot(hb, win_ref[:, g0:g0 + ATTN_WIDTH]))
    _store_split_heads(k_lo, k_hi, new_rows, k)
    _store_split_heads(v_lo, v_hi, new_rows, v)

    @pl.when(t == nt - 1)
    def _emit_window():
        if sample:
            kw_ref[0, 0:WINDOW - tm, :] = ck_ref[0, tm:WINDOW, :]
            vw_ref[0, 0:WINDOW - tm, :] = cv_ref[0, tm:WINDOW, :]
            kw_ref[0, WINDOW - tm:WINDOW, :] = k
            vw_ref[0, WINDOW - tm:WINDOW, :] = v
        else:
            kw_ref[0] = k[tm - WINDOW:tm, :]
            vw_ref[0] = v[tm - WINDOW:tm, :]

    mq = tm if sample else QUERY_GROUP
    qi = lax.broadcasted_iota(jnp.int32, (mq, BAND_KEYS), 0)
    sj = lax.broadcasted_iota(jnp.int32, (mq, BAND_KEYS), 1)
    dist = jnp.abs(WINDOW + qi - sj).astype(_F32)
    qc = (WINDOW + qi) // CHUNK
    kc = sj // CHUNK
    band_ok = (kc <= qc) & (kc >= qc - WINDOW_CHUNKS)
    if sample:
        band_ok = band_ok & (sj < WINDOW + tm)
    lo_lanes = lax.broadcasted_iota(jnp.int32, (HEAD_PAIR * mq, LANES), 1) < HEAD_DIM
    slopes = _alibi_slopes()

    for grp in range(tm // mq):
        r0 = grp * mq
        qrows = slice(r0, r0 + mq)
        krows = slice(r0, r0 + BAND_KEYS)
        if sample or grp > 0:
            ok = band_ok
        else:
            ok = band_ok & ((sj >= WINDOW) | (t > 0))
        for kh in range(N_KV_HEADS):
            c0 = kh * GQA_GROUP * HEAD_DIM
            qs = jnp.concatenate(
                [qbuf[qrows, c0 + p * LANES:c0 + (p + 1) * LANES] for p in range(GQA_GROUP // HEAD_PAIR)], axis=0)
            probs, inv_l = [], []
            for parity, kbuf in enumerate((k_lo, k_hi)):
                s_all = _dot_nt(qs, kbuf[kh, krows, :])
                p_blocks, l_blocks = [], []
                for p in range(GQA_GROUP // HEAD_PAIR):
                    h = kh * GQA_GROUP + p * HEAD_PAIR + parity
                    sink = sink_ref[h]
                    s = jnp.where(ok, s_all[p * mq:(p + 1) * mq] - slopes[h] * dist, MASK_VALUE)
                    m = jnp.maximum(jnp.max(s, axis=-1, keepdims=True), sink)
                    e = jnp.exp(s - m)
                    l_blocks.append(jnp.sum(e, axis=-1, keepdims=True) + jnp.exp(sink - m))
                    p_blocks.append(e.astype(_BF16))
                probs.append(jnp.concatenate(p_blocks, axis=0))
                inv_l.append(1.0 / jnp.concatenate(l_blocks, axis=0))
            o = _dot(probs[0], v_lo[kh, krows, :]) + _dot(probs[1], v_hi[kh, krows, :])
            o = o * jnp.where(lo_lanes, inv_l[0], inv_l[1])
            for p in range(GQA_GROUP // HEAD_PAIR):
                cols = slice(c0 + p * LANES, c0 + (p + 1) * LANES)
                zbuf[qrows, cols] = (o[p * mq:(p + 1) * mq] * sgbuf[qrows, cols]).astype(_BF16)

    y = _dot(zbuf[...], wout_ref[...])
    xn = x_ref[0] + gmod * y
    if final_norm:
        xn = _rms(xn) * fg_ref[...]
    xo_ref[0] = xn

    if not sample:
        for buf in kv_bufs:
            buf[:, 0:WINDOW, :] = buf[:, tm:tm + WINDOW, :]


def _attn_layer(x, mod, g, win, sinks, wout, cache_k, cache_v, final_g, tm):
    B, T, D = x.shape
    nt = T // tm
    sample = cache_k is not None
    final_norm = final_g is not None
    const = lambda *shape: pl.BlockSpec(shape, lambda b, t: (0,) * len(shape))
    in_specs = [
        pl.BlockSpec(memory_space=pltpu.SMEM),
        pl.BlockSpec((1, tm, D), lambda b, t: (b, t, 0)),
        pl.BlockSpec((1, 3, D), lambda b, t: (b, 0, 0)),
        const(1, D),
        const(D, 2 * ATTN_WIDTH + 2 * KV_WIDTH),
        const(ATTN_WIDTH, D),
    ]
    args = [sinks, x, mod, g, win, wout]
    if sample:
        in_specs += [pl.BlockSpec((1, WINDOW, KV_WIDTH), lambda b, t: (b, 0, 0))] * 2
        args += [cache_k, cache_v]
    if final_norm:
        in_specs.append(const(1, D))
        args.append(final_g)
    key_rows = BAND_KEYS if sample else WINDOW + tm
    window_spec = pl.BlockSpec((1, WINDOW, KV_WIDTH), lambda b, t: (b, 0, 0))
    window_shape = jax.ShapeDtypeStruct((B, WINDOW, KV_WIDTH), _F32)
    return pl.pallas_call(
        functools.partial(_attn_kernel, tm=tm, nt=nt, sample=sample, final_norm=final_norm),
        grid=(B, nt),
        in_specs=in_specs,
        out_specs=[pl.BlockSpec((1, tm, D), lambda b, t: (b, t, 0)), window_spec, window_spec],
        out_shape=[jax.ShapeDtypeStruct((B, T, D), _F32), window_shape, window_shape],
        scratch_shapes=[
            pltpu.VMEM((tm, ATTN_WIDTH), _BF16),
            pltpu.VMEM((N_KV_HEADS, key_rows, LANES), _BF16),
            pltpu.VMEM((N_KV_HEADS, key_rows, LANES), _BF16),
            pltpu.VMEM((N_KV_HEADS, key_rows, LANES), _BF16),
            pltpu.VMEM((N_KV_HEADS, key_rows, LANES), _BF16),
            pltpu.VMEM((tm, ATTN_WIDTH), _F32),
            pltpu.VMEM((tm, ATTN_WIDTH), _BF16),
        ],
        compiler_params=pltpu.CompilerParams(
            dimension_semantics=("arbitrary", "arbitrary"), vmem_limit_bytes=VMEM_LIMIT_BYTES),
        name="attn_layer_sample" if sample else "attn_layer_prompt",
    )(*args)


def kernel(x_prompt, x_sample, c_prompt, c_sample, cache_pool, cache_k, cache_v, norm_g, ada_w, ada_b,
           pool_w_in, pool_w_grp, pool_scale, pool_w_out, attn_w_in, attn_sinks, attn_w_out, final_g):
    n_prompt, n_sample = x_prompt.shape[0], x_sample.shape[0]
    mods = _modulation(jnp.concatenate([c_prompt, c_sample], axis=0), ada_w, ada_b)
    mods = mods.reshape(DEPTH, n_prompt + n_sample, 3, D_MODEL)

    pool_w_in, pool_w_grp, pool_w_out = (w.astype(_BF16) for w in (pool_w_in, pool_w_grp, pool_w_out))
    attn_w_in, attn_w_out = attn_w_in.astype(_BF16), attn_w_out.astype(_BF16)
    hist = jnp.pad(cache_pool, ((0, 0), (0, 0), (HIST_ROWS - POOL_HIST, 0), (0, 0)))
    cache_k = cache_k.reshape(cache_k.shape[:3] + (KV_WIDTH,))
    cache_v = cache_v.reshape(cache_v.shape[:3] + (KV_WIDTH,))

    def trunk(x, mod, sample):
        tm = x.shape[1] if sample else min(PROMPT_TILE, x.shape[1])
        pool_states, k_states, v_states = [], [], []
        for i in range(DEPTH):
            j = i // 2
            g = norm_g[i].reshape(1, D_MODEL)
            if i % 2 == 0:
                x, st = _pool_layer(x, mod[i], g, pool_w_in[j], pool_w_grp[j], pool_scale[j].reshape(1, POOL_WIDTH),
                                    pool_w_out[j], hist[j] if sample else None, tm)
                pool_states.append(st)
            else:
                fg = final_g.reshape(1, D_MODEL) if i == DEPTH - 1 else None
                x, kw, vw = _attn_layer(x, mod[i], g, attn_w_in[j], attn_sinks[j], attn_w_out[j],
                                        cache_k[j] if sample else None, cache_v[j] if sample else None, fg, tm)
                k_states.append(kw.reshape(kw.shape[:2] + (N_KV_HEADS, HEAD_DIM)))
                v_states.append(vw.reshape(vw.shape[:2] + (N_KV_HEADS, HEAD_DIM)))
        return x, jnp.stack(pool_states), jnp.stack(k_states), jnp.stack(v_states)

    y_prompt, pool_p, k_p, v_p = trunk(x_prompt, mods[:, :n_prompt], False)
    y_sample, pool_s, k_s, v_s = trunk(x_sample, mods[:, n_prompt:], True)
    return (y_prompt, y_sample, pool_p, k_p, v_p, pool_s, k_s, v_s)
```

```python
import functools

import numpy as np
import jax
import jax.numpy as jnp
from jax import lax
from jax.experimental import pallas as pl
from jax.experimental.pallas import tpu as pltpu

D_MODEL = 1024
DEPTH = 4
PAST_LEN = 1024
CHUNK = 64
N_HEADS = 16
HEAD_DIM = 64
N_KV_HEADS = 4
GQA_GROUP = N_HEADS // N_KV_HEADS
ATTN_WIDTH = N_HEADS * HEAD_DIM
KV_WIDTH = N_KV_HEADS * HEAD_DIM
WINDOW = 128
WINDOW_CHUNKS = WINDOW // CHUNK
POOL_WIDTH = D_MODEL
POOL_WINDOWS = (2, 4, 8, 16)
POOL_GROUP_WIDTH = POOL_WIDTH // len(POOL_WINDOWS)
POOL_HIST = max(POOL_WINDOWS) - 1
NORM_EPS = 1e-6
MASK_VALUE = -1e30
LOG2_E = 1.4426950408889634

LANES = 128
SUBLANES = 8
VMEM_LIMIT_BYTES = 56 * 1024 * 1024

HIST_ROWS = 2 * SUBLANES
HEAD_PAIR = LANES // HEAD_DIM
BAND_KEYS = 2 * WINDOW
PROMPT_TILE = 512
QUERY_GROUP = 2 * CHUNK

_F32 = jnp.float32
_BF16 = jnp.bfloat16


def _alibi_slopes():
    h = np.arange(1, N_HEADS + 1, dtype=np.float32)
    return [float(s) for s in np.exp2(np.float32(-8.0) * h / np.float32(N_HEADS))]


def _rms(x):
    return x * lax.rsqrt(jnp.mean(x * x, axis=-1, keepdims=True) + NORM_EPS)


def _modulated_norm(x, g, shift, scale):
    return (_rms(x) * g) * (1.0 + scale) + shift


def _silu(x):
    return x * jax.nn.sigmoid(x)


def _dot(a, b):
    return jnp.dot(a, b, preferred_element_type=_F32)


def _dot_nt(a, b):
    return lax.dot_general(a, b, (((1,), (1,)), ((), ())), preferred_element_type=_F32)


def _mod_kernel(c_ref, w_ref, b_ref, o_ref):
    c = c_ref[...]
    o_ref[0] = _dot(_silu(c).astype(_BF16), w_ref[0].astype(_BF16)) + b_ref[0]


def _modulation(c_all, ada_w, ada_b):
    rows = c_all.shape[0]
    n_col_tiles = 3
    return pl.pallas_call(
        _mod_kernel,
        grid=(DEPTH, n_col_tiles),
        in_specs=[
            pl.BlockSpec((rows, D_MODEL), lambda i, n: (0, 0)),
            pl.BlockSpec((1, D_MODEL, D_MODEL), lambda i, n: (i, 0, n)),
            pl.BlockSpec((1, 1, D_MODEL), lambda i, n: (i, 0, n)),
        ],
        out_specs=pl.BlockSpec((1, rows, D_MODEL), lambda i, n: (i, 0, n)),
        out_shape=jax.ShapeDtypeStruct((DEPTH, rows, 3 * D_MODEL), _F32),
        compiler_params=pltpu.CompilerParams(
            dimension_semantics=("arbitrary", "arbitrary"), vmem_limit_bytes=VMEM_LIMIT_BYTES),
        name="adaln_modulation",
    )(c_all, ada_w, ada_b.reshape(DEPTH, 1, 3 * D_MODEL))


def _pool_kernel(*refs, tm, nt, sample):
    if sample:
        (x_ref, mod_ref, g_ref, win_ref, wgrp_ref, psc_ref, wout_ref, hist_ref,
         xo_ref, st_ref, ubuf, sgbuf, pbuf, zbuf) = refs
    else:
        (x_ref, mod_ref, g_ref, win_ref, wgrp_ref, psc_ref, wout_ref,
         xo_ref, st_ref, ubuf, sgbuf, pbuf, zbuf) = refs
    t = pl.program_id(1)

    @pl.when(t == 0)
    def _load_history():
        if sample:
            ubuf[0:HIST_ROWS, :] = hist_ref[0]
        else:
            ubuf[0:HIST_ROWS, :] = jnp.zeros((HIST_ROWS, POOL_WIDTH), _F32)

    x = x_ref[0]
    shift, scale, gmod = mod_ref[0, 0:1, :], mod_ref[0, 1:2, :], mod_ref[0, 2:3, :]
    hb = _modulated_norm(x, g_ref[...], shift, scale).astype(_BF16)
    ubuf[HIST_ROWS:HIST_ROWS + tm, :] = _dot(hb, win_ref[:, 0:POOL_WIDTH])
    sgbuf[...] = _silu(_dot(hb, win_ref[:, POOL_WIDTH:2 * POOL_WIDTH]))

    rc = min(CHUNK, tm)
    for c in range(tm // rc):
        base = HIST_ROWS + c * rc
        for g, w in enumerate(POOL_WINDOWS):
            cols = slice(g * POOL_GROUP_WIDTH, (g + 1) * POOL_GROUP_WIDTH)
            ext = ubuf[base - HIST_ROWS:base + rc, cols]
            s = ext
            k = 1
            while k < w:
                s = s + pltpu.roll(s, k, axis=0)
                k *= 2
            s, u = s[HIST_ROWS:], ext[HIST_ROWS:]
            if sample or c > 0:
                pooled = s * (1.0 / w) - u
            else:
                pos = t * tm + lax.broadcasted_iota(jnp.int32, (rc, POOL_GROUP_WIDTH), 0)
                pooled = s / jnp.minimum(pos + 1, w).astype(_F32) - u
            pbuf[c * rc:(c + 1) * rc, cols] = pooled.astype(_BF16)

    for g in range(len(POOL_WINDOWS)):
        cols = slice(g * POOL_GROUP_WIDTH, (g + 1) * POOL_GROUP_WIDTH)
        z = _dot(pbuf[:, cols], wgrp_ref[g]) * psc_ref[:, cols]
        zbuf[:, cols] = (z * sgbuf[:, cols]).astype(_BF16)
    y = _dot(zbuf[...], wout_ref[...])
    xo_ref[0] = x_ref[0] + gmod * y

    @pl.when(t == nt - 1)
    def _emit_state():
        st_ref[0] = ubuf[tm:tm + HIST_ROWS, :]

    ubuf[0:HIST_ROWS, :] = ubuf[tm:tm + HIST_ROWS, :]


def _pool_layer(x, mod, g, win, wgrp, psc, wout, hist, tm):
    B, T, D = x.shape
    nt = T // tm
    sample = hist is not None
    const = lambda *shape: pl.BlockSpec(shape, lambda b, t: (0,) * len(shape))
    in_specs = [
        pl.BlockSpec((1, tm, D), lambda b, t: (b, t, 0)),
        pl.BlockSpec((1, 3, D), lambda b, t: (b, 0, 0)),
        const(1, D),
        const(D, 2 * POOL_WIDTH),
        const(len(POOL_WINDOWS), POOL_GROUP_WIDTH, POOL_GROUP_WIDTH),
        const(1, POOL_WIDTH),
        const(POOL_WIDTH, D),
    ]
    args = [x, mod, g, win, wgrp, psc, wout]
    if sample:
        in_specs.append(pl.BlockSpec((1, HIST_ROWS, POOL_WIDTH), lambda b, t: (b, 0, 0)))
        args.append(hist)
    xo, st = pl.pallas_call(
        functools.partial(_pool_kernel, tm=tm, nt=nt, sample=sample),
        grid=(B, nt),
        in_specs=in_specs,
        out_specs=[
            pl.BlockSpec((1, tm, D), lambda b, t: (b, t, 0)),
            pl.BlockSpec((1, HIST_ROWS, POOL_WIDTH), lambda b, t: (b, 0, 0)),
        ],
        out_shape=[
            jax.ShapeDtypeStruct((B, T, D), _F32),
            jax.ShapeDtypeStruct((B, HIST_ROWS, POOL_WIDTH), _F32),
        ],
        scratch_shapes=[
            pltpu.VMEM((HIST_ROWS + tm, POOL_WIDTH), _F32),
            pltpu.VMEM((tm, POOL_WIDTH), _F32),
            pltpu.VMEM((tm, POOL_WIDTH), _BF16),
            pltpu.VMEM((tm, POOL_WIDTH), _BF16),
        ],
        compiler_params=pltpu.CompilerParams(
            dimension_semantics=("arbitrary", "arbitrary"), vmem_limit_bytes=VMEM_LIMIT_BYTES),
        name="pool_layer_sample" if sample else "pool_layer_prompt",
    )(*args)
    return xo, st[:, HIST_ROWS - POOL_HIST:]


def _store_split_heads(dst_lo, dst_hi, rows, val):
    n = val.shape[0]
    lo = lax.broadcasted_iota(jnp.int32, (n, LANES), 1) < HEAD_DIM
    zero = jnp.zeros((n, LANES), _F32)
    for j in range(KV_WIDTH // LANES):
        a = val[:, j * LANES:(j + 1) * LANES]
        ar = pltpu.roll(a, HEAD_DIM, axis=1)
        dst_lo[HEAD_PAIR * j, rows, :] = jnp.where(lo, a, zero).astype(_BF16)
        dst_hi[HEAD_PAIR * j, rows, :] = jnp.where(lo, zero, ar).astype(_BF16)
        dst_lo[HEAD_PAIR * j + 1, rows, :] = jnp.where(lo, ar, zero).astype(_BF16)
        dst_hi[HEAD_PAIR * j + 1, rows, :] = jnp.where(lo, zero, a).astype(_BF16)


def _attn_kernel(*refs, tm, nt, sample, final_norm):
    refs = list(refs)
    sink_ref, x_ref, mod_ref, g_ref, win_ref, wout_ref = refs[:6]
    del refs[:6]
    if sample:
        ck_ref, cv_ref = refs[:2]
        del refs[:2]
    if final_norm:
        fg_ref = refs.pop(0)
    xo_ref, kw_ref, vw_ref, qbuf, k_lo, k_hi, v_lo, v_hi, sgbuf, zbuf, bias_ref = refs
    t = pl.program_id(1)
    kv_bufs = (k_lo, k_hi, v_lo, v_hi)
    new_rows = slice(WINDOW, WINDOW + tm)

    @pl.when(t == 0)
    def _load_window():
        if sample:
            _store_split_heads(k_lo, k_hi, slice(0, WINDOW), ck_ref[0])
            _store_split_heads(v_lo, v_hi, slice(0, WINDOW), cv_ref[0])
            for buf in kv_bufs:
                buf[:, WINDOW + tm:BAND_KEYS, :] = jnp.zeros(
                    (N_KV_HEADS, BAND_KEYS - WINDOW - tm, LANES), _BF16)
        else:
            for buf in kv_bufs:
                buf[:, 0:WINDOW, :] = jnp.zeros((N_KV_HEADS, WINDOW, LANES), _BF16)

    x = x_ref[0]
    shift, scale, gmod = mod_ref[0, 0:1, :], mod_ref[0, 1:2, :], mod_ref[0, 2:3, :]
    hb = _modulated_norm(x, g_ref[...], shift, scale).astype(_BF16)
    q0, k0, v0, g0 = 0, ATTN_WIDTH, ATTN_WIDTH + KV_WIDTH, ATTN_WIDTH + 2 * KV_WIDTH
    qbuf[...] = (_dot(hb, win_ref[:, q0:k0]) * (HEAD_DIM ** -0.5 * LOG2_E)).astype(_BF16)
    k = _dot(hb, win_ref[:, k0:v0])
    v = _dot(hb, win_ref[:, v0:g0])
    sgbuf[...] = _silu(_dot(hb, win_ref[:, g0:g0 + ATTN_WIDTH]))
    _store_split_heads(k_lo, k_hi, new_rows, k)
    _store_split_heads(v_lo, v_hi, new_rows, v)

    @pl.when(t == nt - 1)
    def _emit_window():
        if sample:
            kw_ref[0, 0:WINDOW - tm, :] = ck_ref[0, tm:WINDOW, :]
            vw_ref[0, 0:WINDOW - tm, :] = cv_ref[0, tm:WINDOW, :]
            kw_ref[0, WINDOW - tm:WINDOW, :] = k
            vw_ref[0, WINDOW - tm:WINDOW, :] = v
        else:
            kw_ref[0] = k[tm - WINDOW:tm, :]
            vw_ref[0] = v[tm - WINDOW:tm, :]

    mq = tm if sample else QUERY_GROUP
    rb = min(CHUNK, mq)
    n_blocks = mq // rb

    @pl.when(t == 0)
    def _fill_alibi_bias():
        qi = lax.broadcasted_iota(jnp.int32, (mq, BAND_KEYS), 0)
        sj = lax.broadcasted_iota(jnp.int32, (mq, BAND_KEYS), 1)
        dist = jnp.abs(WINDOW + qi - sj).astype(_F32)
        for h, slope in enumerate(_alibi_slopes()):
            bias_ref[h] = dist * (-slope * LOG2_E)

    lane = lax.broadcasted_iota(jnp.int32, (rb, LANES), 1)
    lo_lanes = lax.broadcasted_iota(jnp.int32, (HEAD_PAIR * mq, LANES), 1) < HEAD_DIM
    key_lo_lanes = lax.broadcasted_iota(jnp.int32, (BAND_KEYS, LANES), 1) < HEAD_DIM
    ones_lo = jnp.where(key_lo_lanes, 1.0, 0.0).astype(_BF16)
    ones_hi = jnp.where(key_lo_lanes, 0.0, 1.0).astype(_BF16)

    def key_masks(grp, blk):
        qc = (WINDOW + blk * rb) // CHUNK
        masks = []
        for col in range(BAND_KEYS // LANES):
            conds = []
            for half in range(LANES // CHUNK):
                kc = (col * LANES) // CHUNK + half
                if not (qc - WINDOW_CHUNKS <= kc <= qc):
                    conds.append(lane >= CHUNK if half == 0 else lane < CHUNK)
            if sample and col * LANES + LANES > WINDOW + tm:
                conds.append(lane < WINDOW + tm - col * LANES)
            if not sample and grp == 0 and col * LANES < WINDOW:
                conds.append(t > 0)
            masks.append(functools.reduce(jnp.logical_and, conds) if conds else None)
        return masks

    for grp in range(tm // mq):
        r0 = grp * mq
        qrows = slice(r0, r0 + mq)
        krows = slice(r0, r0 + BAND_KEYS)
        masks = [key_masks(grp, blk) for blk in range(n_blocks)]
        for kh in range(N_KV_HEADS):
            c0 = kh * GQA_GROUP * HEAD_DIM
            qs = jnp.concatenate(
                [qbuf[qrows, c0 + p * LANES:c0 + (p + 1) * LANES] for p in range(GQA_GROUP // HEAD_PAIR)], axis=0)
            probs, sink_p = [], []
            for parity, kbuf in enumerate((k_lo, k_hi)):
                s_all = _dot_nt(qs, kbuf[kh, krows, :])
                p_blocks, sink_blocks = [], []
                for p in range(GQA_GROUP // HEAD_PAIR):
                    h = kh * GQA_GROUP + p * HEAD_PAIR + parity
                    sink2 = sink_ref[h] * LOG2_E
                    for blk in range(n_blocks):
                        rows = slice(p * mq + blk * rb, p * mq + (blk + 1) * rb)
                        cols = []
                        for col, allowed in enumerate(masks[blk]):
                            lanes = slice(col * LANES, (col + 1) * LANES)
                            s = s_all[rows, lanes] + bias_ref[h, blk * rb:(blk + 1) * rb, lanes]
                            cols.append(s if allowed is None else jnp.where(allowed, s, MASK_VALUE))
                        m = jnp.maximum(jnp.max(functools.reduce(jnp.maximum, cols), axis=-1, keepdims=True), sink2)
                        sink_blocks.append(jnp.exp2(sink2 - m))
                        p_blocks.append(jnp.concatenate([jnp.exp2(c - m).astype(_BF16) for c in cols], axis=1))
                probs.append(jnp.concatenate(p_blocks, axis=0))
                sink_p.append(jnp.concatenate(sink_blocks, axis=0))
            acc = (_dot(probs[0], jnp.concatenate([v_lo[kh, krows, :], ones_lo], axis=1))
                   + _dot(probs[1], jnp.concatenate([v_hi[kh, krows, :], ones_hi], axis=1)))
            o = acc[:, :LANES] / (acc[:, LANES:] + jnp.where(lo_lanes, sink_p[0], sink_p[1]))
            for p in range(GQA_GROUP // HEAD_PAIR):
                cols = slice(c0 + p * LANES, c0 + (p + 1) * LANES)
                zbuf[qrows, cols] = (o[p * mq:(p + 1) * mq] * sgbuf[qrows, cols]).astype(_BF16)

    y = _dot(zbuf[...], wout_ref[...])
    xn = x_ref[0] + gmod * y
    if final_norm:
        xn = _rms(xn) * fg_ref[...]
    xo_ref[0] = xn

    if not sample:
        for buf in kv_bufs:
            buf[:, 0:WINDOW, :] = buf[:, tm:tm + WINDOW, :]


def _attn_layer(x, mod, g, win, sinks, wout, cache_k, cache_v, final_g, tm):
    B, T, D = x.shape
    nt = T // tm
    sample = cache_k is not None
    final_norm = final_g is not None
    const = lambda *shape: pl.BlockSpec(shape, lambda b, t: (0,) * len(shape))
    in_specs = [
        pl.BlockSpec(memory_space=pltpu.SMEM),
        pl.BlockSpec((1, tm, D), lambda b, t: (b, t, 0)),
        pl.BlockSpec((1, 3, D), lambda b, t: (b, 0, 0)),
        const(1, D),
        const(D, 2 * ATTN_WIDTH + 2 * KV_WIDTH),
        const(ATTN_WIDTH, D),
    ]
    args = [sinks, x, mod, g, win, wout]
    if sample:
        in_specs += [pl.BlockSpec((1, WINDOW, KV_WIDTH), lambda b, t: (b, 0, 0))] * 2
        args += [cache_k, cache_v]
    if final_norm:
        in_specs.append(const(1, D))
        args.append(final_g)
    key_rows = BAND_KEYS if sample else WINDOW + tm
    window_spec = pl.BlockSpec((1, WINDOW, KV_WIDTH), lambda b, t: (b, 0, 0))
    window_shape = jax.ShapeDtypeStruct((B, WINDOW, KV_WIDTH), _F32)
    return pl.pallas_call(
        functools.partial(_attn_kernel, tm=tm, nt=nt, sample=sample, final_norm=final_norm),
        grid=(B, nt),
        in_specs=in_specs,
        out_specs=[pl.BlockSpec((1, tm, D), lambda b, t: (b, t, 0)), window_spec, window_spec],
        out_shape=[jax.ShapeDtypeStruct((B, T, D), _F32), window_shape, window_shape],
        scratch_shapes=[
            pltpu.VMEM((tm, ATTN_WIDTH), _BF16),
            pltpu.VMEM((N_KV_HEADS, key_rows, LANES), _BF16),
            pltpu.VMEM((N_KV_HEADS, key_rows, LANES), _BF16),
            pltpu.VMEM((N_KV_HEADS, key_rows, LANES), _BF16),
            pltpu.VMEM((N_KV_HEADS, key_rows, LANES), _BF16),
            pltpu.VMEM((tm, ATTN_WIDTH), _F32),
            pltpu.VMEM((tm, ATTN_WIDTH), _BF16),
            pltpu.VMEM((N_HEADS, tm if sample else QUERY_GROUP, BAND_KEYS), _F32),
        ],
        compiler_params=pltpu.CompilerParams(
            dimension_semantics=("arbitrary", "arbitrary"), vmem_limit_bytes=VMEM_LIMIT_BYTES),
        name="attn_layer_sample" if sample else "attn_layer_prompt",
    )(*args)


SUM_ROWS = 2 * SUBLANES
VT_ROWS = LANES + SUM_ROWS


def _dot_tn(a, b):
    return lax.dot_general(a, b, (((0,), (0,)), ((), ())), preferred_element_type=_F32)


def _attn_prompt_kernel(*refs, tm, nt, final_norm):
    refs = list(refs)
    (sink_ref, x_ref, mod_ref, g_ref, wqt_ref, wk_ref, wvt_ref, wgt_ref, wv_ref, wout_ref) = refs[:10]
    del refs[:10]
    if final_norm:
        fg_ref = refs.pop(0)
    xo_ref, kw_ref, vw_ref, qt, k_lo, k_hi, vt_lo, vt_hi, sgt, zt, bias_t = refs
    t = pl.program_id(1)
    n_pairs = GQA_GROUP // HEAD_PAIR
    slopes = _alibi_slopes()

    @pl.when(t == 0)
    def _reset_window():
        for buf in (k_lo, k_hi):
            buf[:, 0:WINDOW, :] = jnp.zeros((N_KV_HEADS, WINDOW, LANES), _BF16)
        row = lax.broadcasted_iota(jnp.int32, (VT_ROWS, WINDOW + tm), 0)
        even_ones = (row >= LANES) & (row < LANES + SUBLANES)
        odd_ones = row >= LANES + SUBLANES
        for kh in range(N_KV_HEADS):
            vt_lo[kh] = jnp.where(even_ones, 1.0, 0.0).astype(_BF16)
            vt_hi[kh] = jnp.where(odd_ones, 1.0, 0.0).astype(_BF16)
        si = lax.broadcasted_iota(jnp.int32, (BAND_KEYS, QUERY_GROUP), 0)
        qj = lax.broadcasted_iota(jnp.int32, (BAND_KEYS, QUERY_GROUP), 1)
        dist = jnp.abs(WINDOW + qj - si).astype(_F32)
        for h in range(N_HEADS):
            bias_t[h] = dist * (-slopes[h] * LOG2_E)

    x = x_ref[0]
    shift, scale, gmod = mod_ref[0, 0:1, :], mod_ref[0, 1:2, :], mod_ref[0, 2:3, :]
    hb = _modulated_norm(x, g_ref[...], shift, scale).astype(_BF16)
    qt[...] = (_dot_nt(wqt_ref[...], hb) * (HEAD_DIM ** -0.5 * LOG2_E)).astype(_BF16)
    k = _dot(hb, wk_ref[...])
    vt = _dot_nt(wvt_ref[...], hb)
    sgt[...] = _silu(_dot_nt(wgt_ref[...], hb))
    new = slice(WINDOW, WINDOW + tm)
    _store_split_heads(k_lo, k_hi, new, k)
    for kh in range(N_KV_HEADS):
        v_head = vt[kh * HEAD_DIM:(kh + 1) * HEAD_DIM, :].astype(_BF16)
        vt_lo[kh, 0:HEAD_DIM, new] = v_head
        vt_hi[kh, HEAD_DIM:2 * HEAD_DIM, new] = v_head

    @pl.when(t == nt - 1)
    def _emit_window():
        kw_ref[0] = k[tm - WINDOW:tm, :]
        vw_ref[0] = _dot(hb[tm - WINDOW:tm, :], wv_ref[...])

    lane = lax.broadcasted_iota(jnp.int32, (CHUNK, LANES), 1)

    def key_chunk_mask(grp, kc):
        conds = []
        for i in range(QUERY_GROUP // CHUNK):
            qc = WINDOW_CHUNKS + i
            if not (qc - WINDOW_CHUNKS <= kc <= qc):
                conds.append(lane >= CHUNK if i == 0 else lane < CHUNK)
        if grp == 0 and kc < WINDOW_CHUNKS:
            conds.append(t > 0)
        return functools.reduce(jnp.logical_and, conds) if conds else None


    def score_dots(grp, kh, st):
        r0, c0 = grp * QUERY_GROUP, kh * GQA_GROUP * HEAD_DIM
        q_lanes, band = slice(r0, r0 + QUERY_GROUP), slice(r0, r0 + BAND_KEYS)
        q_rhs = jnp.concatenate([qt[c0 + p * LANES:c0 + (p + 1) * LANES, q_lanes] for p in range(n_pairs)], axis=1)
        st["s_t"] = []
        for kbuf in (k_lo, k_hi):
            st["s_t"].append(_dot(kbuf[kh, band, :], q_rhs))
            yield

    def score_max(grp, kh, st):
        masks = [key_chunk_mask(grp, kc) for kc in range(BAND_KEYS // CHUNK)]
        st["blocks"], st["m"], st["sink_p"] = [], [], []
        for parity in range(HEAD_PAIR):
            for p in range(n_pairs):
                h = kh * GQA_GROUP + p * HEAD_PAIR + parity
                sink2 = sink_ref[h] * LOG2_E
                blocks = []
                for kc, allowed in enumerate(masks):
                    rows = slice(kc * CHUNK, (kc + 1) * CHUNK)
                    s = st["s_t"][parity][rows, p * QUERY_GROUP:(p + 1) * QUERY_GROUP] + bias_t[h, rows, :]
                    blocks.append(s if allowed is None else jnp.where(allowed, s, MASK_VALUE))
                m = jnp.max(functools.reduce(jnp.maximum, blocks), axis=0, keepdims=True)
                m = jnp.maximum(m, sink2)
                st["blocks"].append(blocks)
                st["m"].append(m)
                st["sink_p"].append(jnp.exp2(sink2 - m))
                yield

    def probabilities(grp, kh, st):
        st["p_t"] = []
        for blocks, m in zip(st["blocks"], st["m"]):
            st["p_t"].append(jnp.concatenate([jnp.exp2(b - m).astype(_BF16) for b in blocks], axis=0))
            yield

    def value_dots(grp, kh, st):
        band = slice(grp * QUERY_GROUP, grp * QUERY_GROUP + BAND_KEYS)
        st["acc"] = []
        for parity, vbuf in enumerate((vt_lo, vt_hi)):
            p_t = jnp.concatenate(st["p_t"][parity * n_pairs:(parity + 1) * n_pairs], axis=1)
            st["acc"].append(_dot(vbuf[kh, :, band], p_t))
            yield

    def finish(grp, kh, st):
        r0, c0 = grp * QUERY_GROUP, kh * GQA_GROUP * HEAD_DIM
        q_lanes = slice(r0, r0 + QUERY_GROUP)
        acc = st["acc"][0] + st["acc"][1]
        for p in range(n_pairs):
            lanes = slice(p * QUERY_GROUP, (p + 1) * QUERY_GROUP)
            inv_even = 1.0 / (acc[LANES:LANES + 1, lanes] + st["sink_p"][p])
            inv_odd = 1.0 / (acc[LANES + SUBLANES:LANES + SUBLANES + 1, lanes] + st["sink_p"][n_pairs + p])
            out = jnp.concatenate([acc[0:HEAD_DIM, lanes] * inv_even,
                                   acc[HEAD_DIM:2 * HEAD_DIM, lanes] * inv_odd], axis=0)
            rows = slice(c0 + p * LANES, c0 + (p + 1) * LANES)
            zt[rows, q_lanes] = (out * sgt[rows, q_lanes]).astype(_BF16)
            yield
        st.clear()

    stages = (score_dots, score_max, probabilities, value_dots, finish)
    steps = [(grp, kh) for grp in range(tm // QUERY_GROUP) for kh in range(N_KV_HEADS)]
    state = [{} for _ in steps]
    for r in range(len(steps) + len(stages) - 1):
        live = [stage(*steps[r - j], state[r - j]) for j, stage in enumerate(stages) if 0 <= r - j < len(steps)]
        while live:
            live = [g for g in live if next(g, live) is not live]

    y = _dot_tn(zt[...], wout_ref[...])
    xn = x_ref[0] + gmod * y
    if final_norm:
        xn = _rms(xn) * fg_ref[...]
    xo_ref[0] = xn

    for buf in (k_lo, k_hi):
        buf[:, 0:WINDOW, :] = buf[:, tm:tm + WINDOW, :]
    for buf in (vt_lo, vt_hi):
        buf[:, :, 0:WINDOW] = buf[:, :, tm:tm + WINDOW]


def _attn_prompt_layer(x, mod, g, win, sinks, wout, final_g, tm):
    B, T, D = x.shape
    nt = T // tm
    final_norm = final_g is not None
    q0, k0, v0, g0 = 0, ATTN_WIDTH, ATTN_WIDTH + KV_WIDTH, ATTN_WIDTH + 2 * KV_WIDTH
    wq_t, wk, wv = win[:, q0:k0].T, win[:, k0:v0], win[:, v0:g0]
    wv_t, wg_t = wv.T, win[:, g0:g0 + ATTN_WIDTH].T
    const = lambda *shape: pl.BlockSpec(shape, lambda b, t: (0,) * len(shape))
    in_specs = [
        pl.BlockSpec(memory_space=pltpu.SMEM),
        pl.BlockSpec((1, tm, D), lambda b, t: (b, t, 0)),
        pl.BlockSpec((1, 3, D), lambda b, t: (b, 0, 0)),
        const(1, D),
        const(ATTN_WIDTH, D), const(D, KV_WIDTH), const(KV_WIDTH, D), const(ATTN_WIDTH, D), const(D, KV_WIDTH),
        const(ATTN_WIDTH, D),
    ]
    args = [sinks, x, mod, g, wq_t, wk, wv_t, wg_t, wv, wout]
    if final_norm:
        in_specs.append(const(1, D))
        args.append(final_g)
    window_spec = pl.BlockSpec((1, WINDOW, KV_WIDTH), lambda b, t: (b, 0, 0))
    window_shape = jax.ShapeDtypeStruct((B, WINDOW, KV_WIDTH), _F32)
    return pl.pallas_call(
        functools.partial(_attn_prompt_kernel, tm=tm, nt=nt, final_norm=final_norm),
        grid=(B, nt),
        in_specs=in_specs,
        out_specs=[pl.BlockSpec((1, tm, D), lambda b, t: (b, t, 0)), window_spec, window_spec],
        out_shape=[jax.ShapeDtypeStruct((B, T, D), _F32), window_shape, window_shape],
        scratch_shapes=[
            pltpu.VMEM((ATTN_WIDTH, tm), _BF16),
            pltpu.VMEM((N_KV_HEADS, WINDOW + tm, LANES), _BF16),
            pltpu.VMEM((N_KV_HEADS, WINDOW + tm, LANES), _BF16),
            pltpu.VMEM((N_KV_HEADS, VT_ROWS, WINDOW + tm), _BF16),
            pltpu.VMEM((N_KV_HEADS, VT_ROWS, WINDOW + tm), _BF16),
            pltpu.VMEM((ATTN_WIDTH, tm), _F32),
            pltpu.VMEM((ATTN_WIDTH, tm), _BF16),
            pltpu.VMEM((N_HEADS, BAND_KEYS, QUERY_GROUP), _F32),
        ],
        compiler_params=pltpu.CompilerParams(
            dimension_semantics=("arbitrary", "arbitrary"), vmem_limit_bytes=VMEM_LIMIT_BYTES),
        name="attn_layer_prompt",
    )(*args)


def kernel(x_prompt, x_sample, c_prompt, c_sample, cache_pool, cache_k, cache_v, norm_g, ada_w, ada_b,
           pool_w_in, pool_w_grp, pool_scale, pool_w_out, attn_w_in, attn_sinks, attn_w_out, final_g):
    n_prompt, n_sample = x_prompt.shape[0], x_sample.shape[0]
    mods = _modulation(jnp.concatenate([c_prompt, c_sample], axis=0), ada_w, ada_b)
    mods = mods.reshape(DEPTH, n_prompt + n_sample, 3, D_MODEL)

    pool_w_in, pool_w_grp, pool_w_out = (w.astype(_BF16) for w in (pool_w_in, pool_w_grp, pool_w_out))
    attn_w_in, attn_w_out = attn_w_in.astype(_BF16), attn_w_out.astype(_BF16)
    hist = jnp.pad(cache_pool, ((0, 0), (0, 0), (HIST_ROWS - POOL_HIST, 0), (0, 0)))
    cache_k = cache_k.reshape(cache_k.shape[:3] + (KV_WIDTH,))
    cache_v = cache_v.reshape(cache_v.shape[:3] + (KV_WIDTH,))

    def trunk(x, mod, sample):
        tm = x.shape[1] if sample else min(PROMPT_TILE, x.shape[1])
        pool_states, k_states, v_states = [], [], []
        for i in range(DEPTH):
            j = i // 2
            g = norm_g[i].reshape(1, D_MODEL)
            if i % 2 == 0:
                x, st = _pool_layer(x, mod[i], g, pool_w_in[j], pool_w_grp[j], pool_scale[j].reshape(1, POOL_WIDTH),
                                    pool_w_out[j], hist[j] if sample else None, tm)
                pool_states.append(st)
            else:
                fg = final_g.reshape(1, D_MODEL) if i == DEPTH - 1 else None
                if sample:
                    x, kw, vw = _attn_layer(x, mod[i], g, attn_w_in[j], attn_sinks[j], attn_w_out[j],
                                            cache_k[j], cache_v[j], fg, tm)
                else:
                    x, kw, vw = _attn_prompt_layer(x, mod[i], g, attn_w_in[j], attn_sinks[j], attn_w_out[j], fg, tm)
                k_states.append(kw.reshape(kw.shape[:2] + (N_KV_HEADS, HEAD_DIM)))
                v_states.append(vw.reshape(vw.shape[:2] + (N_KV_HEADS, HEAD_DIM)))
        return x, jnp.stack(pool_states), jnp.stack(k_states), jnp.stack(v_states)

    y_prompt, pool_p, k_p, v_p = trunk(x_prompt, mods[:, :n_prompt], False)
    y_sample, pool_s, k_s, v_s = trunk(x_sample, mods[:, n_prompt:], True)
    return (y_prompt, y_sample, pool_p, k_p, v_p, pool_s, k_s, v_s)
```

```python
import functools

import numpy as np
import jax
import jax.numpy as jnp
from jax import lax
from jax.experimental import pallas as pl
from jax.experimental.pallas import tpu as pltpu

D_MODEL = 1024
DEPTH = 4
PAST_LEN = 1024
CHUNK = 64
N_HEADS = 16
HEAD_DIM = 64
N_KV_HEADS = 4
GQA_GROUP = N_HEADS // N_KV_HEADS
ATTN_WIDTH = N_HEADS * HEAD_DIM
KV_WIDTH = N_KV_HEADS * HEAD_DIM
WINDOW = 128
WINDOW_CHUNKS = WINDOW // CHUNK
POOL_WIDTH = D_MODEL
POOL_WINDOWS = (2, 4, 8, 16)
POOL_GROUP_WIDTH = POOL_WIDTH // len(POOL_WINDOWS)
POOL_HIST = max(POOL_WINDOWS) - 1
NORM_EPS = 1e-6
MASK_VALUE = -1e30
LOG2_E = 1.4426950408889634

LANES = 128
SUBLANES = 8
VMEM_LIMIT_BYTES = 56 * 1024 * 1024

HIST_ROWS = 2 * SUBLANES
HEAD_PAIR = LANES // HEAD_DIM
BAND_KEYS = 2 * WINDOW
POOL_PROMPT_TILE = 1024
ATTN_PROMPT_TILE = 1024
QUERY_GROUP = 2 * CHUNK
PROJ_ROWS = 256
PROJ_FEATURES = 512

_F32 = jnp.float32
_BF16 = jnp.bfloat16


def _alibi_slopes():
    h = np.arange(1, N_HEADS + 1, dtype=np.float32)
    return [float(s) for s in np.exp2(np.float32(-8.0) * h / np.float32(N_HEADS))]


def _rms(x):
    return x * lax.rsqrt(jnp.mean(x * x, axis=-1, keepdims=True) + NORM_EPS)


def _modulated_norm(x, g, shift, scale):
    return (_rms(x) * g) * (1.0 + scale) + shift


def _silu(x):
    h = 0.5 * x
    return h + h * jnp.tanh(h)


def _emit_round_robin(gens):
    live = list(gens)
    while live:
        live = [g for g in live if next(g, live) is not live]


def _dot(a, b):
    return jnp.dot(a, b, preferred_element_type=_F32)


def _dot_nt(a, b):
    return lax.dot_general(a, b, (((1,), (1,)), ((), ())), preferred_element_type=_F32)


def _mod_kernel(c_ref, w_ref, b_ref, o_ref):
    c = c_ref[...]
    o_ref[0] = _dot(_silu(c).astype(_BF16), w_ref[0].astype(_BF16)) + b_ref[0]


def _modulation(c_all, ada_w, ada_b):
    rows = c_all.shape[0]
    n_col_tiles = 3
    return pl.pallas_call(
        _mod_kernel,
        grid=(DEPTH, n_col_tiles),
        in_specs=[
            pl.BlockSpec((rows, D_MODEL), lambda i, n: (0, 0)),
            pl.BlockSpec((1, D_MODEL, D_MODEL), lambda i, n: (i, 0, n)),
            pl.BlockSpec((1, 1, D_MODEL), lambda i, n: (i, 0, n)),
        ],
        out_specs=pl.BlockSpec((1, rows, D_MODEL), lambda i, n: (i, 0, n)),
        out_shape=jax.ShapeDtypeStruct((DEPTH, rows, 3 * D_MODEL), _F32),
        compiler_params=pltpu.CompilerParams(
            dimension_semantics=("arbitrary", "arbitrary"), vmem_limit_bytes=VMEM_LIMIT_BYTES),
        name="adaln_modulation",
    )(c_all, ada_w, ada_b.reshape(DEPTH, 1, 3 * D_MODEL))


def _pool_kernel(*refs, tm, nt, sample):
    if sample:
        (x_ref, mod_ref, g_ref, win_ref, wgrp_ref, psc_ref, wout_ref, hist_ref,
         xo_ref, st_ref, ubuf, sgbuf, pbuf, zbuf, hbuf) = refs
    else:
        (x_ref, mod_ref, g_ref, win_ref, wgrp_ref, psc_ref, wout_ref,
         xo_ref, st_ref, ubuf, sgbuf, pbuf, zbuf, hbuf) = refs
    t = pl.program_id(1)

    @pl.when(t == 0)
    def _load_history():
        if sample:
            ubuf[0:HIST_ROWS, :] = hist_ref[0]
        else:
            ubuf[0:HIST_ROWS, :] = jnp.zeros((HIST_ROWS, POOL_WIDTH), _F32)

    shift, scale, gmod = mod_ref[0, 0:1, :], mod_ref[0, 1:2, :], mod_ref[0, 2:3, :]
    part = min(PROJ_ROWS, tm)
    rc = min(CHUNK, part)
    group_cols = [slice(g * POOL_GROUP_WIDTH, (g + 1) * POOL_GROUP_WIDTH) for g in range(len(POOL_WINDOWS))]


    def norm_part(p):
        for r0 in range(p * part, (p + 1) * part, rc):
            rows = slice(r0, r0 + rc)
            hbuf[rows, :] = _modulated_norm(x_ref[0, rows, :], g_ref[...], shift, scale).astype(_BF16)
            yield

    def project_part(p):
        rows = slice(p * part, (p + 1) * part)
        hb = hbuf[rows, :]
        ubuf[HIST_ROWS + p * part:HIST_ROWS + (p + 1) * part, :] = _dot(hb, win_ref[:, 0:POOL_WIDTH])
        yield
        sgbuf[rows, :] = _dot(hb, win_ref[:, POOL_WIDTH:2 * POOL_WIDTH])
        yield

    def pool_part(p):
        for r0 in range(p * part, (p + 1) * part, rc):
            rows = slice(r0, r0 + rc)
            sgbuf[rows, :] = _silu(sgbuf[rows, :])
            for cols, w in zip(group_cols, POOL_WINDOWS):
                ext = ubuf[r0:r0 + HIST_ROWS + rc, cols]
                s = ext
                k = 1
                while k < w:
                    s = s + pltpu.roll(s, k, axis=0)
                    k *= 2
                s, u = s[HIST_ROWS:], ext[HIST_ROWS:]
                if sample or r0 > 0:
                    pooled = s * (1.0 / w) - u
                else:
                    pos = t * tm + lax.broadcasted_iota(jnp.int32, (rc, POOL_GROUP_WIDTH), 0)
                    pooled = s / jnp.minimum(pos + 1, w).astype(_F32) - u
                pbuf[rows, cols] = pooled.astype(_BF16)
            yield

    def mix_part(p):
        rows = slice(p * part, (p + 1) * part)
        for g, cols in enumerate(group_cols):
            z = _dot(pbuf[rows, cols], wgrp_ref[g]) * psc_ref[:, cols]
            zbuf[rows, cols] = (z * sgbuf[rows, cols]).astype(_BF16)
            yield

    def output_part(p):
        rows = slice(p * part, (p + 1) * part)
        xo_ref[0, rows, :] = x_ref[0, rows, :] + gmod * _dot(zbuf[rows, :], wout_ref[...])
        yield

    stages = (norm_part, project_part, pool_part, mix_part, output_part)
    n_parts = tm // part
    for r in range(n_parts + len(stages) - 1):
        _emit_round_robin([stage(r - j) for j, stage in enumerate(stages) if 0 <= r - j < n_parts])

    @pl.when(t == nt - 1)
    def _emit_state():
        st_ref[0] = ubuf[tm:tm + HIST_ROWS, :]

    ubuf[0:HIST_ROWS, :] = ubuf[tm:tm + HIST_ROWS, :]


def _stacked(index, *shape):
    return pl.BlockSpec((None,) + shape, lambda b, t: (index,) + (0,) * len(shape))


def _mod_spec(i, row0):
    return pl.BlockSpec((None, 1, 3, D_MODEL), lambda b, t: (i, row0 + b, 0, 0))


def _pool_layer(x, mods, norm_g, win, wgrp, psc, wout, hist, *, i, row0, tm):
    B, T, D = x.shape
    nt = T // tm
    j = i // 2
    sample = hist is not None
    in_specs = [
        pl.BlockSpec((1, tm, D), lambda b, t: (b, t, 0)),
        _mod_spec(i, row0),
        _stacked(i, 1, D),
        _stacked(j, D, 2 * POOL_WIDTH),
        _stacked(j, len(POOL_WINDOWS), POOL_GROUP_WIDTH, POOL_GROUP_WIDTH),
        _stacked(j, 1, POOL_WIDTH),
        _stacked(j, POOL_WIDTH, D),
    ]
    args = [x, mods, norm_g, win, wgrp, psc, wout]
    if sample:
        in_specs.append(pl.BlockSpec((None, 1, HIST_ROWS, POOL_WIDTH), lambda b, t: (j, b, 0, 0)))
        args.append(hist)
    xo, st = pl.pallas_call(
        functools.partial(_pool_kernel, tm=tm, nt=nt, sample=sample),
        grid=(B, nt),
        in_specs=in_specs,
        out_specs=[
            pl.BlockSpec((1, tm, D), lambda b, t: (b, t, 0)),
            pl.BlockSpec((1, HIST_ROWS, POOL_WIDTH), lambda b, t: (b, 0, 0)),
        ],
        out_shape=[
            jax.ShapeDtypeStruct((B, T, D), _F32),
            jax.ShapeDtypeStruct((B, HIST_ROWS, POOL_WIDTH), _F32),
        ],
        scratch_shapes=[
            pltpu.VMEM((HIST_ROWS + tm, POOL_WIDTH), _F32),
            pltpu.VMEM((tm, POOL_WIDTH), _F32),
            pltpu.VMEM((tm, POOL_WIDTH), _BF16),
            pltpu.VMEM((tm, POOL_WIDTH), _BF16),
            pltpu.VMEM((tm, D), _BF16),
        ],
        compiler_params=pltpu.CompilerParams(
            dimension_semantics=("arbitrary", "arbitrary"), vmem_limit_bytes=VMEM_LIMIT_BYTES),
        name="pool_layer_sample" if sample else "pool_layer_prompt",
    )(*args)
    return xo, st[:, HIST_ROWS - POOL_HIST:]


def _store_split_heads(dst_lo, dst_hi, rows, val):
    n = val.shape[0]
    lo = lax.broadcasted_iota(jnp.int32, (n, LANES), 1) < HEAD_DIM
    zero = jnp.zeros((n, LANES), _F32)
    for j in range(KV_WIDTH // LANES):
        a = val[:, j * LANES:(j + 1) * LANES]
        ar = pltpu.roll(a, HEAD_DIM, axis=1)
        dst_lo[HEAD_PAIR * j, rows, :] = jnp.where(lo, a, zero).astype(_BF16)
        dst_hi[HEAD_PAIR * j, rows, :] = jnp.where(lo, zero, ar).astype(_BF16)
        dst_lo[HEAD_PAIR * j + 1, rows, :] = jnp.where(lo, ar, zero).astype(_BF16)
        dst_hi[HEAD_PAIR * j + 1, rows, :] = jnp.where(lo, zero, a).astype(_BF16)


def _attn_sample_kernel(*refs, tm, layer, final_norm):
    refs = list(refs)
    sink_ref, x_ref, mod_ref, g_ref, win_ref, wout_ref, ck_ref, cv_ref = refs[:8]
    del refs[:8]
    if final_norm:
        fg_ref = refs.pop(0)
    xo_ref, kw_ref, vw_ref, qbuf, k_lo, k_hi, v_lo, v_hi, sgbuf, zbuf, bias_ref = refs
    new_rows = slice(WINDOW, WINDOW + tm)

    _store_split_heads(k_lo, k_hi, slice(0, WINDOW), ck_ref[0])
    _store_split_heads(v_lo, v_hi, slice(0, WINDOW), cv_ref[0])
    for buf in (k_lo, k_hi, v_lo, v_hi):
        buf[:, WINDOW + tm:BAND_KEYS, :] = jnp.zeros((N_KV_HEADS, BAND_KEYS - WINDOW - tm, LANES), _BF16)

    x = x_ref[0]
    shift, scale, gmod = mod_ref[0, 0:1, :], mod_ref[0, 1:2, :], mod_ref[0, 2:3, :]
    hb = _modulated_norm(x, g_ref[...], shift, scale).astype(_BF16)
    q0, k0, v0, g0 = 0, ATTN_WIDTH, ATTN_WIDTH + KV_WIDTH, ATTN_WIDTH + 2 * KV_WIDTH
    qbuf[...] = (_dot(hb, win_ref[:, q0:k0]) * (HEAD_DIM ** -0.5 * LOG2_E)).astype(_BF16)
    k = _dot(hb, win_ref[:, k0:v0])
    v = _dot(hb, win_ref[:, v0:g0])
    sgbuf[...] = _silu(_dot(hb, win_ref[:, g0:g0 + ATTN_WIDTH]))
    _store_split_heads(k_lo, k_hi, new_rows, k)
    _store_split_heads(v_lo, v_hi, new_rows, v)

    kw_ref[0, 0:WINDOW - tm, :] = ck_ref[0, tm:WINDOW, :]
    vw_ref[0, 0:WINDOW - tm, :] = cv_ref[0, tm:WINDOW, :]
    kw_ref[0, WINDOW - tm:WINDOW, :] = k
    vw_ref[0, WINDOW - tm:WINDOW, :] = v

    qi = lax.broadcasted_iota(jnp.int32, (tm, BAND_KEYS), 0)
    sj = lax.broadcasted_iota(jnp.int32, (tm, BAND_KEYS), 1)
    dist = jnp.abs(WINDOW + qi - sj).astype(_F32)
    for h, slope in enumerate(_alibi_slopes()):
        bias_ref[h] = dist * (-slope * LOG2_E)

    lane = lax.broadcasted_iota(jnp.int32, (tm, LANES), 1)
    lo_lanes = lax.broadcasted_iota(jnp.int32, (HEAD_PAIR * tm, LANES), 1) < HEAD_DIM
    key_lo_lanes = lax.broadcasted_iota(jnp.int32, (BAND_KEYS, LANES), 1) < HEAD_DIM
    ones_lo = jnp.where(key_lo_lanes, 1.0, 0.0).astype(_BF16)
    ones_hi = jnp.where(key_lo_lanes, 0.0, 1.0).astype(_BF16)

    qc = WINDOW // CHUNK
    masks = []
    for col in range(BAND_KEYS // LANES):
        conds = []
        for half in range(LANES // CHUNK):
            kc = (col * LANES) // CHUNK + half
            if not (qc - WINDOW_CHUNKS <= kc <= qc):
                conds.append(lane >= CHUNK if half == 0 else lane < CHUNK)
        if col * LANES + LANES > WINDOW + tm:
            conds.append(lane < WINDOW + tm - col * LANES)
        masks.append(functools.reduce(jnp.logical_and, conds) if conds else None)

    n_pairs = GQA_GROUP // HEAD_PAIR
    for kh in range(N_KV_HEADS):
        c0 = kh * GQA_GROUP * HEAD_DIM
        qs = jnp.concatenate([qbuf[:, c0 + p * LANES:c0 + (p + 1) * LANES] for p in range(n_pairs)], axis=0)
        probs, sink_p = [], []
        for parity, kbuf in enumerate((k_lo, k_hi)):
            s_all = _dot_nt(qs, kbuf[kh])
            p_blocks, sink_blocks = [], []
            for p in range(n_pairs):
                h = kh * GQA_GROUP + p * HEAD_PAIR + parity
                sink2 = sink_ref[layer, h] * LOG2_E
                cols = []
                for col, allowed in enumerate(masks):
                    lanes = slice(col * LANES, (col + 1) * LANES)
                    s = s_all[p * tm:(p + 1) * tm, lanes] + bias_ref[h, :, lanes]
                    cols.append(s if allowed is None else jnp.where(allowed, s, MASK_VALUE))
                m = jnp.maximum(jnp.max(functools.reduce(jnp.maximum, cols), axis=-1, keepdims=True), sink2)
                sink_blocks.append(jnp.exp2(sink2 - m))
                p_blocks.append(jnp.concatenate([jnp.exp2(c - m).astype(_BF16) for c in cols], axis=1))
            probs.append(jnp.concatenate(p_blocks, axis=0))
            sink_p.append(jnp.concatenate(sink_blocks, axis=0))
        acc = (_dot(probs[0], jnp.concatenate([v_lo[kh], ones_lo], axis=1))
               + _dot(probs[1], jnp.concatenate([v_hi[kh], ones_hi], axis=1)))
        o = acc[:, :LANES] / (acc[:, LANES:] + jnp.where(lo_lanes, sink_p[0], sink_p[1]))
        for p in range(n_pairs):
            cols = slice(c0 + p * LANES, c0 + (p + 1) * LANES)
            zbuf[:, cols] = (o[p * tm:(p + 1) * tm] * sgbuf[:, cols]).astype(_BF16)

    y = _dot(zbuf[...], wout_ref[...])
    xn = x_ref[0] + gmod * y
    if final_norm:
        xn = _rms(xn) * fg_ref[...]
    xo_ref[0] = xn


def _attn_sample_layer(x, mods, norm_g, win, sinks, wout, cache_k, cache_v, final_g, *, i, row0):
    B, tm, D = x.shape
    assert tm < CHUNK and PAST_LEN % CHUNK == 0
    j = i // 2
    final_norm = final_g is not None
    in_specs = [
        pl.BlockSpec(memory_space=pltpu.SMEM),
        pl.BlockSpec((1, tm, D), lambda b, t: (b, 0, 0)),
        _mod_spec(i, row0),
        _stacked(i, 1, D),
        _stacked(j, D, 2 * ATTN_WIDTH + 2 * KV_WIDTH),
        _stacked(j, ATTN_WIDTH, D),
        pl.BlockSpec((None, 1, WINDOW, KV_WIDTH), lambda b, t: (j, b, 0, 0)),
        pl.BlockSpec((None, 1, WINDOW, KV_WIDTH), lambda b, t: (j, b, 0, 0)),
    ]
    args = [sinks, x, mods, norm_g, win, wout, cache_k, cache_v]
    if final_norm:
        in_specs.append(pl.BlockSpec((1, D), lambda b, t: (0, 0)))
        args.append(final_g)
    window_spec = pl.BlockSpec((1, WINDOW, KV_WIDTH), lambda b, t: (b, 0, 0))
    window_shape = jax.ShapeDtypeStruct((B, WINDOW, KV_WIDTH), _F32)
    return pl.pallas_call(
        functools.partial(_attn_sample_kernel, tm=tm, layer=j, final_norm=final_norm),
        grid=(B, 1),
        in_specs=in_specs,
        out_specs=[pl.BlockSpec((1, tm, D), lambda b, t: (b, 0, 0)), window_spec, window_spec],
        out_shape=[jax.ShapeDtypeStruct((B, tm, D), _F32), window_shape, window_shape],
        scratch_shapes=[
            pltpu.VMEM((tm, ATTN_WIDTH), _BF16),
            pltpu.VMEM((N_KV_HEADS, BAND_KEYS, LANES), _BF16),
            pltpu.VMEM((N_KV_HEADS, BAND_KEYS, LANES), _BF16),
            pltpu.VMEM((N_KV_HEADS, BAND_KEYS, LANES), _BF16),
            pltpu.VMEM((N_KV_HEADS, BAND_KEYS, LANES), _BF16),
            pltpu.VMEM((tm, ATTN_WIDTH), _F32),
            pltpu.VMEM((tm, ATTN_WIDTH), _BF16),
            pltpu.VMEM((N_HEADS, tm, BAND_KEYS), _F32),
        ],
        compiler_params=pltpu.CompilerParams(
            dimension_semantics=("arbitrary", "arbitrary"), vmem_limit_bytes=VMEM_LIMIT_BYTES),
        name="attn_layer_sample",
    )(*args)


SUM_ROWS = 2 * SUBLANES
VT_ROWS = LANES + SUM_ROWS


def _dot_tn(a, b):
    return lax.dot_general(a, b, (((0,), (0,)), ((), ())), preferred_element_type=_F32)


def _attn_prompt_kernel(*refs, tm, nt, layer, final_norm):
    refs = list(refs)
    (sink_ref, x_ref, mod_ref, g_ref, wqt_ref, wk_ref, wvt_ref, wgt_ref, wv_ref, wout_ref) = refs[:10]
    del refs[:10]
    if final_norm:
        fg_ref = refs.pop(0)
    xo_ref, kw_ref, vw_ref, qt, k_lo, k_hi, vt_lo, vt_hi, sgt, zt, bias_t, hbuf = refs
    t = pl.program_id(1)
    n_pairs = GQA_GROUP // HEAD_PAIR
    slopes = _alibi_slopes()

    @pl.when(t == 0)
    def _reset_window():
        for buf in (k_lo, k_hi):
            buf[:, 0:WINDOW, :] = jnp.zeros((N_KV_HEADS, WINDOW, LANES), _BF16)
        row = lax.broadcasted_iota(jnp.int32, (VT_ROWS, WINDOW + tm), 0)
        even_ones = (row >= LANES) & (row < LANES + SUBLANES)
        odd_ones = row >= LANES + SUBLANES
        for kh in range(N_KV_HEADS):
            vt_lo[kh] = jnp.where(even_ones, 1.0, 0.0).astype(_BF16)
            vt_hi[kh] = jnp.where(odd_ones, 1.0, 0.0).astype(_BF16)
        si = lax.broadcasted_iota(jnp.int32, (BAND_KEYS, QUERY_GROUP), 0)
        qj = lax.broadcasted_iota(jnp.int32, (BAND_KEYS, QUERY_GROUP), 1)
        dist = jnp.abs(WINDOW + qj - si).astype(_F32)
        for h in range(N_HEADS):
            bias_t[h] = dist * (-slopes[h] * LOG2_E)

    shift, scale, gmod = mod_ref[0, 0:1, :], mod_ref[0, 1:2, :], mod_ref[0, 2:3, :]
    n_parts = tm // PROJ_ROWS


    run_region = _emit_round_robin

    def norm_part(j):
        for r0 in range(j * PROJ_ROWS, (j + 1) * PROJ_ROWS, CHUNK):
            rows = slice(r0, r0 + CHUNK)
            hbuf[rows, :] = _modulated_norm(x_ref[0, rows, :], g_ref[...], shift, scale).astype(_BF16)
            yield

    def project_part(j):
        rows = slice(j * PROJ_ROWS, (j + 1) * PROJ_ROWS)
        new = slice(WINDOW + j * PROJ_ROWS, WINDOW + (j + 1) * PROJ_ROWS)
        hb = hbuf[rows, :]
        for f0 in range(0, ATTN_WIDTH, PROJ_FEATURES):
            feats = slice(f0, f0 + PROJ_FEATURES)
            sgt[feats, rows] = _dot_nt(wgt_ref[feats, :], hb)
            yield
        for f0 in range(0, ATTN_WIDTH, PROJ_FEATURES):
            feats = slice(f0, f0 + PROJ_FEATURES)
            qt[feats, rows] = (_dot_nt(wqt_ref[feats, :], hb) * (HEAD_DIM ** -0.5 * LOG2_E)).astype(_BF16)
            yield
        vt = _dot_nt(wvt_ref[...], hb)
        for kh in range(N_KV_HEADS):
            v_head = vt[kh * HEAD_DIM:(kh + 1) * HEAD_DIM, :].astype(_BF16)
            vt_lo[kh, 0:HEAD_DIM, new] = v_head
            vt_hi[kh, HEAD_DIM:2 * HEAD_DIM, new] = v_head
        yield
        _store_split_heads(k_lo, k_hi, new, _dot(hb, wk_ref[...]))
        yield

    def silu_part(j, f_lo=0, f_hi=ATTN_WIDTH):
        rows = slice(j * PROJ_ROWS, (j + 1) * PROJ_ROWS)
        for f0 in range(f_lo, f_hi, LANES):
            feats = slice(f0, f0 + LANES)
            sgt[feats, rows] = _silu(sgt[feats, rows])
            yield

    def output_part(grp):
        rows = slice(grp * QUERY_GROUP, (grp + 1) * QUERY_GROUP)
        y = _dot_tn(zt[:, rows], wout_ref[...])
        xn = x_ref[0, rows, :] + gmod * y
        if final_norm:
            xn = _rms(xn) * fg_ref[...]
        xo_ref[0, rows, :] = xn
        yield

    lane = lax.broadcasted_iota(jnp.int32, (CHUNK, LANES), 1)

    def key_chunk_mask(grp, kc):
        conds = []
        for i in range(QUERY_GROUP // CHUNK):
            qc = WINDOW_CHUNKS + i
            if not (qc - WINDOW_CHUNKS <= kc <= qc):
                conds.append(lane >= CHUNK if i == 0 else lane < CHUNK)
        if grp == 0 and kc < WINDOW_CHUNKS:
            conds.append(t > 0)
        return functools.reduce(jnp.logical_and, conds) if conds else None


    def score_dots(grp, kh, st):
        r0, c0 = grp * QUERY_GROUP, kh * GQA_GROUP * HEAD_DIM
        q_lanes, band = slice(r0, r0 + QUERY_GROUP), slice(r0, r0 + BAND_KEYS)
        q_rhs = jnp.concatenate([qt[c0 + p * LANES:c0 + (p + 1) * LANES, q_lanes] for p in range(n_pairs)], axis=1)
        st["s_t"] = []
        for kbuf in (k_lo, k_hi):
            st["s_t"].append(_dot(kbuf[kh, band, :], q_rhs))
            yield

    def score_max(grp, kh, st):
        masks = [key_chunk_mask(grp, kc) for kc in range(BAND_KEYS // CHUNK)]
        st["blocks"], st["m"], st["sink_p"] = [], [], []
        for parity in range(HEAD_PAIR):
            for p in range(n_pairs):
                h = kh * GQA_GROUP + p * HEAD_PAIR + parity
                sink2 = sink_ref[layer, h] * LOG2_E
                blocks = []
                for kc, allowed in enumerate(masks):
                    rows = slice(kc * CHUNK, (kc + 1) * CHUNK)
                    s = st["s_t"][parity][rows, p * QUERY_GROUP:(p + 1) * QUERY_GROUP] + bias_t[h, rows, :]
                    blocks.append(s if allowed is None else jnp.where(allowed, s, MASK_VALUE))
                m = jnp.max(functools.reduce(jnp.maximum, blocks), axis=0, keepdims=True)
                m = jnp.maximum(m, sink2)
                st["blocks"].append(blocks)
                st["m"].append(m)
                st["sink_p"].append(jnp.exp2(sink2 - m))
                yield

    def probabilities(grp, kh, st):
        st["p_t"] = []
        for blocks, m in zip(st["blocks"], st["m"]):
            st["p_t"].append(jnp.concatenate([jnp.exp2(b - m).astype(_BF16) for b in blocks], axis=0))
            yield

    def value_dots(grp, kh, st):
        band = slice(grp * QUERY_GROUP, grp * QUERY_GROUP + BAND_KEYS)
        st["acc"] = []
        for parity, vbuf in enumerate((vt_lo, vt_hi)):
            p_t = jnp.concatenate(st["p_t"][parity * n_pairs:(parity + 1) * n_pairs], axis=1)
            st["acc"].append(_dot(vbuf[kh, :, band], p_t))
            yield

    def finish(grp, kh, st):
        r0, c0 = grp * QUERY_GROUP, kh * GQA_GROUP * HEAD_DIM
        q_lanes = slice(r0, r0 + QUERY_GROUP)
        acc = st["acc"][0] + st["acc"][1]
        for p in range(n_pairs):
            lanes = slice(p * QUERY_GROUP, (p + 1) * QUERY_GROUP)
            inv_even = 1.0 / (acc[LANES:LANES + 1, lanes] + st["sink_p"][p])
            inv_odd = 1.0 / (acc[LANES + SUBLANES:LANES + SUBLANES + 1, lanes] + st["sink_p"][n_pairs + p])
            out = jnp.concatenate([acc[0:HEAD_DIM, lanes] * inv_even,
                                   acc[HEAD_DIM:2 * HEAD_DIM, lanes] * inv_odd], axis=0)
            rows = slice(c0 + p * LANES, c0 + (p + 1) * LANES)
            zt[rows, q_lanes] = (out * sgt[rows, q_lanes]).astype(_BF16)
            yield
        st.clear()

    stages = (score_dots, score_max, probabilities, value_dots, finish)
    steps = [(grp, kh) for grp in range(tm // QUERY_GROUP) for kh in range(N_KV_HEADS)]
    state = [{} for _ in steps]
    n_regions = len(steps) + len(stages) - 1
    fill = len(stages) - 1
    per_part = len(steps) // n_parts

    def advance(gen, n):
        for _ in range(n):
            if next(gen, gen) is gen:
                return
            yield

    def spread(gen, first, count, per_region):
        for r in range(first, first + count):
            extras[r].append(advance(gen, per_region))

    extras = [[] for _ in range(n_regions + 1)]
    for j in range(n_parts):
        r0 = j * per_part
        if j > 0:
            spread(silu_part(j), r0, fill, ATTN_WIDTH // LANES // fill)
        if j + 1 < n_parts:
            n_dots = 2 * (ATTN_WIDTH // PROJ_FEATURES) + 2
            spread(project_part(j + 1), r0, per_part - 2, pl.cdiv(n_dots, per_part - 2))
        if j + 2 < n_parts:
            spread(norm_part(j + 2), r0 + per_part - 2, 2, PROJ_ROWS // CHUNK // 2)
    for grp in range(tm // QUERY_GROUP):
        extras[min((grp + 1) * N_KV_HEADS + fill, n_regions)].append(output_part(grp))

    run_region([norm_part(0)])
    first = project_part(0)
    run_region([advance(first, ATTN_WIDTH // PROJ_FEATURES)] + ([norm_part(1)] if n_parts > 1 else []))
    run_region([first, silu_part(0)])
    for r in range(n_regions):
        run_region([stage(*steps[r - j], state[r - j]) for j, stage in enumerate(stages) if 0 <= r - j < len(steps)]
                   + extras[r])
    run_region(extras[n_regions])

    @pl.when(t == nt - 1)
    def _emit_window():
        last = hbuf[tm - WINDOW:tm, :]
        kw_ref[0] = _dot(last, wk_ref[...])
        vw_ref[0] = _dot(last, wv_ref[...])

    for buf in (k_lo, k_hi):
        buf[:, 0:WINDOW, :] = buf[:, tm:tm + WINDOW, :]
    for buf in (vt_lo, vt_hi):
        buf[:, :, 0:WINDOW] = buf[:, :, tm:tm + WINDOW]


def _split_attn_w_in(win):
    q0, k0, v0, g0 = 0, ATTN_WIDTH, ATTN_WIDTH + KV_WIDTH, ATTN_WIDTH + 2 * KV_WIDTH
    t = lambda w: jnp.swapaxes(w, 1, 2)
    wk, wv = win[:, :, k0:v0], win[:, :, v0:g0]
    return t(win[:, :, q0:k0]), wk, t(wv), t(win[:, :, g0:g0 + ATTN_WIDTH]), wv


def _attn_prompt_layer(x, mods, norm_g, win_parts, sinks, wout, final_g, *, i, row0, tm):
    B, T, D = x.shape
    nt = T // tm
    j = i // 2
    final_norm = final_g is not None
    in_specs = [
        pl.BlockSpec(memory_space=pltpu.SMEM),
        pl.BlockSpec((1, tm, D), lambda b, t: (b, t, 0)),
        _mod_spec(i, row0),
        _stacked(i, 1, D),
        _stacked(j, ATTN_WIDTH, D), _stacked(j, D, KV_WIDTH), _stacked(j, KV_WIDTH, D), _stacked(j, ATTN_WIDTH, D),
        _stacked(j, D, KV_WIDTH),
        _stacked(j, ATTN_WIDTH, D),
    ]
    args = [sinks, x, mods, norm_g, *win_parts, wout]
    if final_norm:
        in_specs.append(pl.BlockSpec((1, D), lambda b, t: (0, 0)))
        args.append(final_g)
    window_spec = pl.BlockSpec((1, WINDOW, KV_WIDTH), lambda b, t: (b, 0, 0))
    window_shape = jax.ShapeDtypeStruct((B, WINDOW, KV_WIDTH), _F32)
    return pl.pallas_call(
        functools.partial(_attn_prompt_kernel, tm=tm, nt=nt, layer=j, final_norm=final_norm),
        grid=(B, nt),
        in_specs=in_specs,
        out_specs=[pl.BlockSpec((1, tm, D), lambda b, t: (b, t, 0)), window_spec, window_spec],
        out_shape=[jax.ShapeDtypeStruct((B, T, D), _F32), window_shape, window_shape],
        scratch_shapes=[
            pltpu.VMEM((ATTN_WIDTH, tm), _BF16),
            pltpu.VMEM((N_KV_HEADS, WINDOW + tm, LANES), _BF16),
            pltpu.VMEM((N_KV_HEADS, WINDOW + tm, LANES), _BF16),
            pltpu.VMEM((N_KV_HEADS, VT_ROWS, WINDOW + tm), _BF16),
            pltpu.VMEM((N_KV_HEADS, VT_ROWS, WINDOW + tm), _BF16),
            pltpu.VMEM((ATTN_WIDTH, tm), _F32),
            pltpu.VMEM((ATTN_WIDTH, tm), _BF16),
            pltpu.VMEM((N_HEADS, BAND_KEYS, QUERY_GROUP), _F32),
            pltpu.VMEM((tm, D), _BF16),
        ],
        compiler_params=pltpu.CompilerParams(
            dimension_semantics=("arbitrary", "arbitrary"), vmem_limit_bytes=VMEM_LIMIT_BYTES),
        name="attn_layer_prompt",
    )(*args)


def kernel(x_prompt, x_sample, c_prompt, c_sample, cache_pool, cache_k, cache_v, norm_g, ada_w, ada_b,
           pool_w_in, pool_w_grp, pool_scale, pool_w_out, attn_w_in, attn_sinks, attn_w_out, final_g):
    n_prompt, n_sample = x_prompt.shape[0], x_sample.shape[0]
    mods = _modulation(jnp.concatenate([c_prompt, c_sample], axis=0), ada_w, ada_b)
    mods = mods.reshape(DEPTH, n_prompt + n_sample, 3, D_MODEL)

    norm_g = norm_g.reshape(DEPTH, 1, D_MODEL)
    final_g = final_g.reshape(1, D_MODEL)
    pool_w_in, pool_w_grp, pool_w_out = (w.astype(_BF16) for w in (pool_w_in, pool_w_grp, pool_w_out))
    pool_scale = pool_scale.reshape(-1, 1, POOL_WIDTH)
    attn_w_in, attn_w_out = attn_w_in.astype(_BF16), attn_w_out.astype(_BF16)
    attn_w_in_parts = _split_attn_w_in(attn_w_in)
    hist = jnp.pad(cache_pool, ((0, 0), (0, 0), (HIST_ROWS - POOL_HIST, 0), (0, 0)))
    cache_k = cache_k.reshape(cache_k.shape[:3] + (KV_WIDTH,))
    cache_v = cache_v.reshape(cache_v.shape[:3] + (KV_WIDTH,))

    def trunk(x, row0, sample):
        rows = x.shape[1]
        pool_states, k_states, v_states = [], [], []
        for i in range(DEPTH):
            if i % 2 == 0:
                x, st = _pool_layer(x, mods, norm_g, pool_w_in, pool_w_grp, pool_scale, pool_w_out,
                                    hist if sample else None, i=i, row0=row0,
                                    tm=rows if sample else min(POOL_PROMPT_TILE, rows))
                pool_states.append(st)
            else:
                fg = final_g if i == DEPTH - 1 else None
                if sample:
                    x, kw, vw = _attn_sample_layer(x, mods, norm_g, attn_w_in, attn_sinks, attn_w_out,
                                                   cache_k, cache_v, fg, i=i, row0=row0)
                else:
                    x, kw, vw = _attn_prompt_layer(x, mods, norm_g, attn_w_in_parts, attn_sinks, attn_w_out, fg,
                                                   i=i, row0=row0, tm=min(ATTN_PROMPT_TILE, rows))
                k_states.append(kw.reshape(kw.shape[:2] + (N_KV_HEADS, HEAD_DIM)))
                v_states.append(vw.reshape(vw.shape[:2] + (N_KV_HEADS, HEAD_DIM)))
        return x, jnp.stack(pool_states), jnp.stack(k_states), jnp.stack(v_states)

    y_prompt, pool_p, k_p, v_p = trunk(x_prompt, 0, False)
    y_sample, pool_s, k_s, v_s = trunk(x_sample, n_prompt, True)
    return (y_prompt, y_sample, pool_p, k_p, v_p, pool_s, k_s, v_s)
```

```python
import functools

import numpy as np
import jax
import jax.numpy as jnp
from jax import lax
from jax.experimental import pallas as pl
from jax.experimental.pallas import tpu as pltpu

D_MODEL = 1024
DEPTH = 4
PAST_LEN = 1024
CHUNK = 64
N_HEADS = 16
HEAD_DIM = 64
N_KV_HEADS = 4
GQA_GROUP = N_HEADS // N_KV_HEADS
ATTN_WIDTH = N_HEADS * HEAD_DIM
KV_WIDTH = N_KV_HEADS * HEAD_DIM
WINDOW = 128
WINDOW_CHUNKS = WINDOW // CHUNK
POOL_WIDTH = D_MODEL
POOL_WINDOWS = (2, 4, 8, 16)
POOL_GROUP_WIDTH = POOL_WIDTH // len(POOL_WINDOWS)
POOL_HIST = max(POOL_WINDOWS) - 1
NORM_EPS = 1e-6
MASK_VALUE = -1e30
LOG2_E = 1.4426950408889634

LANES = 128
SUBLANES = 8
VMEM_LIMIT_BYTES = 56 * 1024 * 1024

HIST_ROWS = 2 * SUBLANES
HEAD_PAIR = LANES // HEAD_DIM
BAND_KEYS = 2 * WINDOW
POOL_PROMPT_TILE = 1024
ATTN_PROMPT_TILE = 512
QUERY_GROUP = 2 * CHUNK
PROJ_ROWS = 256
PROJ_FEATURES = 512

_F32 = jnp.float32
_BF16 = jnp.bfloat16


def _alibi_slopes():
    h = np.arange(1, N_HEADS + 1, dtype=np.float32)
    return [float(s) for s in np.exp2(np.float32(-8.0) * h / np.float32(N_HEADS))]


def _rms(x):
    return x * lax.rsqrt(jnp.mean(x * x, axis=-1, keepdims=True) + NORM_EPS)


def _modulated_norm(x, g, shift, scale):
    return (_rms(x) * g) * (1.0 + scale) + shift


def _silu(x):
    h = 0.5 * x
    return h + h * jnp.tanh(h)


def _emit_round_robin(gens):
    live = list(gens)
    while live:
        live = [g for g in live if next(g, live) is not live]


def _dot(a, b):
    return jnp.dot(a, b, preferred_element_type=_F32)


def _dot_nt(a, b):
    return lax.dot_general(a, b, (((1,), (1,)), ((), ())), preferred_element_type=_F32)


def _mod_kernel(c_ref, w_ref, b_ref, o_ref):
    c = c_ref[...]
    o_ref[0] = _dot(_silu(c).astype(_BF16), w_ref[0].astype(_BF16)) + b_ref[0]


def _modulation(c_all, ada_w, ada_b):
    rows = c_all.shape[0]
    n_col_tiles = 3
    return pl.pallas_call(
        _mod_kernel,
        grid=(DEPTH, n_col_tiles),
        in_specs=[
            pl.BlockSpec((rows, D_MODEL), lambda i, n: (0, 0)),
            pl.BlockSpec((1, D_MODEL, D_MODEL), lambda i, n: (i, 0, n)),
            pl.BlockSpec((1, 1, D_MODEL), lambda i, n: (i, 0, n)),
        ],
        out_specs=pl.BlockSpec((1, rows, D_MODEL), lambda i, n: (i, 0, n)),
        out_shape=jax.ShapeDtypeStruct((DEPTH, rows, 3 * D_MODEL), _F32),
        compiler_params=pltpu.CompilerParams(
            dimension_semantics=("arbitrary", "arbitrary"), vmem_limit_bytes=VMEM_LIMIT_BYTES),
        name="adaln_modulation",
    )(c_all, ada_w, ada_b.reshape(DEPTH, 1, 3 * D_MODEL))


def _pool_kernel(*refs, n_seg, seg, nt, sample, n_carried):
    refs = list(refs)
    x_ref, mod_ref, g_ref, win_ref, wgrp_ref, psc_ref, wout_ref = refs[:7]
    del refs[:7]
    if sample:
        hist_ref = refs.pop(0)
    del refs[:n_carried]
    xo_ref, st_ref, ubuf, sgbuf, pbuf, zbuf, hbuf = refs
    t = pl.program_id(1)
    tm = n_seg * seg

    @pl.when(t == 0)
    def _load_history():
        if sample:
            ubuf[:, 0:HIST_ROWS, :] = hist_ref[...]
        else:
            ubuf[:, 0:HIST_ROWS, :] = jnp.zeros((n_seg, HIST_ROWS, POOL_WIDTH), _F32)

    part = min(PROJ_ROWS, tm)
    piece = min(part, seg)
    rc = min(CHUNK, piece)
    group_cols = [slice(g * POOL_GROUP_WIDTH, (g + 1) * POOL_GROUP_WIDTH) for g in range(len(POOL_WINDOWS))]

    def pieces(p, n):
        return [(r // seg, r % seg, r - p * part) for r in range(p * part, (p + 1) * part, n)]


    def norm_part(p):
        for b, o, i in pieces(p, rc):
            h = _modulated_norm(x_ref[b, o:o + rc, :], g_ref[...], mod_ref[b, 0:1, :], mod_ref[b, 1:2, :])
            hbuf[p * part + i:p * part + i + rc, :] = h.astype(_BF16)
            yield

    def project_part(p):
        rows = slice(p * part, (p + 1) * part)
        hb = hbuf[rows, :]
        u = _dot(hb, win_ref[:, 0:POOL_WIDTH])
        for b, o, i in pieces(p, piece):
            ubuf[b, HIST_ROWS + o:HIST_ROWS + o + piece, :] = u[i:i + piece]
        yield
        sgbuf[rows, :] = _dot(hb, win_ref[:, POOL_WIDTH:2 * POOL_WIDTH])
        yield

    def pool_part(p):
        for b, o, i in pieces(p, rc):
            rows = slice(p * part + i, p * part + i + rc)
            sgbuf[rows, :] = _silu(sgbuf[rows, :])
            for cols, w in zip(group_cols, POOL_WINDOWS):
                ext = ubuf[b, o:o + HIST_ROWS + rc, cols]
                s = ext
                k = 1
                while k < w:
                    s = s + pltpu.roll(s, k, axis=0)
                    k *= 2
                s, u = s[HIST_ROWS:], ext[HIST_ROWS:]
                if sample or o > 0:
                    pooled = s * (1.0 / w) - u
                else:
                    pos = t * seg + lax.broadcasted_iota(jnp.int32, (rc, POOL_GROUP_WIDTH), 0)
                    pooled = s / jnp.minimum(pos + 1, w).astype(_F32) - u
                pbuf[rows, cols] = pooled.astype(_BF16)
            yield

    def mix_part(p):
        rows = slice(p * part, (p + 1) * part)
        for g, cols in enumerate(group_cols):
            z = _dot(pbuf[rows, cols], wgrp_ref[g]) * psc_ref[:, cols]
            zbuf[rows, cols] = (z * sgbuf[rows, cols]).astype(_BF16)
            yield

    def output_part(p):
        y = _dot(zbuf[p * part:(p + 1) * part, :], wout_ref[...])
        for b, o, i in pieces(p, piece):
            xo_ref[b, o:o + piece, :] = x_ref[b, o:o + piece, :] + mod_ref[b, 2:3, :] * y[i:i + piece]
        yield

    stages = (norm_part, project_part, pool_part, mix_part, output_part)
    n_parts = tm // part
    for r in range(n_parts + len(stages) - 1):
        _emit_round_robin([stage(r - j) for j, stage in enumerate(stages) if 0 <= r - j < n_parts])

    @pl.when(t == nt - 1)
    def _emit_state():
        st_ref[...] = ubuf[:, seg:seg + HIST_ROWS, :]

    ubuf[:, 0:HIST_ROWS, :] = ubuf[:, seg:seg + HIST_ROWS, :]


def _stacked(index, *shape):
    return pl.BlockSpec((None,) + shape, lambda b, t: (index,) + (0,) * len(shape))


def _mod_spec(i, row0):
    return pl.BlockSpec((None, 1, 3, D_MODEL), lambda b, t: (i, row0 + b, 0, 0))


def _carry(carried, first_output):
    carried = list(carried or ())
    specs = [pl.BlockSpec(memory_space=pl.ANY)] * len(carried)
    return specs, carried, lambda n_in: {n_in + k: first_output + k for k in range(len(carried))}


def _pool_layer(x, mods, norm_g, win, wgrp, psc, wout, hist, states, *, i, row0, tm):
    B, T, D = x.shape
    j = i // 2
    n_layers = win.shape[0]
    sample = hist is not None
    n_seg, seg = (B, T) if sample else (1, tm)
    grid = (B // n_seg, T // seg)
    assert row0 % n_seg == 0
    in_specs = [
        pl.BlockSpec((n_seg, seg, D), lambda b, t: (b, t, 0)),
        pl.BlockSpec((None, n_seg, 3, D), lambda b, t: (i, row0 // n_seg + b, 0, 0)),
        _stacked(i, 1, D),
        _stacked(j, D, 2 * POOL_WIDTH),
        _stacked(j, len(POOL_WINDOWS), POOL_GROUP_WIDTH, POOL_GROUP_WIDTH),
        _stacked(j, 1, POOL_WIDTH),
        _stacked(j, POOL_WIDTH, D),
    ]
    args = [x, mods, norm_g, win, wgrp, psc, wout]
    if sample:
        in_specs.append(pl.BlockSpec((None, n_seg, HIST_ROWS, POOL_WIDTH), lambda b, t: (j, b, 0, 0)))
        args.append(hist)
    tm = n_seg * seg
    carry_specs, carry_args, aliases = _carry(None if states is None else [states], first_output=1)
    return pl.pallas_call(
        functools.partial(_pool_kernel, n_seg=n_seg, seg=seg, nt=grid[1], sample=sample, n_carried=len(carry_args)),
        grid=grid,
        in_specs=in_specs + carry_specs,
        out_specs=[
            pl.BlockSpec((n_seg, seg, D), lambda b, t: (b, t, 0)),
            pl.BlockSpec((None, n_seg, HIST_ROWS, POOL_WIDTH), lambda b, t: (j, b, 0, 0)),
        ],
        out_shape=[
            jax.ShapeDtypeStruct((B, T, D), _F32),
            jax.ShapeDtypeStruct((n_layers, B, HIST_ROWS, POOL_WIDTH), _F32),
        ],
        input_output_aliases=aliases(len(args)),
        scratch_shapes=[
            pltpu.VMEM((n_seg, HIST_ROWS + seg, POOL_WIDTH), _F32),
            pltpu.VMEM((tm, POOL_WIDTH), _F32),
            pltpu.VMEM((tm, POOL_WIDTH), _BF16),
            pltpu.VMEM((tm, POOL_WIDTH), _BF16),
            pltpu.VMEM((tm, D), _BF16),
        ],
        compiler_params=pltpu.CompilerParams(
            dimension_semantics=("arbitrary", "arbitrary"), vmem_limit_bytes=VMEM_LIMIT_BYTES),
        name="pool_layer_sample" if sample else "pool_layer_prompt",
    )(*args, *carry_args)


def _store_split_heads(dst_lo, dst_hi, rows, val):
    n = val.shape[0]
    lo = lax.broadcasted_iota(jnp.int32, (n, LANES), 1) < HEAD_DIM
    zero = jnp.zeros((n, LANES), _F32)
    for j in range(KV_WIDTH // LANES):
        a = val[:, j * LANES:(j + 1) * LANES]
        ar = pltpu.roll(a, HEAD_DIM, axis=1)
        dst_lo[HEAD_PAIR * j, rows, :] = jnp.where(lo, a, zero).astype(_BF16)
        dst_hi[HEAD_PAIR * j, rows, :] = jnp.where(lo, zero, ar).astype(_BF16)
        dst_lo[HEAD_PAIR * j + 1, rows, :] = jnp.where(lo, ar, zero).astype(_BF16)
        dst_hi[HEAD_PAIR * j + 1, rows, :] = jnp.where(lo, zero, a).astype(_BF16)


def _attn_sample_kernel(*refs, n_batch, tm, layer, final_norm, n_carried):
    refs = list(refs)
    sink_ref, x_ref, mod_ref, g_ref, win_ref, wout_ref, ck_ref, cv_ref = refs[:8]
    del refs[:8]
    if final_norm:
        fg_ref = refs.pop(0)
    del refs[:n_carried]
    xo_ref, kw_ref, vw_ref, qbuf, k_lo, k_hi, v_lo, v_hi, sgbuf, zbuf, bias_ref, hbuf = refs
    batch_rows = [slice(b * tm, (b + 1) * tm) for b in range(n_batch)]

    for b, rows in enumerate(batch_rows):
        h = _modulated_norm(x_ref[b], g_ref[...], mod_ref[b, 0:1, :], mod_ref[b, 1:2, :])
        hbuf[rows, :] = h.astype(_BF16)
    hb = hbuf[...]
    q0, k0, v0, g0 = 0, ATTN_WIDTH, ATTN_WIDTH + KV_WIDTH, ATTN_WIDTH + 2 * KV_WIDTH
    qbuf[...] = (_dot(hb, win_ref[:, q0:k0]) * (HEAD_DIM ** -0.5 * LOG2_E)).astype(_BF16)
    k = _dot(hb, win_ref[:, k0:v0])
    v = _dot(hb, win_ref[:, v0:g0])
    sgbuf[...] = _silu(_dot(hb, win_ref[:, g0:g0 + ATTN_WIDTH]))
    for b, rows in enumerate(batch_rows):
        band0 = b * BAND_KEYS
        _store_split_heads(k_lo, k_hi, slice(band0, band0 + WINDOW), ck_ref[b])
        _store_split_heads(v_lo, v_hi, slice(band0, band0 + WINDOW), cv_ref[b])
        _store_split_heads(k_lo, k_hi, slice(band0 + WINDOW, band0 + WINDOW + tm), k[rows])
        _store_split_heads(v_lo, v_hi, slice(band0 + WINDOW, band0 + WINDOW + tm), v[rows])
        for buf in (k_lo, k_hi, v_lo, v_hi):
            buf[:, band0 + WINDOW + tm:band0 + BAND_KEYS, :] = jnp.zeros(
                (N_KV_HEADS, BAND_KEYS - WINDOW - tm, LANES), _BF16)
        kw_ref[b, 0:WINDOW - tm, :] = ck_ref[b, tm:WINDOW, :]
        vw_ref[b, 0:WINDOW - tm, :] = cv_ref[b, tm:WINDOW, :]
        kw_ref[b, WINDOW - tm:WINDOW, :] = k[rows]
        vw_ref[b, WINDOW - tm:WINDOW, :] = v[rows]

    qi = lax.broadcasted_iota(jnp.int32, (tm, BAND_KEYS), 0)
    sj = lax.broadcasted_iota(jnp.int32, (tm, BAND_KEYS), 1)
    dist = jnp.abs(WINDOW + qi - sj).astype(_F32)
    for h, slope in enumerate(_alibi_slopes()):
        bias_ref[h] = dist * (-slope * LOG2_E)

    lane = lax.broadcasted_iota(jnp.int32, (tm, LANES), 1)
    lo_lanes = lax.broadcasted_iota(jnp.int32, (HEAD_PAIR * tm, LANES), 1) < HEAD_DIM
    key_lo_lanes = lax.broadcasted_iota(jnp.int32, (BAND_KEYS, LANES), 1) < HEAD_DIM
    ones_lo = jnp.where(key_lo_lanes, 1.0, 0.0).astype(_BF16)
    ones_hi = jnp.where(key_lo_lanes, 0.0, 1.0).astype(_BF16)

    qc = WINDOW // CHUNK
    masks = []
    for col in range(BAND_KEYS // LANES):
        conds = []
        for half in range(LANES // CHUNK):
            kc = (col * LANES) // CHUNK + half
            if not (qc - WINDOW_CHUNKS <= kc <= qc):
                conds.append(lane >= CHUNK if half == 0 else lane < CHUNK)
        if col * LANES + LANES > WINDOW + tm:
            conds.append(lane < WINDOW + tm - col * LANES)
        masks.append(functools.reduce(jnp.logical_and, conds) if conds else None)

    n_pairs = GQA_GROUP // HEAD_PAIR

    def score_dots(b, kh, st):
        c0 = kh * GQA_GROUP * HEAD_DIM
        qs = jnp.concatenate(
            [qbuf[batch_rows[b], c0 + p * LANES:c0 + (p + 1) * LANES] for p in range(n_pairs)], axis=0)
        st["s"] = [_dot_nt(qs, kbuf[kh, b * BAND_KEYS:(b + 1) * BAND_KEYS, :]) for kbuf in (k_lo, k_hi)]
        yield

    def probabilities(b, kh, st):
        st["probs"], st["sink_p"] = [], []
        for parity in range(HEAD_PAIR):
            p_blocks, sink_blocks = [], []
            for p in range(n_pairs):
                h = kh * GQA_GROUP + p * HEAD_PAIR + parity
                sink2 = sink_ref[layer, h] * LOG2_E
                cols = []
                for col, allowed in enumerate(masks):
                    lanes = slice(col * LANES, (col + 1) * LANES)
                    s = st["s"][parity][p * tm:(p + 1) * tm, lanes] + bias_ref[h, :, lanes]
                    cols.append(s if allowed is None else jnp.where(allowed, s, MASK_VALUE))
                m = jnp.maximum(jnp.max(functools.reduce(jnp.maximum, cols), axis=-1, keepdims=True), sink2)
                sink_blocks.append(jnp.exp2(sink2 - m))
                p_blocks.append(jnp.concatenate([jnp.exp2(c - m).astype(_BF16) for c in cols], axis=1))
                yield
            st["probs"].append(jnp.concatenate(p_blocks, axis=0))
            st["sink_p"].append(jnp.concatenate(sink_blocks, axis=0))

    def value_dots(b, kh, st):
        band = slice(b * BAND_KEYS, (b + 1) * BAND_KEYS)
        st["acc"] = (_dot(st["probs"][0], jnp.concatenate([v_lo[kh, band, :], ones_lo], axis=1))
                     + _dot(st["probs"][1], jnp.concatenate([v_hi[kh, band, :], ones_hi], axis=1)))
        yield

    def finish(b, kh, st):
        rows, c0 = batch_rows[b], kh * GQA_GROUP * HEAD_DIM
        acc = st["acc"]
        o = acc[:, :LANES] / (acc[:, LANES:] + jnp.where(lo_lanes, st["sink_p"][0], st["sink_p"][1]))
        for p in range(n_pairs):
            cols = slice(c0 + p * LANES, c0 + (p + 1) * LANES)
            zbuf[rows, cols] = (o[p * tm:(p + 1) * tm] * sgbuf[rows, cols]).astype(_BF16)
        st.clear()
        yield

    stages = (score_dots, probabilities, value_dots, finish)
    steps = [(b, kh) for b in range(n_batch) for kh in range(N_KV_HEADS)]
    state = [{} for _ in steps]
    for r in range(len(steps) + len(stages) - 1):
        _emit_round_robin(
            [stage(*steps[r - j], state[r - j]) for j, stage in enumerate(stages) if 0 <= r - j < len(steps)])

    y = _dot(zbuf[...], wout_ref[...])
    for b, rows in enumerate(batch_rows):
        xn = x_ref[b] + mod_ref[b, 2:3, :] * y[rows]
        if final_norm:
            xn = _rms(xn) * fg_ref[...]
        xo_ref[b] = xn


def _attn_sample_layer(x, mods, norm_g, win, sinks, wout, cache_k, cache_v, final_g, windows, *, i, row0):
    B, tm, D = x.shape
    assert tm < CHUNK and PAST_LEN % CHUNK == 0 and row0 % B == 0
    j = i // 2
    n_layers = win.shape[0]
    final_norm = final_g is not None
    whole = lambda *shape: pl.BlockSpec(shape, lambda b, t: (0,) * len(shape))
    in_specs = [
        pl.BlockSpec(memory_space=pltpu.SMEM),
        whole(B, tm, D),
        pl.BlockSpec((None, B, 3, D), lambda b, t: (i, row0 // B, 0, 0)),
        _stacked(i, 1, D),
        _stacked(j, D, 2 * ATTN_WIDTH + 2 * KV_WIDTH),
        _stacked(j, ATTN_WIDTH, D),
        _stacked(j, B, WINDOW, KV_WIDTH),
        _stacked(j, B, WINDOW, KV_WIDTH),
    ]
    args = [sinks, x, mods, norm_g, win, wout, cache_k, cache_v]
    if final_norm:
        in_specs.append(whole(1, D))
        args.append(final_g)
    window_shape = jax.ShapeDtypeStruct((n_layers, B, WINDOW, KV_WIDTH), _F32)
    carry_specs, carry_args, aliases = _carry(windows, first_output=1)
    return pl.pallas_call(
        functools.partial(_attn_sample_kernel, n_batch=B, tm=tm, layer=j, final_norm=final_norm,
                          n_carried=len(carry_args)),
        grid=(1, 1),
        in_specs=in_specs + carry_specs,
        out_specs=[whole(B, tm, D), _stacked(j, B, WINDOW, KV_WIDTH), _stacked(j, B, WINDOW, KV_WIDTH)],
        out_shape=[jax.ShapeDtypeStruct((B, tm, D), _F32), window_shape, window_shape],
        input_output_aliases=aliases(len(args)),
        scratch_shapes=[
            pltpu.VMEM((B * tm, ATTN_WIDTH), _BF16),
            pltpu.VMEM((N_KV_HEADS, B * BAND_KEYS, LANES), _BF16),
            pltpu.VMEM((N_KV_HEADS, B * BAND_KEYS, LANES), _BF16),
            pltpu.VMEM((N_KV_HEADS, B * BAND_KEYS, LANES), _BF16),
            pltpu.VMEM((N_KV_HEADS, B * BAND_KEYS, LANES), _BF16),
            pltpu.VMEM((B * tm, ATTN_WIDTH), _F32),
            pltpu.VMEM((B * tm, ATTN_WIDTH), _BF16),
            pltpu.VMEM((N_HEADS, tm, BAND_KEYS), _F32),
            pltpu.VMEM((B * tm, D), _BF16),
        ],
        compiler_params=pltpu.CompilerParams(
            dimension_semantics=("arbitrary", "arbitrary"), vmem_limit_bytes=VMEM_LIMIT_BYTES),
        name="attn_layer_sample",
    )(*args, *carry_args)


SUM_ROWS = 2 * SUBLANES
VT_ROWS = LANES + SUM_ROWS


def _dot_tn(a, b):
    return lax.dot_general(a, b, (((0,), (0,)), ((), ())), preferred_element_type=_F32)


def _attn_prompt_kernel(*refs, tm, nt, layer, final_norm, n_carried):
    refs = list(refs)
    (sink_ref, x_ref, mod_ref, g_ref, wqt_ref, wk_ref, wvt_ref, wgt_ref, wv_ref, wout_ref) = refs[:10]
    del refs[:10]
    if final_norm:
        fg_ref = refs.pop(0)
    del refs[:n_carried]
    xo_ref, kw_ref, vw_ref, qt, k_lo, k_hi, vt_lo, vt_hi, sgt, zt, bias_t, hbuf = refs
    t = pl.program_id(1)
    n_pairs = GQA_GROUP // HEAD_PAIR
    slopes = _alibi_slopes()

    @pl.when(t == 0)
    def _reset_window():
        for buf in (k_lo, k_hi):
            buf[:, 0:WINDOW, :] = jnp.zeros((N_KV_HEADS, WINDOW, LANES), _BF16)
        row = lax.broadcasted_iota(jnp.int32, (VT_ROWS, WINDOW + tm), 0)
        even_ones = (row >= LANES) & (row < LANES + SUBLANES)
        odd_ones = row >= LANES + SUBLANES
        for kh in range(N_KV_HEADS):
            vt_lo[kh] = jnp.where(even_ones, 1.0, 0.0).astype(_BF16)
            vt_hi[kh] = jnp.where(odd_ones, 1.0, 0.0).astype(_BF16)
        si = lax.broadcasted_iota(jnp.int32, (BAND_KEYS, QUERY_GROUP), 0)
        qj = lax.broadcasted_iota(jnp.int32, (BAND_KEYS, QUERY_GROUP), 1)
        dist = jnp.abs(WINDOW + qj - si).astype(_F32)
        for h in range(N_HEADS):
            bias_t[h] = dist * (-slopes[h] * LOG2_E)

    shift, scale, gmod = mod_ref[0, 0:1, :], mod_ref[0, 1:2, :], mod_ref[0, 2:3, :]
    n_parts = tm // PROJ_ROWS


    run_region = _emit_round_robin

    def norm_part(j):
        for r0 in range(j * PROJ_ROWS, (j + 1) * PROJ_ROWS, CHUNK):
            rows = slice(r0, r0 + CHUNK)
            hbuf[rows, :] = _modulated_norm(x_ref[0, rows, :], g_ref[...], shift, scale).astype(_BF16)
            yield

    def project_part(j):
        rows = slice(j * PROJ_ROWS, (j + 1) * PROJ_ROWS)
        new = slice(WINDOW + j * PROJ_ROWS, WINDOW + (j + 1) * PROJ_ROWS)
        hb = hbuf[rows, :]
        for f0 in range(0, ATTN_WIDTH, PROJ_FEATURES):
            feats = slice(f0, f0 + PROJ_FEATURES)
            sgt[feats, rows] = _dot_nt(wgt_ref[feats, :], hb)
            yield
        for f0 in range(0, ATTN_WIDTH, PROJ_FEATURES):
            feats = slice(f0, f0 + PROJ_FEATURES)
            qt[feats, rows] = (_dot_nt(wqt_ref[feats, :], hb) * (HEAD_DIM ** -0.5 * LOG2_E)).astype(_BF16)
            yield
        vt = _dot_nt(wvt_ref[...], hb)
        for kh in range(N_KV_HEADS):
            v_head = vt[kh * HEAD_DIM:(kh + 1) * HEAD_DIM, :].astype(_BF16)
            vt_lo[kh, 0:HEAD_DIM, new] = v_head
            vt_hi[kh, HEAD_DIM:2 * HEAD_DIM, new] = v_head
        yield
        _store_split_heads(k_lo, k_hi, new, _dot(hb, wk_ref[...]))
        yield

    def silu_part(j, f_lo=0, f_hi=ATTN_WIDTH):
        rows = slice(j * PROJ_ROWS, (j + 1) * PROJ_ROWS)
        for f0 in range(f_lo, f_hi, LANES):
            feats = slice(f0, f0 + LANES)
            sgt[feats, rows] = _silu(sgt[feats, rows])
            yield

    def output_part(grp):
        rows = slice(grp * QUERY_GROUP, (grp + 1) * QUERY_GROUP)
        y = _dot_tn(zt[:, rows], wout_ref[...])
        xn = x_ref[0, rows, :] + gmod * y
        if final_norm:
            xn = _rms(xn) * fg_ref[...]
        xo_ref[0, rows, :] = xn
        yield

    lane = lax.broadcasted_iota(jnp.int32, (CHUNK, LANES), 1)

    def key_chunk_mask(grp, kc):
        conds = []
        for i in range(QUERY_GROUP // CHUNK):
            qc = WINDOW_CHUNKS + i
            if not (qc - WINDOW_CHUNKS <= kc <= qc):
                conds.append(lane >= CHUNK if i == 0 else lane < CHUNK)
        if grp == 0 and kc < WINDOW_CHUNKS:
            conds.append(t > 0)
        return functools.reduce(jnp.logical_and, conds) if conds else None


    def score_dots(grp, kh, st):
        r0, c0 = grp * QUERY_GROUP, kh * GQA_GROUP * HEAD_DIM
        q_lanes, band = slice(r0, r0 + QUERY_GROUP), slice(r0, r0 + BAND_KEYS)
        q_rhs = jnp.concatenate([qt[c0 + p * LANES:c0 + (p + 1) * LANES, q_lanes] for p in range(n_pairs)], axis=1)
        st["s_t"] = []
        for kbuf in (k_lo, k_hi):
            st["s_t"].append(_dot(kbuf[kh, band, :], q_rhs))
            yield

    def score_max(grp, kh, st):
        masks = [key_chunk_mask(grp, kc) for kc in range(BAND_KEYS // CHUNK)]
        st["blocks"], st["m"], st["sink_p"] = [], [], []
        for parity in range(HEAD_PAIR):
            for p in range(n_pairs):
                h = kh * GQA_GROUP + p * HEAD_PAIR + parity
                sink2 = sink_ref[layer, h] * LOG2_E
                blocks = []
                for kc, allowed in enumerate(masks):
                    rows = slice(kc * CHUNK, (kc + 1) * CHUNK)
                    s = st["s_t"][parity][rows, p * QUERY_GROUP:(p + 1) * QUERY_GROUP] + bias_t[h, rows, :]
                    blocks.append(s if allowed is None else jnp.where(allowed, s, MASK_VALUE))
                m = jnp.max(functools.reduce(jnp.maximum, blocks), axis=0, keepdims=True)
                m = jnp.maximum(m, sink2)
                st["blocks"].append(blocks)
                st["m"].append(m)
                st["sink_p"].append(jnp.exp2(sink2 - m))
                yield

    def probabilities(grp, kh, st):
        st["p_t"] = []
        for blocks, m in zip(st["blocks"], st["m"]):
            st["p_t"].append(jnp.concatenate([jnp.exp2(b - m).astype(_BF16) for b in blocks], axis=0))
            yield

    def value_dots(grp, kh, st):
        band = slice(grp * QUERY_GROUP, grp * QUERY_GROUP + BAND_KEYS)
        st["acc"] = []
        for parity, vbuf in enumerate((vt_lo, vt_hi)):
            p_t = jnp.concatenate(st["p_t"][parity * n_pairs:(parity + 1) * n_pairs], axis=1)
            st["acc"].append(_dot(vbuf[kh, :, band], p_t))
            yield

    def finish(grp, kh, st):
        r0, c0 = grp * QUERY_GROUP, kh * GQA_GROUP * HEAD_DIM
        q_lanes = slice(r0, r0 + QUERY_GROUP)
        acc = st["acc"][0] + st["acc"][1]
        for p in range(n_pairs):
            lanes = slice(p * QUERY_GROUP, (p + 1) * QUERY_GROUP)
            inv_even = 1.0 / (acc[LANES:LANES + 1, lanes] + st["sink_p"][p])
            inv_odd = 1.0 / (acc[LANES + SUBLANES:LANES + SUBLANES + 1, lanes] + st["sink_p"][n_pairs + p])
            out = jnp.concatenate([acc[0:HEAD_DIM, lanes] * inv_even,
                                   acc[HEAD_DIM:2 * HEAD_DIM, lanes] * inv_odd], axis=0)
            rows = slice(c0 + p * LANES, c0 + (p + 1) * LANES)
            zt[rows, q_lanes] = (out * sgt[rows, q_lanes]).astype(_BF16)
            yield
        st.clear()

    stages = (score_dots, score_max, probabilities, value_dots, finish)
    steps = [(grp, kh) for grp in range(tm // QUERY_GROUP) for kh in range(N_KV_HEADS)]
    state = [{} for _ in steps]
    n_regions = len(steps) + len(stages) - 1
    fill = len(stages) - 1
    per_part = len(steps) // n_parts

    def advance(gen, n):
        for _ in range(n):
            if next(gen, gen) is gen:
                return
            yield

    def spread(gen, first, count, per_region):
        for r in range(first, first + count):
            extras[r].append(advance(gen, per_region))

    extras = [[] for _ in range(n_regions + 1)]
    for j in range(n_parts):
        r0 = j * per_part
        if j > 0:
            spread(silu_part(j), r0, fill, ATTN_WIDTH // LANES // fill)
        if j + 1 < n_parts:
            n_dots = 2 * (ATTN_WIDTH // PROJ_FEATURES) + 2
            spread(project_part(j + 1), r0, per_part - 2, pl.cdiv(n_dots, per_part - 2))
        if j + 2 < n_parts:
            spread(norm_part(j + 2), r0 + per_part - 2, 2, PROJ_ROWS // CHUNK // 2)
    for grp in range(tm // QUERY_GROUP):
        extras[min((grp + 1) * N_KV_HEADS + fill, n_regions)].append(output_part(grp))

    run_region([norm_part(0)])
    first = project_part(0)
    run_region([advance(first, ATTN_WIDTH // PROJ_FEATURES)] + ([norm_part(1)] if n_parts > 1 else []))
    run_region([first, silu_part(0)])
    for r in range(n_regions):
        run_region([stage(*steps[r - j], state[r - j]) for j, stage in enumerate(stages) if 0 <= r - j < len(steps)]
                   + extras[r])
    run_region(extras[n_regions])

    @pl.when(t == nt - 1)
    def _emit_window():
        last = hbuf[tm - WINDOW:tm, :]
        kw_ref[0] = _dot(last, wk_ref[...])
        vw_ref[0] = _dot(last, wv_ref[...])

    for buf in (k_lo, k_hi):
        buf[:, 0:WINDOW, :] = buf[:, tm:tm + WINDOW, :]
    for buf in (vt_lo, vt_hi):
        buf[:, :, 0:WINDOW] = buf[:, :, tm:tm + WINDOW]


def _split_attn_w_in(win):
    q0, k0, v0, g0 = 0, ATTN_WIDTH, ATTN_WIDTH + KV_WIDTH, ATTN_WIDTH + 2 * KV_WIDTH
    t = lambda w: jnp.swapaxes(w, 1, 2)
    wk, wv = win[:, :, k0:v0], win[:, :, v0:g0]
    return t(win[:, :, q0:k0]), wk, t(wv), t(win[:, :, g0:g0 + ATTN_WIDTH]), wv


def _attn_prompt_layer(x, mods, norm_g, win_parts, sinks, wout, final_g, windows, *, i, row0, tm):
    B, T, D = x.shape
    nt = T // tm
    j = i // 2
    final_norm = final_g is not None
    in_specs = [
        pl.BlockSpec(memory_space=pltpu.SMEM),
        pl.BlockSpec((1, tm, D), lambda b, t: (b, t, 0)),
        _mod_spec(i, row0),
        _stacked(i, 1, D),
        _stacked(j, ATTN_WIDTH, D), _stacked(j, D, KV_WIDTH), _stacked(j, KV_WIDTH, D), _stacked(j, ATTN_WIDTH, D),
        _stacked(j, D, KV_WIDTH),
        _stacked(j, ATTN_WIDTH, D),
    ]
    args = [sinks, x, mods, norm_g, *win_parts, wout]
    if final_norm:
        in_specs.append(pl.BlockSpec((1, D), lambda b, t: (0, 0)))
        args.append(final_g)
    window_spec = pl.BlockSpec((None, 1, WINDOW, KV_WIDTH), lambda b, t: (j, b, 0, 0))
    window_shape = jax.ShapeDtypeStruct((win_parts[0].shape[0], B, WINDOW, KV_WIDTH), _F32)
    carry_specs, carry_args, aliases = _carry(windows, first_output=1)
    return pl.pallas_call(
        functools.partial(_attn_prompt_kernel, tm=tm, nt=nt, layer=j, final_norm=final_norm,
                          n_carried=len(carry_args)),
        grid=(B, nt),
        in_specs=in_specs + carry_specs,
        out_specs=[pl.BlockSpec((1, tm, D), lambda b, t: (b, t, 0)), window_spec, window_spec],
        out_shape=[jax.ShapeDtypeStruct((B, T, D), _F32), window_shape, window_shape],
        input_output_aliases=aliases(len(args)),
        scratch_shapes=[
            pltpu.VMEM((ATTN_WIDTH, tm), _BF16),
            pltpu.VMEM((N_KV_HEADS, WINDOW + tm, LANES), _BF16),
            pltpu.VMEM((N_KV_HEADS, WINDOW + tm, LANES), _BF16),
            pltpu.VMEM((N_KV_HEADS, VT_ROWS, WINDOW + tm), _BF16),
            pltpu.VMEM((N_KV_HEADS, VT_ROWS, WINDOW + tm), _BF16),
            pltpu.VMEM((ATTN_WIDTH, tm), _F32),
            pltpu.VMEM((ATTN_WIDTH, tm), _BF16),
            pltpu.VMEM((N_HEADS, BAND_KEYS, QUERY_GROUP), _F32),
            pltpu.VMEM((tm, D), _BF16),
        ],
        compiler_params=pltpu.CompilerParams(
            dimension_semantics=("arbitrary", "arbitrary"), vmem_limit_bytes=VMEM_LIMIT_BYTES),
        name="attn_layer_prompt",
    )(*args, *carry_args)


def kernel(x_prompt, x_sample, c_prompt, c_sample, cache_pool, cache_k, cache_v, norm_g, ada_w, ada_b,
           pool_w_in, pool_w_grp, pool_scale, pool_w_out, attn_w_in, attn_sinks, attn_w_out, final_g):
    n_prompt, n_sample = x_prompt.shape[0], x_sample.shape[0]
    mods = _modulation(jnp.concatenate([c_prompt, c_sample], axis=0), ada_w, ada_b)
    mods = mods.reshape(DEPTH, n_prompt + n_sample, 3, D_MODEL)

    norm_g = norm_g.reshape(DEPTH, 1, D_MODEL)
    final_g = final_g.reshape(1, D_MODEL)
    pool_w_in, pool_w_grp, pool_w_out = (w.astype(_BF16) for w in (pool_w_in, pool_w_grp, pool_w_out))
    pool_scale = pool_scale.reshape(-1, 1, POOL_WIDTH)
    attn_w_in, attn_w_out = attn_w_in.astype(_BF16), attn_w_out.astype(_BF16)
    attn_w_in_parts = _split_attn_w_in(attn_w_in)
    hist = jnp.pad(cache_pool, ((0, 0), (0, 0), (HIST_ROWS - POOL_HIST, 0), (0, 0)))
    cache_k = cache_k.reshape(cache_k.shape[:3] + (KV_WIDTH,))
    cache_v = cache_v.reshape(cache_v.shape[:3] + (KV_WIDTH,))

    def trunk(x, row0, sample):
        rows = x.shape[1]
        states, windows = None, None
        for i in range(DEPTH):
            if i % 2 == 0:
                x, states = _pool_layer(x, mods, norm_g, pool_w_in, pool_w_grp, pool_scale, pool_w_out,
                                        hist if sample else None, states, i=i, row0=row0,
                                        tm=rows if sample else min(POOL_PROMPT_TILE, rows))
            else:
                fg = final_g if i == DEPTH - 1 else None
                if sample:
                    x, *windows = _attn_sample_layer(x, mods, norm_g, attn_w_in, attn_sinks, attn_w_out,
                                                     cache_k, cache_v, fg, windows, i=i, row0=row0)
                else:
                    x, *windows = _attn_prompt_layer(x, mods, norm_g, attn_w_in_parts, attn_sinks, attn_w_out, fg,
                                                     windows, i=i, row0=row0, tm=min(ATTN_PROMPT_TILE, rows))
        k_win, v_win = (w.reshape(w.shape[:3] + (N_KV_HEADS, HEAD_DIM)) for w in windows)
        return x, states[:, :, HIST_ROWS - POOL_HIST:], k_win, v_win

    y_prompt, pool_p, k_p, v_p = trunk(x_prompt, 0, False)
    y_sample, pool_s, k_s, v_s = trunk(x_sample, n_prompt, True)
    return (y_prompt, y_sample, pool_p, k_p, v_p, pool_s, k_s, v_s)
```

```python
import functools

import numpy as np
import jax
import jax.numpy as jnp
from jax import lax
from jax.experimental import pallas as pl
from jax.experimental.pallas import tpu as pltpu

D_MODEL = 1024
DEPTH = 4
PAST_LEN = 1024
CHUNK = 64
N_HEADS = 16
HEAD_DIM = 64
N_KV_HEADS = 4
GQA_GROUP = N_HEADS // N_KV_HEADS
ATTN_WIDTH = N_HEADS * HEAD_DIM
KV_WIDTH = N_KV_HEADS * HEAD_DIM
WINDOW = 128
WINDOW_CHUNKS = WINDOW // CHUNK
POOL_WIDTH = D_MODEL
POOL_WINDOWS = (2, 4, 8, 16)
POOL_GROUP_WIDTH = POOL_WIDTH // len(POOL_WINDOWS)
POOL_HIST = max(POOL_WINDOWS) - 1
NORM_EPS = 1e-6
MASK_VALUE = -1e30
LOG2_E = 1.4426950408889634

LANES = 128
SUBLANES = 8
VMEM_LIMIT_BYTES = 56 * 1024 * 1024

HIST_ROWS = 2 * SUBLANES
HEAD_PAIR = LANES // HEAD_DIM
BAND_KEYS = 2 * WINDOW
POOL_PROMPT_TILE = 1024
ATTN_PROMPT_TILE = 1024
QUERY_GROUP = 2 * CHUNK
PROJ_ROWS = 256
PROJ_FEATURES = 512

_F32 = jnp.float32
_BF16 = jnp.bfloat16


def _alibi_slopes():
    h = np.arange(1, N_HEADS + 1, dtype=np.float32)
    return [float(s) for s in np.exp2(np.float32(-8.0) * h / np.float32(N_HEADS))]


def _rms(x):
    return x * lax.rsqrt(jnp.mean(x * x, axis=-1, keepdims=True) + NORM_EPS)


def _modulated_norm(x, g, shift, scale):
    return (_rms(x) * g) * (1.0 + scale) + shift


def _silu(x):
    h = 0.5 * x
    return h + h * jnp.tanh(h)


def _emit_round_robin(gens):
    live = list(gens)
    while live:
        live = [g for g in live if next(g, live) is not live]


def _dot(a, b):
    return jnp.dot(a, b, preferred_element_type=_F32)


def _dot_nt(a, b):
    return lax.dot_general(a, b, (((1,), (1,)), ((), ())), preferred_element_type=_F32)


def _mod_kernel(c_ref, w_ref, b_ref, o_ref):
    c = c_ref[...]
    o_ref[0] = _dot(_silu(c).astype(_BF16), w_ref[0].astype(_BF16)) + b_ref[0]


def _modulation(c_all, ada_w, ada_b):
    rows = c_all.shape[0]
    n_col_tiles = 3
    return pl.pallas_call(
        _mod_kernel,
        grid=(DEPTH, n_col_tiles),
        in_specs=[
            pl.BlockSpec((rows, D_MODEL), lambda i, n: (0, 0)),
            pl.BlockSpec((1, D_MODEL, D_MODEL), lambda i, n: (i, 0, n)),
            pl.BlockSpec((1, 1, D_MODEL), lambda i, n: (i, 0, n)),
        ],
        out_specs=pl.BlockSpec((1, rows, D_MODEL), lambda i, n: (i, 0, n)),
        out_shape=jax.ShapeDtypeStruct((DEPTH, rows, 3 * D_MODEL), _F32),
        compiler_params=pltpu.CompilerParams(
            dimension_semantics=("arbitrary", "arbitrary"), vmem_limit_bytes=VMEM_LIMIT_BYTES),
        name="adaln_modulation",
    )(c_all, ada_w, ada_b.reshape(DEPTH, 1, 3 * D_MODEL))


def _pool_kernel(*refs, n_seg, seg, nt, sample, n_carried):
    refs = list(refs)
    x_ref, mod_ref, g_ref, win_ref, wgrp_ref, psc_ref, wout_ref = refs[:7]
    del refs[:7]
    if sample:
        hist_ref = refs.pop(0)
    del refs[:n_carried]
    xo_ref, st_ref, ubuf, sgbuf, pbuf, zbuf, hbuf = refs
    t = pl.program_id(1)
    tm = n_seg * seg

    @pl.when(t == 0)
    def _load_history():
        if sample:
            ubuf[:, 0:HIST_ROWS, :] = hist_ref[...]
        else:
            ubuf[:, 0:HIST_ROWS, :] = jnp.zeros((n_seg, HIST_ROWS, POOL_WIDTH), _F32)

    part = min(PROJ_ROWS, tm)
    piece = min(part, seg)
    rc = min(CHUNK, piece)
    group_cols = [slice(g * POOL_GROUP_WIDTH, (g + 1) * POOL_GROUP_WIDTH) for g in range(len(POOL_WINDOWS))]

    def pieces(p, n):
        return [(r // seg, r % seg, r - p * part) for r in range(p * part, (p + 1) * part, n)]


    def norm_part(p):
        for b, o, i in pieces(p, rc):
            h = _modulated_norm(x_ref[b, o:o + rc, :], g_ref[...], mod_ref[b, 0:1, :], mod_ref[b, 1:2, :])
            hbuf[p * part + i:p * part + i + rc, :] = h.astype(_BF16)
            yield

    def project_part(p):
        rows = slice(p * part, (p + 1) * part)
        hb = hbuf[rows, :]
        u = _dot(hb, win_ref[:, 0:POOL_WIDTH])
        for b, o, i in pieces(p, piece):
            ubuf[b, HIST_ROWS + o:HIST_ROWS + o + piece, :] = u[i:i + piece]
        yield
        sgbuf[rows, :] = _dot(hb, win_ref[:, POOL_WIDTH:2 * POOL_WIDTH])
        yield

    def pool_part(p):
        for b, o, i in pieces(p, rc):
            rows = slice(p * part + i, p * part + i + rc)
            sgbuf[rows, :] = _silu(sgbuf[rows, :])
            for cols, w in zip(group_cols, POOL_WINDOWS):
                ext = ubuf[b, o:o + HIST_ROWS + rc, cols]
                s = ext
                k = 1
                while k < w:
                    s = s + pltpu.roll(s, k, axis=0)
                    k *= 2
                s, u = s[HIST_ROWS:], ext[HIST_ROWS:]
                if sample or o > 0:
                    pooled = s * (1.0 / w) - u
                else:
                    pos = t * seg + lax.broadcasted_iota(jnp.int32, (rc, POOL_GROUP_WIDTH), 0)
                    pooled = s / jnp.minimum(pos + 1, w).astype(_F32) - u
                pbuf[rows, cols] = pooled.astype(_BF16)
            yield

    def mix_part(p):
        rows = slice(p * part, (p + 1) * part)
        for g, cols in enumerate(group_cols):
            z = _dot(pbuf[rows, cols], wgrp_ref[g]) * psc_ref[:, cols]
            zbuf[rows, cols] = (z * sgbuf[rows, cols]).astype(_BF16)
            yield

    def output_part(p):
        y = _dot(zbuf[p * part:(p + 1) * part, :], wout_ref[...])
        for b, o, i in pieces(p, piece):
            xo_ref[b, o:o + piece, :] = x_ref[b, o:o + piece, :] + mod_ref[b, 2:3, :] * y[i:i + piece]
        yield

    stages = (norm_part, project_part, pool_part, mix_part, output_part)
    n_parts = tm // part
    for r in range(n_parts + len(stages) - 1):
        _emit_round_robin([stage(r - j) for j, stage in enumerate(stages) if 0 <= r - j < n_parts])

    @pl.when(t == nt - 1)
    def _emit_state():
        st_ref[...] = ubuf[:, seg:seg + HIST_ROWS, :]

    ubuf[:, 0:HIST_ROWS, :] = ubuf[:, seg:seg + HIST_ROWS, :]


def _stacked(index, *shape):
    return pl.BlockSpec((None,) + shape, lambda b, t: (index,) + (0,) * len(shape))


def _mod_spec(i, row0):
    return pl.BlockSpec((None, 1, 3, D_MODEL), lambda b, t: (i, row0 + b, 0, 0))


def _carry(carried, first_output):
    carried = list(carried or ())
    specs = [pl.BlockSpec(memory_space=pl.ANY)] * len(carried)
    return specs, carried, lambda n_in: {n_in + k: first_output + k for k in range(len(carried))}


def _pool_layer(x, mods, norm_g, win, wgrp, psc, wout, hist, states, *, i, row0, tm):
    B, T, D = x.shape
    j = i // 2
    n_layers = win.shape[0]
    sample = hist is not None
    n_seg, seg = (B, T) if sample else (1, tm)
    grid = (B // n_seg, T // seg)
    assert row0 % n_seg == 0
    in_specs = [
        pl.BlockSpec((n_seg, seg, D), lambda b, t: (b, t, 0)),
        pl.BlockSpec((None, n_seg, 3, D), lambda b, t: (i, row0 // n_seg + b, 0, 0)),
        _stacked(i, 1, D),
        _stacked(j, D, 2 * POOL_WIDTH),
        _stacked(j, len(POOL_WINDOWS), POOL_GROUP_WIDTH, POOL_GROUP_WIDTH),
        _stacked(j, 1, POOL_WIDTH),
        _stacked(j, POOL_WIDTH, D),
    ]
    args = [x, mods, norm_g, win, wgrp, psc, wout]
    if sample:
        in_specs.append(pl.BlockSpec((None, n_seg, HIST_ROWS, POOL_WIDTH), lambda b, t: (j, b, 0, 0)))
        args.append(hist)
    tm = n_seg * seg
    carry_specs, carry_args, aliases = _carry(None if states is None else [states], first_output=1)
    return pl.pallas_call(
        functools.partial(_pool_kernel, n_seg=n_seg, seg=seg, nt=grid[1], sample=sample, n_carried=len(carry_args)),
        grid=grid,
        in_specs=in_specs + carry_specs,
        out_specs=[
            pl.BlockSpec((n_seg, seg, D), lambda b, t: (b, t, 0)),
            pl.BlockSpec((None, n_seg, HIST_ROWS, POOL_WIDTH), lambda b, t: (j, b, 0, 0)),
        ],
        out_shape=[
            jax.ShapeDtypeStruct((B, T, D), _F32),
            jax.ShapeDtypeStruct((n_layers, B, HIST_ROWS, POOL_WIDTH), _F32),
        ],
        input_output_aliases=aliases(len(args)),
        scratch_shapes=[
            pltpu.VMEM((n_seg, HIST_ROWS + seg, POOL_WIDTH), _F32),
            pltpu.VMEM((tm, POOL_WIDTH), _F32),
            pltpu.VMEM((tm, POOL_WIDTH), _BF16),
            pltpu.VMEM((tm, POOL_WIDTH), _BF16),
            pltpu.VMEM((tm, D), _BF16),
        ],
        compiler_params=pltpu.CompilerParams(
            dimension_semantics=("arbitrary", "arbitrary"), vmem_limit_bytes=VMEM_LIMIT_BYTES),
        name="pool_layer_sample" if sample else "pool_layer_prompt",
    )(*args, *carry_args)


def _store_split_heads(dst_lo, dst_hi, rows, val):
    n = val.shape[0]
    lo = lax.broadcasted_iota(jnp.int32, (n, LANES), 1) < HEAD_DIM
    zero = jnp.zeros((n, LANES), _F32)
    for j in range(KV_WIDTH // LANES):
        a = val[:, j * LANES:(j + 1) * LANES]
        ar = pltpu.roll(a, HEAD_DIM, axis=1)
        dst_lo[HEAD_PAIR * j, rows, :] = jnp.where(lo, a, zero).astype(_BF16)
        dst_hi[HEAD_PAIR * j, rows, :] = jnp.where(lo, zero, ar).astype(_BF16)
        dst_lo[HEAD_PAIR * j + 1, rows, :] = jnp.where(lo, ar, zero).astype(_BF16)
        dst_hi[HEAD_PAIR * j + 1, rows, :] = jnp.where(lo, zero, a).astype(_BF16)


def _attn_sample_kernel(*refs, n_batch, tm, layer, final_norm, n_carried):
    refs = list(refs)
    sink_ref, x_ref, mod_ref, g_ref, win_ref, wout_ref, ck_ref, cv_ref = refs[:8]
    del refs[:8]
    if final_norm:
        fg_ref = refs.pop(0)
    del refs[:n_carried]
    xo_ref, kw_ref, vw_ref, qbuf, k_lo, k_hi, v_lo, v_hi, sgbuf, zbuf, bias_ref, hbuf = refs
    batch_rows = [slice(b * tm, (b + 1) * tm) for b in range(n_batch)]

    for b, rows in enumerate(batch_rows):
        h = _modulated_norm(x_ref[b], g_ref[...], mod_ref[b, 0:1, :], mod_ref[b, 1:2, :])
        hbuf[rows, :] = h.astype(_BF16)
    hb = hbuf[...]
    q0, k0, v0, g0 = 0, ATTN_WIDTH, ATTN_WIDTH + KV_WIDTH, ATTN_WIDTH + 2 * KV_WIDTH
    qbuf[...] = (_dot(hb, win_ref[:, q0:k0]) * (HEAD_DIM ** -0.5 * LOG2_E)).astype(_BF16)
    k = _dot(hb, win_ref[:, k0:v0])
    v = _dot(hb, win_ref[:, v0:g0])
    sgbuf[...] = _silu(_dot(hb, win_ref[:, g0:g0 + ATTN_WIDTH]))
    for b, rows in enumerate(batch_rows):
        band0 = b * BAND_KEYS
        _store_split_heads(k_lo, k_hi, slice(band0, band0 + WINDOW), ck_ref[b])
        _store_split_heads(v_lo, v_hi, slice(band0, band0 + WINDOW), cv_ref[b])
        _store_split_heads(k_lo, k_hi, slice(band0 + WINDOW, band0 + WINDOW + tm), k[rows])
        _store_split_heads(v_lo, v_hi, slice(band0 + WINDOW, band0 + WINDOW + tm), v[rows])
        for buf in (k_lo, k_hi, v_lo, v_hi):
            buf[:, band0 + WINDOW + tm:band0 + BAND_KEYS, :] = jnp.zeros(
                (N_KV_HEADS, BAND_KEYS - WINDOW - tm, LANES), _BF16)
        kw_ref[b, 0:WINDOW - tm, :] = ck_ref[b, tm:WINDOW, :]
        vw_ref[b, 0:WINDOW - tm, :] = cv_ref[b, tm:WINDOW, :]
        kw_ref[b, WINDOW - tm:WINDOW, :] = k[rows]
        vw_ref[b, WINDOW - tm:WINDOW, :] = v[rows]

    qi = lax.broadcasted_iota(jnp.int32, (tm, BAND_KEYS), 0)
    sj = lax.broadcasted_iota(jnp.int32, (tm, BAND_KEYS), 1)
    dist = jnp.abs(WINDOW + qi - sj).astype(_F32)
    for h, slope in enumerate(_alibi_slopes()):
        bias_ref[h] = dist * (-slope * LOG2_E)

    lane = lax.broadcasted_iota(jnp.int32, (tm, LANES), 1)
    lo_lanes = lax.broadcasted_iota(jnp.int32, (HEAD_PAIR * tm, LANES), 1) < HEAD_DIM
    key_lo_lanes = lax.broadcasted_iota(jnp.int32, (BAND_KEYS, LANES), 1) < HEAD_DIM
    ones_lo = jnp.where(key_lo_lanes, 1.0, 0.0).astype(_BF16)
    ones_hi = jnp.where(key_lo_lanes, 0.0, 1.0).astype(_BF16)

    qc = WINDOW // CHUNK
    masks = []
    for col in range(BAND_KEYS // LANES):
        conds = []
        for half in range(LANES // CHUNK):
            kc = (col * LANES) // CHUNK + half
            if not (qc - WINDOW_CHUNKS <= kc <= qc):
                conds.append(lane >= CHUNK if half == 0 else lane < CHUNK)
        if col * LANES + LANES > WINDOW + tm:
            conds.append(lane < WINDOW + tm - col * LANES)
        masks.append(functools.reduce(jnp.logical_and, conds) if conds else None)

    n_pairs = GQA_GROUP // HEAD_PAIR

    def score_dots(b, kh, st):
        c0 = kh * GQA_GROUP * HEAD_DIM
        qs = jnp.concatenate(
            [qbuf[batch_rows[b], c0 + p * LANES:c0 + (p + 1) * LANES] for p in range(n_pairs)], axis=0)
        st["s"] = [_dot_nt(qs, kbuf[kh, b * BAND_KEYS:(b + 1) * BAND_KEYS, :]) for kbuf in (k_lo, k_hi)]
        yield

    def probabilities(b, kh, st):
        st["probs"], st["sink_p"] = [], []
        for parity in range(HEAD_PAIR):
            p_blocks, sink_blocks = [], []
            for p in range(n_pairs):
                h = kh * GQA_GROUP + p * HEAD_PAIR + parity
                sink2 = sink_ref[layer, h] * LOG2_E
                cols = []
                for col, allowed in enumerate(masks):
                    lanes = slice(col * LANES, (col + 1) * LANES)
                    s = st["s"][parity][p * tm:(p + 1) * tm, lanes] + bias_ref[h, :, lanes]
                    cols.append(s if allowed is None else jnp.where(allowed, s, MASK_VALUE))
                m = jnp.maximum(jnp.max(functools.reduce(jnp.maximum, cols), axis=-1, keepdims=True), sink2)
                sink_blocks.append(jnp.exp2(sink2 - m))
                p_blocks.append(jnp.concatenate([jnp.exp2(c - m).astype(_BF16) for c in cols], axis=1))
                yield
            st["probs"].append(jnp.concatenate(p_blocks, axis=0))
            st["sink_p"].append(jnp.concatenate(sink_blocks, axis=0))

    def value_dots(b, kh, st):
        band = slice(b * BAND_KEYS, (b + 1) * BAND_KEYS)
        st["acc"] = (_dot(st["probs"][0], jnp.concatenate([v_lo[kh, band, :], ones_lo], axis=1))
                     + _dot(st["probs"][1], jnp.concatenate([v_hi[kh, band, :], ones_hi], axis=1)))
        yield

    def finish(b, kh, st):
        rows, c0 = batch_rows[b], kh * GQA_GROUP * HEAD_DIM
        acc = st["acc"]
        o = acc[:, :LANES] / (acc[:, LANES:] + jnp.where(lo_lanes, st["sink_p"][0], st["sink_p"][1]))
        for p in range(n_pairs):
            cols = slice(c0 + p * LANES, c0 + (p + 1) * LANES)
            zbuf[rows, cols] = (o[p * tm:(p + 1) * tm] * sgbuf[rows, cols]).astype(_BF16)
        st.clear()
        yield

    stages = (score_dots, probabilities, value_dots, finish)
    steps = [(b, kh) for b in range(n_batch) for kh in range(N_KV_HEADS)]
    state = [{} for _ in steps]
    for r in range(len(steps) + len(stages) - 1):
        _emit_round_robin(
            [stage(*steps[r - j], state[r - j]) for j, stage in enumerate(stages) if 0 <= r - j < len(steps)])

    y = _dot(zbuf[...], wout_ref[...])
    for b, rows in enumerate(batch_rows):
        xn = x_ref[b] + mod_ref[b, 2:3, :] * y[rows]
        if final_norm:
            xn = _rms(xn) * fg_ref[...]
        xo_ref[b] = xn


def _attn_sample_layer(x, mods, norm_g, win, sinks, wout, cache_k, cache_v, final_g, windows, *, i, row0):
    B, tm, D = x.shape
    assert tm < CHUNK and PAST_LEN % CHUNK == 0 and row0 % B == 0
    j = i // 2
    n_layers = win.shape[0]
    final_norm = final_g is not None
    whole = lambda *shape: pl.BlockSpec(shape, lambda b, t: (0,) * len(shape))
    in_specs = [
        pl.BlockSpec(memory_space=pltpu.SMEM),
        whole(B, tm, D),
        pl.BlockSpec((None, B, 3, D), lambda b, t: (i, row0 // B, 0, 0)),
        _stacked(i, 1, D),
        _stacked(j, D, 2 * ATTN_WIDTH + 2 * KV_WIDTH),
        _stacked(j, ATTN_WIDTH, D),
        _stacked(j, B, WINDOW, KV_WIDTH),
        _stacked(j, B, WINDOW, KV_WIDTH),
    ]
    args = [sinks, x, mods, norm_g, win, wout, cache_k, cache_v]
    if final_norm:
        in_specs.append(whole(1, D))
        args.append(final_g)
    window_shape = jax.ShapeDtypeStruct((n_layers, B, WINDOW, KV_WIDTH), _F32)
    carry_specs, carry_args, aliases = _carry(windows, first_output=1)
    return pl.pallas_call(
        functools.partial(_attn_sample_kernel, n_batch=B, tm=tm, layer=j, final_norm=final_norm,
                          n_carried=len(carry_args)),
        grid=(1, 1),
        in_specs=in_specs + carry_specs,
        out_specs=[whole(B, tm, D), _stacked(j, B, WINDOW, KV_WIDTH), _stacked(j, B, WINDOW, KV_WIDTH)],
        out_shape=[jax.ShapeDtypeStruct((B, tm, D), _F32), window_shape, window_shape],
        input_output_aliases=aliases(len(args)),
        scratch_shapes=[
            pltpu.VMEM((B * tm, ATTN_WIDTH), _BF16),
            pltpu.VMEM((N_KV_HEADS, B * BAND_KEYS, LANES), _BF16),
            pltpu.VMEM((N_KV_HEADS, B * BAND_KEYS, LANES), _BF16),
            pltpu.VMEM((N_KV_HEADS, B * BAND_KEYS, LANES), _BF16),
            pltpu.VMEM((N_KV_HEADS, B * BAND_KEYS, LANES), _BF16),
            pltpu.VMEM((B * tm, ATTN_WIDTH), _F32),
            pltpu.VMEM((B * tm, ATTN_WIDTH), _BF16),
            pltpu.VMEM((N_HEADS, tm, BAND_KEYS), _F32),
            pltpu.VMEM((B * tm, D), _BF16),
        ],
        compiler_params=pltpu.CompilerParams(
            dimension_semantics=("arbitrary", "arbitrary"), vmem_limit_bytes=VMEM_LIMIT_BYTES),
        name="attn_layer_sample",
    )(*args, *carry_args)


SUM_ROWS = 2 * SUBLANES
VT_ROWS = HEAD_DIM + SUM_ROWS


def _dot_tn(a, b):
    return lax.dot_general(a, b, (((0,), (0,)), ((), ())), preferred_element_type=_F32)


def _attn_prompt_kernel(*refs, tm, nt, layer, final_norm, n_carried):
    refs = list(refs)
    (sink_ref, x_ref, mod_ref, g_ref, wqt_ref, wk_ref, wvt_ref, wgt_ref, wv_ref, wout_ref) = refs[:10]
    del refs[:10]
    if final_norm:
        fg_ref = refs.pop(0)
    del refs[:n_carried]
    xo_ref, kw_ref, vw_ref, qt, k_lo, k_hi, vt_buf, sgt, zt, bias_t, hbuf = refs
    t = pl.program_id(1)
    n_pairs = GQA_GROUP // HEAD_PAIR
    slopes = _alibi_slopes()

    @pl.when(t == 0)
    def _reset_window():
        for buf in (k_lo, k_hi):
            buf[:, 0:WINDOW, :] = jnp.zeros((N_KV_HEADS, WINDOW, LANES), _BF16)
        row = lax.broadcasted_iota(jnp.int32, (VT_ROWS, WINDOW + tm), 0)
        ones_rows = (row >= HEAD_DIM) & (row < HEAD_DIM + SUBLANES)
        for kh in range(N_KV_HEADS):
            vt_buf[kh] = jnp.where(ones_rows, 1.0, 0.0).astype(_BF16)
        si = lax.broadcasted_iota(jnp.int32, (BAND_KEYS, QUERY_GROUP), 0)
        qj = lax.broadcasted_iota(jnp.int32, (BAND_KEYS, QUERY_GROUP), 1)
        dist = jnp.abs(WINDOW + qj - si).astype(_F32)
        for h in range(N_HEADS):
            bias_t[h] = dist * (-slopes[h] * LOG2_E)

    shift, scale, gmod = mod_ref[0, 0:1, :], mod_ref[0, 1:2, :], mod_ref[0, 2:3, :]
    n_parts = tm // PROJ_ROWS


    run_region = _emit_round_robin

    def norm_part(j):
        for r0 in range(j * PROJ_ROWS, (j + 1) * PROJ_ROWS, CHUNK):
            rows = slice(r0, r0 + CHUNK)
            hbuf[rows, :] = _modulated_norm(x_ref[0, rows, :], g_ref[...], shift, scale).astype(_BF16)
            yield

    def project_part(j):
        rows = slice(j * PROJ_ROWS, (j + 1) * PROJ_ROWS)
        new = slice(WINDOW + j * PROJ_ROWS, WINDOW + (j + 1) * PROJ_ROWS)
        hb = hbuf[rows, :]
        for f0 in range(0, ATTN_WIDTH, PROJ_FEATURES):
            feats = slice(f0, f0 + PROJ_FEATURES)
            sgt[feats, rows] = _dot_nt(wgt_ref[feats, :], hb)
            yield
        for f0 in range(0, ATTN_WIDTH, PROJ_FEATURES):
            feats = slice(f0, f0 + PROJ_FEATURES)
            qt[feats, rows] = (_dot_nt(wqt_ref[feats, :], hb) * (HEAD_DIM ** -0.5 * LOG2_E)).astype(_BF16)
            yield
        vt = _dot_nt(wvt_ref[...], hb)
        for kh in range(N_KV_HEADS):
            vt_buf[kh, 0:HEAD_DIM, new] = vt[kh * HEAD_DIM:(kh + 1) * HEAD_DIM, :].astype(_BF16)
        yield
        _store_split_heads(k_lo, k_hi, new, _dot(hb, wk_ref[...]))
        yield

    def silu_part(j, f_lo=0, f_hi=ATTN_WIDTH):
        rows = slice(j * PROJ_ROWS, (j + 1) * PROJ_ROWS)
        for f0 in range(f_lo, f_hi, LANES):
            feats = slice(f0, f0 + LANES)
            sgt[feats, rows] = _silu(sgt[feats, rows])
            yield

    def output_part(grp):
        rows = slice(grp * QUERY_GROUP, (grp + 1) * QUERY_GROUP)
        y = _dot_tn(zt[:, rows], wout_ref[...])
        xn = x_ref[0, rows, :] + gmod * y
        if final_norm:
            xn = _rms(xn) * fg_ref[...]
        xo_ref[0, rows, :] = xn
        yield

    lane = lax.broadcasted_iota(jnp.int32, (CHUNK, LANES), 1)

    def key_chunk_mask(grp, kc):
        conds = []
        for i in range(QUERY_GROUP // CHUNK):
            qc = WINDOW_CHUNKS + i
            if not (qc - WINDOW_CHUNKS <= kc <= qc):
                conds.append(lane >= CHUNK if i == 0 else lane < CHUNK)
        if grp == 0 and kc < WINDOW_CHUNKS:
            conds.append(t > 0)
        return functools.reduce(jnp.logical_and, conds) if conds else None


    def score_dots(grp, kh, st):
        r0, c0 = grp * QUERY_GROUP, kh * GQA_GROUP * HEAD_DIM
        q_lanes, band = slice(r0, r0 + QUERY_GROUP), slice(r0, r0 + BAND_KEYS)
        q_rhs = jnp.concatenate([qt[c0 + p * LANES:c0 + (p + 1) * LANES, q_lanes] for p in range(n_pairs)], axis=1)
        st["s_t"] = []
        for kbuf in (k_lo, k_hi):
            st["s_t"].append(_dot(kbuf[kh, band, :], q_rhs))
            yield

    def score_max(grp, kh, st):
        masks = [key_chunk_mask(grp, kc) for kc in range(BAND_KEYS // CHUNK)]
        st["blocks"], st["m"], st["sink_p"] = [], [], []
        for parity in range(HEAD_PAIR):
            for p in range(n_pairs):
                h = kh * GQA_GROUP + p * HEAD_PAIR + parity
                sink2 = sink_ref[layer, h] * LOG2_E
                blocks = []
                for kc, allowed in enumerate(masks):
                    rows = slice(kc * CHUNK, (kc + 1) * CHUNK)
                    s = st["s_t"][parity][rows, p * QUERY_GROUP:(p + 1) * QUERY_GROUP] + bias_t[h, rows, :]
                    blocks.append(s if allowed is None else jnp.where(allowed, s, MASK_VALUE))
                m = jnp.max(functools.reduce(jnp.maximum, blocks), axis=0, keepdims=True)
                m = jnp.maximum(m, sink2)
                st["blocks"].append(blocks)
                st["m"].append(m)
                st["sink_p"].append(jnp.exp2(sink2 - m))
                yield

    def probabilities(grp, kh, st):
        st["p_t"] = []
        for blocks, m in zip(st["blocks"], st["m"]):
            st["p_t"].append(jnp.concatenate([jnp.exp2(b - m).astype(_BF16) for b in blocks], axis=0))
            yield

    def value_dots(grp, kh, st):
        band = slice(grp * QUERY_GROUP, grp * QUERY_GROUP + BAND_KEYS)
        st["acc"] = []
        for parity in range(HEAD_PAIR):
            p_t = jnp.concatenate(st["p_t"][parity * n_pairs:(parity + 1) * n_pairs], axis=1)
            st["acc"].append(_dot(vt_buf[kh, :, band], p_t))
            yield

    def finish(grp, kh, st):
        r0, c0 = grp * QUERY_GROUP, kh * GQA_GROUP * HEAD_DIM
        q_lanes = slice(r0, r0 + QUERY_GROUP)
        for p in range(n_pairs):
            lanes = slice(p * QUERY_GROUP, (p + 1) * QUERY_GROUP)
            heads = []
            for parity in range(HEAD_PAIR):
                acc = st["acc"][parity]
                row_sum = acc[HEAD_DIM:HEAD_DIM + 1, lanes] + st["sink_p"][parity * n_pairs + p]
                heads.append(acc[0:HEAD_DIM, lanes] * (1.0 / row_sum))
            out = jnp.concatenate(heads, axis=0)
            rows = slice(c0 + p * LANES, c0 + (p + 1) * LANES)
            zt[rows, q_lanes] = (out * sgt[rows, q_lanes]).astype(_BF16)
            yield
        st.clear()

    stages = (score_dots, score_max, probabilities, value_dots, finish)
    steps = [(grp, kh) for grp in range(tm // QUERY_GROUP) for kh in range(N_KV_HEADS)]
    state = [{} for _ in steps]
    n_regions = len(steps) + len(stages) - 1
    fill = len(stages) - 1
    per_part = len(steps) // n_parts

    def advance(gen, n):
        for _ in range(n):
            if next(gen, gen) is gen:
                return
            yield

    def spread(gen, first, count, per_region):
        for r in range(first, first + count):
            extras[r].append(advance(gen, per_region))

    extras = [[] for _ in range(n_regions + 1)]
    for j in range(n_parts):
        r0 = j * per_part
        if j > 0:
            spread(silu_part(j), r0, fill, ATTN_WIDTH // LANES // fill)
        if j + 1 < n_parts:
            n_dots = 2 * (ATTN_WIDTH // PROJ_FEATURES) + 2
            spread(project_part(j + 1), r0, per_part - 2, pl.cdiv(n_dots, per_part - 2))
        if j + 2 < n_parts:
            spread(norm_part(j + 2), r0 + per_part - 2, 2, PROJ_ROWS // CHUNK // 2)
    for grp in range(tm // QUERY_GROUP):
        extras[min((grp + 1) * N_KV_HEADS + fill, n_regions)].append(output_part(grp))

    run_region([norm_part(0)])
    first = project_part(0)
    run_region([advance(first, ATTN_WIDTH // PROJ_FEATURES)] + ([norm_part(1)] if n_parts > 1 else []))
    run_region([first, silu_part(0)])
    for r in range(n_regions):
        run_region([stage(*steps[r - j], state[r - j]) for j, stage in enumerate(stages) if 0 <= r - j < len(steps)]
                   + extras[r])
    run_region(extras[n_regions])

    @pl.when(t == nt - 1)
    def _emit_window():
        last = hbuf[tm - WINDOW:tm, :]
        kw_ref[0] = _dot(last, wk_ref[...])
        vw_ref[0] = _dot(last, wv_ref[...])

    for buf in (k_lo, k_hi):
        buf[:, 0:WINDOW, :] = buf[:, tm:tm + WINDOW, :]
    vt_buf[:, :, 0:WINDOW] = vt_buf[:, :, tm:tm + WINDOW]


def _split_attn_w_in(win):
    q0, k0, v0, g0 = 0, ATTN_WIDTH, ATTN_WIDTH + KV_WIDTH, ATTN_WIDTH + 2 * KV_WIDTH
    t = lambda w: jnp.swapaxes(w, 1, 2)
    wk, wv = win[:, :, k0:v0], win[:, :, v0:g0]
    return t(win[:, :, q0:k0]), wk, t(wv), t(win[:, :, g0:g0 + ATTN_WIDTH]), wv


def _attn_prompt_layer(x, mods, norm_g, win_parts, sinks, wout, final_g, windows, *, i, row0, tm):
    B, T, D = x.shape
    nt = T // tm
    j = i // 2
    final_norm = final_g is not None
    in_specs = [
        pl.BlockSpec(memory_space=pltpu.SMEM),
        pl.BlockSpec((1, tm, D), lambda b, t: (b, t, 0)),
        _mod_spec(i, row0),
        _stacked(i, 1, D),
        _stacked(j, ATTN_WIDTH, D), _stacked(j, D, KV_WIDTH), _stacked(j, KV_WIDTH, D), _stacked(j, ATTN_WIDTH, D),
        _stacked(j, D, KV_WIDTH),
        _stacked(j, ATTN_WIDTH, D),
    ]
    args = [sinks, x, mods, norm_g, *win_parts, wout]
    if final_norm:
        in_specs.append(pl.BlockSpec((1, D), lambda b, t: (0, 0)))
        args.append(final_g)
    window_spec = pl.BlockSpec((None, 1, WINDOW, KV_WIDTH), lambda b, t: (j, b, 0, 0))
    window_shape = jax.ShapeDtypeStruct((win_parts[0].shape[0], B, WINDOW, KV_WIDTH), _F32)
    carry_specs, carry_args, aliases = _carry(windows, first_output=1)
    return pl.pallas_call(
        functools.partial(_attn_prompt_kernel, tm=tm, nt=nt, layer=j, final_norm=final_norm,
                          n_carried=len(carry_args)),
        grid=(B, nt),
        in_specs=in_specs + carry_specs,
        out_specs=[pl.BlockSpec((1, tm, D), lambda b, t: (b, t, 0)), window_spec, window_spec],
        out_shape=[jax.ShapeDtypeStruct((B, T, D), _F32), window_shape, window_shape],
        input_output_aliases=aliases(len(args)),
        scratch_shapes=[
            pltpu.VMEM((ATTN_WIDTH, tm), _BF16),
            pltpu.VMEM((N_KV_HEADS, WINDOW + tm, LANES), _BF16),
            pltpu.VMEM((N_KV_HEADS, WINDOW + tm, LANES), _BF16),
            pltpu.VMEM((N_KV_HEADS, VT_ROWS, WINDOW + tm), _BF16),
            pltpu.VMEM((ATTN_WIDTH, tm), _F32),
            pltpu.VMEM((ATTN_WIDTH, tm), _BF16),
            pltpu.VMEM((N_HEADS, BAND_KEYS, QUERY_GROUP), _F32),
            pltpu.VMEM((tm, D), _BF16),
        ],
        compiler_params=pltpu.CompilerParams(
            dimension_semantics=("arbitrary", "arbitrary"), vmem_limit_bytes=VMEM_LIMIT_BYTES),
        name="attn_layer_prompt",
    )(*args, *carry_args)


def kernel(x_prompt, x_sample, c_prompt, c_sample, cache_pool, cache_k, cache_v, norm_g, ada_w, ada_b,
           pool_w_in, pool_w_grp, pool_scale, pool_w_out, attn_w_in, attn_sinks, attn_w_out, final_g):
    n_prompt, n_sample = x_prompt.shape[0], x_sample.shape[0]
    mods = _modulation(jnp.concatenate([c_prompt, c_sample], axis=0), ada_w, ada_b)
    mods = mods.reshape(DEPTH, n_prompt + n_sample, 3, D_MODEL)

    norm_g = norm_g.reshape(DEPTH, 1, D_MODEL)
    final_g = final_g.reshape(1, D_MODEL)
    pool_w_in, pool_w_grp, pool_w_out = (w.astype(_BF16) for w in (pool_w_in, pool_w_grp, pool_w_out))
    pool_scale = pool_scale.reshape(-1, 1, POOL_WIDTH)
    attn_w_in, attn_w_out = attn_w_in.astype(_BF16), attn_w_out.astype(_BF16)
    attn_w_in_parts = _split_attn_w_in(attn_w_in)
    hist = jnp.pad(cache_pool, ((0, 0), (0, 0), (HIST_ROWS - POOL_HIST, 0), (0, 0)))
    cache_k = cache_k.reshape(cache_k.shape[:3] + (KV_WIDTH,))
    cache_v = cache_v.reshape(cache_v.shape[:3] + (KV_WIDTH,))

    def trunk(x, row0, sample):
        rows = x.shape[1]
        states, windows = None, None
        for i in range(DEPTH):
            if i % 2 == 0:
                x, states = _pool_layer(x, mods, norm_g, pool_w_in, pool_w_grp, pool_scale, pool_w_out,
                                        hist if sample else None, states, i=i, row0=row0,
                                        tm=rows if sample else min(POOL_PROMPT_TILE, rows))
            else:
                fg = final_g if i == DEPTH - 1 else None
                if sample:
                    x, *windows = _attn_sample_layer(x, mods, norm_g, attn_w_in, attn_sinks, attn_w_out,
                                                     cache_k, cache_v, fg, windows, i=i, row0=row0)
                else:
                    x, *windows = _attn_prompt_layer(x, mods, norm_g, attn_w_in_parts, attn_sinks, attn_w_out, fg,
                                                     windows, i=i, row0=row0, tm=min(ATTN_PROMPT_TILE, rows))
        k_win, v_win = (w.reshape(w.shape[:3] + (N_KV_HEADS, HEAD_DIM)) for w in windows)
        return x, states[:, :, HIST_ROWS - POOL_HIST:], k_win, v_win

    y_prompt, pool_p, k_p, v_p = trunk(x_prompt, 0, False)
    y_sample, pool_s, k_s, v_s = trunk(x_sample, n_prompt, True)
    return (y_prompt, y_sample, pool_p, k_p, v_p, pool_s, k_s, v_s)
```

```python
import functools

import numpy as np
import jax
import jax.numpy as jnp
from jax import lax
from jax.experimental import pallas as pl
from jax.experimental.pallas import tpu as pltpu

D_MODEL = 1024
DEPTH = 4
PAST_LEN = 1024
CHUNK = 64
N_HEADS = 16
HEAD_DIM = 64
N_KV_HEADS = 4
GQA_GROUP = N_HEADS // N_KV_HEADS
ATTN_WIDTH = N_HEADS * HEAD_DIM
KV_WIDTH = N_KV_HEADS * HEAD_DIM
WINDOW = 128
WINDOW_CHUNKS = WINDOW // CHUNK
POOL_WIDTH = D_MODEL
POOL_WINDOWS = (2, 4, 8, 16)
POOL_GROUP_WIDTH = POOL_WIDTH // len(POOL_WINDOWS)
POOL_HIST = max(POOL_WINDOWS) - 1
NORM_EPS = 1e-6
MASK_VALUE = -1e30
LOG2_E = 1.4426950408889634

LANES = 128
SUBLANES = 8
VMEM_LIMIT_BYTES = 56 * 1024 * 1024

HIST_ROWS = 2 * SUBLANES
HEAD_PAIR = LANES // HEAD_DIM
BAND_KEYS = 2 * WINDOW
POOL_PROMPT_TILE = 1024
ATTN_PROMPT_TILE = 1024
QUERY_GROUP = 2 * CHUNK
PROJ_ROWS = 256
PROJ_FEATURES = 512

_F32 = jnp.float32
_BF16 = jnp.bfloat16


def _alibi_slopes():
    h = np.arange(1, N_HEADS + 1, dtype=np.float32)
    return [float(s) for s in np.exp2(np.float32(-8.0) * h / np.float32(N_HEADS))]


def _rms(x):
    return x * lax.rsqrt(jnp.mean(x * x, axis=-1, keepdims=True) + NORM_EPS)


def _modulated_norm(x, g, shift, scale):
    return (_rms(x) * g) * (1.0 + scale) + shift


def _silu(x):
    h = 0.5 * x
    return h + h * jnp.tanh(h)


def _emit_round_robin(gens):
    live = list(gens)
    while live:
        live = [g for g in live if next(g, live) is not live]


def _dot(a, b):
    return jnp.dot(a, b, preferred_element_type=_F32)


def _dot_nt(a, b):
    return lax.dot_general(a, b, (((1,), (1,)), ((), ())), preferred_element_type=_F32)


def _mod_kernel(c_ref, w_ref, b_ref, o_ref):
    c = c_ref[...]
    o_ref[0] = _dot(_silu(c).astype(_BF16), w_ref[0].astype(_BF16)) + b_ref[0]


def _modulation(c_all, ada_w, ada_b):
    rows = c_all.shape[0]
    n_col_tiles = 3
    return pl.pallas_call(
        _mod_kernel,
        grid=(DEPTH, n_col_tiles),
        in_specs=[
            pl.BlockSpec((rows, D_MODEL), lambda i, n: (0, 0)),
            pl.BlockSpec((1, D_MODEL, D_MODEL), lambda i, n: (i, 0, n)),
            pl.BlockSpec((1, 1, D_MODEL), lambda i, n: (i, 0, n)),
        ],
        out_specs=pl.BlockSpec((1, rows, D_MODEL), lambda i, n: (i, 0, n)),
        out_shape=jax.ShapeDtypeStruct((DEPTH, rows, 3 * D_MODEL), _F32),
        compiler_params=pltpu.CompilerParams(
            dimension_semantics=("arbitrary", "arbitrary"), vmem_limit_bytes=VMEM_LIMIT_BYTES),
        name="adaln_modulation",
    )(c_all, ada_w, ada_b.reshape(DEPTH, 1, 3 * D_MODEL))


def _pool_kernel(*refs, n_seg, seg, nt, sample, n_carried):
    refs = list(refs)
    x_ref, mod_ref, g_ref, win_ref, wgrp_ref, psc_ref, wout_ref = refs[:7]
    del refs[:7]
    if sample:
        hist_ref = refs.pop(0)
    del refs[:n_carried]
    xo_ref, st_ref, ubuf, sgbuf, pbuf, zbuf, hbuf = refs
    t = pl.program_id(1)
    tm = n_seg * seg

    @pl.when(t == 0)
    def _load_history():
        if sample:
            ubuf[:, 0:HIST_ROWS, :] = hist_ref[...]
        else:
            ubuf[:, 0:HIST_ROWS, :] = jnp.zeros((n_seg, HIST_ROWS, POOL_WIDTH), _F32)

    part = min(PROJ_ROWS, tm)
    piece = min(part, seg)
    rc = min(CHUNK, piece)
    group_cols = [slice(g * POOL_GROUP_WIDTH, (g + 1) * POOL_GROUP_WIDTH) for g in range(len(POOL_WINDOWS))]

    def pieces(p, n):
        return [(r // seg, r % seg, r - p * part) for r in range(p * part, (p + 1) * part, n)]


    def norm_part(p):
        for b, o, i in pieces(p, rc):
            h = _modulated_norm(x_ref[b, o:o + rc, :], g_ref[...], mod_ref[b, 0:1, :], mod_ref[b, 1:2, :])
            hbuf[p * part + i:p * part + i + rc, :] = h.astype(_BF16)
            yield

    def project_part(p):
        rows = slice(p * part, (p + 1) * part)
        hb = hbuf[rows, :]
        u = _dot(hb, win_ref[:, 0:POOL_WIDTH])
        for b, o, i in pieces(p, piece):
            ubuf[b, HIST_ROWS + o:HIST_ROWS + o + piece, :] = u[i:i + piece]
        yield
        sgbuf[rows, :] = _silu(_dot(hb, win_ref[:, POOL_WIDTH:2 * POOL_WIDTH]))
        yield

    def pool_part(p):
        for b, o, i in pieces(p, rc):
            rows = slice(p * part + i, p * part + i + rc)
            for cols, w in zip(group_cols, POOL_WINDOWS):
                ext = ubuf[b, o:o + HIST_ROWS + rc, cols]
                s = ext
                k = 1
                while k < w:
                    s = s + pltpu.roll(s, k, axis=0)
                    k *= 2
                s, u = s[HIST_ROWS:], ext[HIST_ROWS:]
                if sample or o > 0:
                    pooled = s * (1.0 / w) - u
                else:
                    pos = t * seg + lax.broadcasted_iota(jnp.int32, (rc, POOL_GROUP_WIDTH), 0)
                    pooled = s / jnp.minimum(pos + 1, w).astype(_F32) - u
                pbuf[rows, cols] = pooled.astype(_BF16)
            yield

    def mix_part(p):
        rows = slice(p * part, (p + 1) * part)
        for g, cols in enumerate(group_cols):
            z = _dot(pbuf[rows, cols], wgrp_ref[g]) * psc_ref[:, cols]
            zbuf[rows, cols] = (z * sgbuf[rows, cols]).astype(_BF16)
            yield

    def output_part(p):
        y = _dot(zbuf[p * part:(p + 1) * part, :], wout_ref[...])
        for b, o, i in pieces(p, piece):
            xo_ref[b, o:o + piece, :] = x_ref[b, o:o + piece, :] + mod_ref[b, 2:3, :] * y[i:i + piece]
        yield

    stages = (norm_part, project_part, pool_part, mix_part, output_part)
    n_parts = tm // part
    for r in range(n_parts + len(stages) - 1):
        _emit_round_robin([stage(r - j) for j, stage in enumerate(stages) if 0 <= r - j < n_parts])

    @pl.when(t == nt - 1)
    def _emit_state():
        st_ref[...] = ubuf[:, seg:seg + HIST_ROWS, :]

    ubuf[:, 0:HIST_ROWS, :] = ubuf[:, seg:seg + HIST_ROWS, :]


def _stacked(index, *shape):
    return pl.BlockSpec((None,) + shape, lambda b, t: (index,) + (0,) * len(shape))


def _mod_spec(i, row0):
    return pl.BlockSpec((None, 1, 3, D_MODEL), lambda b, t: (i, row0 + b, 0, 0))


def _carry(carried, first_output):
    carried = list(carried)
    specs = [pl.BlockSpec(memory_space=pl.ANY)] * len(carried)
    return specs, carried, lambda n_in: {n_in + k: first_output + k for k in range(len(carried))}


def _pool_layer(x, mods, norm_g, win, wgrp, psc, wout, hist, states, *, i, row0, tm):
    B, T, D = x.shape
    j = i // 2
    n_layers = win.shape[0]
    sample = hist is not None
    n_seg, seg = (B, T) if sample else (1, tm)
    grid = (B // n_seg, T // seg)
    assert row0 % n_seg == 0
    in_specs = [
        pl.BlockSpec((n_seg, seg, D), lambda b, t: (b, t, 0)),
        pl.BlockSpec((None, n_seg, 3, D), lambda b, t: (i, row0 // n_seg + b, 0, 0)),
        _stacked(i, 1, D),
        _stacked(j, D, 2 * POOL_WIDTH),
        _stacked(j, len(POOL_WINDOWS), POOL_GROUP_WIDTH, POOL_GROUP_WIDTH),
        _stacked(j, 1, POOL_WIDTH),
        _stacked(j, POOL_WIDTH, D),
    ]
    args = [x, mods, norm_g, win, wgrp, psc, wout]
    if sample:
        in_specs.append(pl.BlockSpec((None, n_seg, HIST_ROWS, POOL_WIDTH), lambda b, t: (j, b, 0, 0)))
        args.append(hist)
    tm = n_seg * seg
    carry_specs, carry_args, aliases = _carry([states], first_output=1)
    return pl.pallas_call(
        functools.partial(_pool_kernel, n_seg=n_seg, seg=seg, nt=grid[1], sample=sample, n_carried=len(carry_args)),
        grid=grid,
        in_specs=in_specs + carry_specs,
        out_specs=[
            pl.BlockSpec((n_seg, seg, D), lambda b, t: (b, t, 0)),
            pl.BlockSpec((None, n_seg, HIST_ROWS, POOL_WIDTH), lambda b, t: (j, b, 0, 0)),
        ],
        out_shape=[
            jax.ShapeDtypeStruct((B, T, D), _F32),
            jax.ShapeDtypeStruct((n_layers, B, HIST_ROWS, POOL_WIDTH), _F32),
        ],
        input_output_aliases=aliases(len(args)),
        scratch_shapes=[
            pltpu.VMEM((n_seg, HIST_ROWS + seg, POOL_WIDTH), _F32),
            pltpu.VMEM((tm, POOL_WIDTH), _F32),
            pltpu.VMEM((tm, POOL_WIDTH), _BF16),
            pltpu.VMEM((tm, POOL_WIDTH), _BF16),
            pltpu.VMEM((tm, D), _BF16),
        ],
        compiler_params=pltpu.CompilerParams(
            dimension_semantics=("arbitrary", "arbitrary"), vmem_limit_bytes=VMEM_LIMIT_BYTES),
        name="pool_layer_sample" if sample else "pool_layer_prompt",
    )(*args, *carry_args)


def _store_split_heads(dst_lo, dst_hi, rows, val):
    n = val.shape[0]
    lo = lax.broadcasted_iota(jnp.int32, (n, LANES), 1) < HEAD_DIM
    zero = jnp.zeros((n, LANES), _F32)
    for j in range(KV_WIDTH // LANES):
        a = val[:, j * LANES:(j + 1) * LANES]
        ar = pltpu.roll(a, HEAD_DIM, axis=1)
        dst_lo[HEAD_PAIR * j, rows, :] = jnp.where(lo, a, zero).astype(_BF16)
        dst_hi[HEAD_PAIR * j, rows, :] = jnp.where(lo, zero, ar).astype(_BF16)
        dst_lo[HEAD_PAIR * j + 1, rows, :] = jnp.where(lo, ar, zero).astype(_BF16)
        dst_hi[HEAD_PAIR * j + 1, rows, :] = jnp.where(lo, zero, a).astype(_BF16)


def _attn_sample_kernel(*refs, n_batch, tm, layer, final_norm, n_carried):
    refs = list(refs)
    sink_ref, x_ref, mod_ref, g_ref, win_ref, wout_ref, ck_ref, cv_ref = refs[:8]
    del refs[:8]
    if final_norm:
        fg_ref = refs.pop(0)
    del refs[:n_carried]
    xo_ref, kw_ref, vw_ref, qbuf, k_lo, k_hi, v_lo, v_hi, sgbuf, zbuf, bias_ref, hbuf = refs
    batch_rows = [slice(b * tm, (b + 1) * tm) for b in range(n_batch)]

    for b, rows in enumerate(batch_rows):
        h = _modulated_norm(x_ref[b], g_ref[...], mod_ref[b, 0:1, :], mod_ref[b, 1:2, :])
        hbuf[rows, :] = h.astype(_BF16)
    hb = hbuf[...]
    q0, k0, v0, g0 = 0, ATTN_WIDTH, ATTN_WIDTH + KV_WIDTH, ATTN_WIDTH + 2 * KV_WIDTH
    qbuf[...] = (_dot(hb, win_ref[:, q0:k0]) * (HEAD_DIM ** -0.5 * LOG2_E)).astype(_BF16)
    k = _dot(hb, win_ref[:, k0:v0])
    v = _dot(hb, win_ref[:, v0:g0])
    sgbuf[...] = _silu(_dot(hb, win_ref[:, g0:g0 + ATTN_WIDTH]))
    for b, rows in enumerate(batch_rows):
        band0 = b * BAND_KEYS
        _store_split_heads(k_lo, k_hi, slice(band0, band0 + WINDOW), ck_ref[b])
        _store_split_heads(v_lo, v_hi, slice(band0, band0 + WINDOW), cv_ref[b])
        _store_split_heads(k_lo, k_hi, slice(band0 + WINDOW, band0 + WINDOW + tm), k[rows])
        _store_split_heads(v_lo, v_hi, slice(band0 + WINDOW, band0 + WINDOW + tm), v[rows])
        for buf in (k_lo, k_hi, v_lo, v_hi):
            buf[:, band0 + WINDOW + tm:band0 + BAND_KEYS, :] = jnp.zeros(
                (N_KV_HEADS, BAND_KEYS - WINDOW - tm, LANES), _BF16)
        kw_ref[b, 0:WINDOW - tm, :] = ck_ref[b, tm:WINDOW, :]
        vw_ref[b, 0:WINDOW - tm, :] = cv_ref[b, tm:WINDOW, :]
        kw_ref[b, WINDOW - tm:WINDOW, :] = k[rows]
        vw_ref[b, WINDOW - tm:WINDOW, :] = v[rows]

    qi = lax.broadcasted_iota(jnp.int32, (tm, BAND_KEYS), 0)
    sj = lax.broadcasted_iota(jnp.int32, (tm, BAND_KEYS), 1)
    dist = jnp.abs(WINDOW + qi - sj).astype(_F32)
    for h, slope in enumerate(_alibi_slopes()):
        bias_ref[h] = dist * (-slope * LOG2_E)

    lane = lax.broadcasted_iota(jnp.int32, (tm, LANES), 1)
    lo_lanes = lax.broadcasted_iota(jnp.int32, (HEAD_PAIR * tm, LANES), 1) < HEAD_DIM
    key_lo_lanes = lax.broadcasted_iota(jnp.int32, (BAND_KEYS, LANES), 1) < HEAD_DIM
    ones_lo = jnp.where(key_lo_lanes, 1.0, 0.0).astype(_BF16)
    ones_hi = jnp.where(key_lo_lanes, 0.0, 1.0).astype(_BF16)

    qc = WINDOW // CHUNK
    masks = []
    for col in range(BAND_KEYS // LANES):
        conds = []
        for half in range(LANES // CHUNK):
            kc = (col * LANES) // CHUNK + half
            if not (qc - WINDOW_CHUNKS <= kc <= qc):
                conds.append(lane >= CHUNK if half == 0 else lane < CHUNK)
        if col * LANES + LANES > WINDOW + tm:
            conds.append(lane < WINDOW + tm - col * LANES)
        masks.append(functools.reduce(jnp.logical_and, conds) if conds else None)

    n_pairs = GQA_GROUP // HEAD_PAIR

    def score_dots(b, kh, st):
        c0 = kh * GQA_GROUP * HEAD_DIM
        qs = jnp.concatenate(
            [qbuf[batch_rows[b], c0 + p * LANES:c0 + (p + 1) * LANES] for p in range(n_pairs)], axis=0)
        st["s"] = [_dot_nt(qs, kbuf[kh, b * BAND_KEYS:(b + 1) * BAND_KEYS, :]) for kbuf in (k_lo, k_hi)]
        yield

    def probabilities(b, kh, st):
        st["probs"], st["sink_p"] = [], []
        for parity in range(HEAD_PAIR):
            p_blocks, sink_blocks = [], []
            for p in range(n_pairs):
                h = kh * GQA_GROUP + p * HEAD_PAIR + parity
                sink2 = sink_ref[layer, h] * LOG2_E
                cols = []
                for col, allowed in enumerate(masks):
                    lanes = slice(col * LANES, (col + 1) * LANES)
                    s = st["s"][parity][p * tm:(p + 1) * tm, lanes] + bias_ref[h, :, lanes]
                    cols.append(s if allowed is None else jnp.where(allowed, s, MASK_VALUE))
                m = jnp.maximum(jnp.max(functools.reduce(jnp.maximum, cols), axis=-1, keepdims=True), sink2)
                sink_blocks.append(jnp.exp2(sink2 - m))
                p_blocks.append(jnp.concatenate([jnp.exp2(c - m).astype(_BF16) for c in cols], axis=1))
                yield
            st["probs"].append(jnp.concatenate(p_blocks, axis=0))
            st["sink_p"].append(jnp.concatenate(sink_blocks, axis=0))

    def value_dots(b, kh, st):
        band = slice(b * BAND_KEYS, (b + 1) * BAND_KEYS)
        st["acc"] = (_dot(st["probs"][0], jnp.concatenate([v_lo[kh, band, :], ones_lo], axis=1))
                     + _dot(st["probs"][1], jnp.concatenate([v_hi[kh, band, :], ones_hi], axis=1)))
        yield

    def finish(b, kh, st):
        rows, c0 = batch_rows[b], kh * GQA_GROUP * HEAD_DIM
        acc = st["acc"]
        o = acc[:, :LANES] / (acc[:, LANES:] + jnp.where(lo_lanes, st["sink_p"][0], st["sink_p"][1]))
        for p in range(n_pairs):
            cols = slice(c0 + p * LANES, c0 + (p + 1) * LANES)
            zbuf[rows, cols] = (o[p * tm:(p + 1) * tm] * sgbuf[rows, cols]).astype(_BF16)
        st.clear()
        yield

    stages = (score_dots, probabilities, value_dots, finish)
    steps = [(b, kh) for b in range(n_batch) for kh in range(N_KV_HEADS)]
    state = [{} for _ in steps]
    for r in range(len(steps) + len(stages) - 1):
        _emit_round_robin(
            [stage(*steps[r - j], state[r - j]) for j, stage in enumerate(stages) if 0 <= r - j < len(steps)])

    y = _dot(zbuf[...], wout_ref[...])
    for b, rows in enumerate(batch_rows):
        xn = x_ref[b] + mod_ref[b, 2:3, :] * y[rows]
        if final_norm:
            xn = _rms(xn) * fg_ref[...]
        xo_ref[b] = xn


def _attn_sample_layer(x, mods, norm_g, win, sinks, wout, cache_k, cache_v, final_g, windows, *, i, row0):
    B, tm, D = x.shape
    assert tm < CHUNK and PAST_LEN % CHUNK == 0 and row0 % B == 0
    j = i // 2
    n_layers = win.shape[0]
    final_norm = final_g is not None
    whole = lambda *shape: pl.BlockSpec(shape, lambda b, t: (0,) * len(shape))
    in_specs = [
        pl.BlockSpec(memory_space=pltpu.SMEM),
        whole(B, tm, D),
        pl.BlockSpec((None, B, 3, D), lambda b, t: (i, row0 // B, 0, 0)),
        _stacked(i, 1, D),
        _stacked(j, D, 2 * ATTN_WIDTH + 2 * KV_WIDTH),
        _stacked(j, ATTN_WIDTH, D),
        _stacked(j, B, WINDOW, KV_WIDTH),
        _stacked(j, B, WINDOW, KV_WIDTH),
    ]
    args = [sinks, x, mods, norm_g, win, wout, cache_k, cache_v]
    if final_norm:
        in_specs.append(whole(1, D))
        args.append(final_g)
    window_shape = jax.ShapeDtypeStruct((n_layers, B, WINDOW, KV_WIDTH), _F32)
    carry_specs, carry_args, aliases = _carry(windows, first_output=1)
    return pl.pallas_call(
        functools.partial(_attn_sample_kernel, n_batch=B, tm=tm, layer=j, final_norm=final_norm,
                          n_carried=len(carry_args)),
        grid=(1, 1),
        in_specs=in_specs + carry_specs,
        out_specs=[whole(B, tm, D), _stacked(j, B, WINDOW, KV_WIDTH), _stacked(j, B, WINDOW, KV_WIDTH)],
        out_shape=[jax.ShapeDtypeStruct((B, tm, D), _F32), window_shape, window_shape],
        input_output_aliases=aliases(len(args)),
        scratch_shapes=[
            pltpu.VMEM((B * tm, ATTN_WIDTH), _BF16),
            pltpu.VMEM((N_KV_HEADS, B * BAND_KEYS, LANES), _BF16),
            pltpu.VMEM((N_KV_HEADS, B * BAND_KEYS, LANES), _BF16),
            pltpu.VMEM((N_KV_HEADS, B * BAND_KEYS, LANES), _BF16),
            pltpu.VMEM((N_KV_HEADS, B * BAND_KEYS, LANES), _BF16),
            pltpu.VMEM((B * tm, ATTN_WIDTH), _F32),
            pltpu.VMEM((B * tm, ATTN_WIDTH), _BF16),
            pltpu.VMEM((N_HEADS, tm, BAND_KEYS), _F32),
            pltpu.VMEM((B * tm, D), _BF16),
        ],
        compiler_params=pltpu.CompilerParams(
            dimension_semantics=("arbitrary", "arbitrary"), vmem_limit_bytes=VMEM_LIMIT_BYTES),
        name="attn_layer_sample",
    )(*args, *carry_args)


SUM_ROWS = 2 * SUBLANES
VT_ROWS = HEAD_DIM + SUM_ROWS


def _dot_tn(a, b):
    return lax.dot_general(a, b, (((0,), (0,)), ((), ())), preferred_element_type=_F32)


def _attn_prompt_kernel(*refs, tm, nt, layer, final_norm, n_carried):
    refs = list(refs)
    (sink_ref, x_ref, mod_ref, g_ref, wqt_ref, wk_ref, wvt_ref, wgt_ref, wv_ref, wout_ref) = refs[:10]
    del refs[:10]
    if final_norm:
        fg_ref = refs.pop(0)
    del refs[:n_carried]
    xo_ref, kw_ref, vw_ref, qt, k_lo, k_hi, vt_buf, sgt, zt, bias_t, hbuf = refs
    t = pl.program_id(1)
    n_pairs = GQA_GROUP // HEAD_PAIR
    slopes = _alibi_slopes()

    @pl.when(t == 0)
    def _reset_window():
        for buf in (k_lo, k_hi):
            buf[:, 0:WINDOW, :] = jnp.zeros((N_KV_HEADS, WINDOW, LANES), _BF16)
        row = lax.broadcasted_iota(jnp.int32, (VT_ROWS, WINDOW + tm), 0)
        ones_rows = (row >= HEAD_DIM) & (row < HEAD_DIM + SUBLANES)
        for kh in range(N_KV_HEADS):
            vt_buf[kh] = jnp.where(ones_rows, 1.0, 0.0).astype(_BF16)
        si = lax.broadcasted_iota(jnp.int32, (BAND_KEYS, QUERY_GROUP), 0)
        qj = lax.broadcasted_iota(jnp.int32, (BAND_KEYS, QUERY_GROUP), 1)
        dist = jnp.abs(WINDOW + qj - si).astype(_F32)
        for h in range(N_HEADS):
            bias_t[h] = dist * (-slopes[h] * LOG2_E)

    shift, scale, gmod = mod_ref[0, 0:1, :], mod_ref[0, 1:2, :], mod_ref[0, 2:3, :]
    n_parts = tm // PROJ_ROWS


    run_region = _emit_round_robin

    def norm_part(j):
        for r0 in range(j * PROJ_ROWS, (j + 1) * PROJ_ROWS, CHUNK):
            rows = slice(r0, r0 + CHUNK)
            hbuf[rows, :] = _modulated_norm(x_ref[0, rows, :], g_ref[...], shift, scale).astype(_BF16)
            yield

    def project_part(j):
        rows = slice(j * PROJ_ROWS, (j + 1) * PROJ_ROWS)
        new = slice(WINDOW + j * PROJ_ROWS, WINDOW + (j + 1) * PROJ_ROWS)
        hb = hbuf[rows, :]
        for f0 in range(0, ATTN_WIDTH, PROJ_FEATURES):
            feats = slice(f0, f0 + PROJ_FEATURES)
            sgt[feats, rows] = _silu(_dot_nt(wgt_ref[feats, :], hb))
            yield
        for f0 in range(0, ATTN_WIDTH, PROJ_FEATURES):
            feats = slice(f0, f0 + PROJ_FEATURES)
            qt[feats, rows] = (_dot_nt(wqt_ref[feats, :], hb) * (HEAD_DIM ** -0.5 * LOG2_E)).astype(_BF16)
            yield
        vt = _dot_nt(wvt_ref[...], hb)
        for kh in range(N_KV_HEADS):
            vt_buf[kh, 0:HEAD_DIM, new] = vt[kh * HEAD_DIM:(kh + 1) * HEAD_DIM, :].astype(_BF16)
        yield
        _store_split_heads(k_lo, k_hi, new, _dot(hb, wk_ref[...]))
        yield

    def output_part(grp):
        rows = slice(grp * QUERY_GROUP, (grp + 1) * QUERY_GROUP)
        y = _dot_tn(zt[:, rows], wout_ref[...])
        xn = x_ref[0, rows, :] + gmod * y
        if final_norm:
            xn = _rms(xn) * fg_ref[...]
        xo_ref[0, rows, :] = xn
        yield

    lane = lax.broadcasted_iota(jnp.int32, (CHUNK, LANES), 1)

    def key_chunk_mask(grp, kc):
        conds = []
        for i in range(QUERY_GROUP // CHUNK):
            qc = WINDOW_CHUNKS + i
            if not (qc - WINDOW_CHUNKS <= kc <= qc):
                conds.append(lane >= CHUNK if i == 0 else lane < CHUNK)
        if grp == 0 and kc < WINDOW_CHUNKS:
            conds.append(t > 0)
        return functools.reduce(jnp.logical_and, conds) if conds else None


    def score_dots(grp, kh, st):
        r0, c0 = grp * QUERY_GROUP, kh * GQA_GROUP * HEAD_DIM
        q_lanes, band = slice(r0, r0 + QUERY_GROUP), slice(r0, r0 + BAND_KEYS)
        q_rhs = jnp.concatenate([qt[c0 + p * LANES:c0 + (p + 1) * LANES, q_lanes] for p in range(n_pairs)], axis=1)
        st["s_t"] = []
        for kbuf in (k_lo, k_hi):
            st["s_t"].append(_dot(kbuf[kh, band, :], q_rhs))
            yield

    def score_max(grp, kh, st):
        masks = [key_chunk_mask(grp, kc) for kc in range(BAND_KEYS // CHUNK)]
        st["blocks"], st["m"], st["sink_p"] = [], [], []
        for parity in range(HEAD_PAIR):
            for p in range(n_pairs):
                h = kh * GQA_GROUP + p * HEAD_PAIR + parity
                sink2 = sink_ref[layer, h] * LOG2_E
                blocks = []
                for kc, allowed in enumerate(masks):
                    rows = slice(kc * CHUNK, (kc + 1) * CHUNK)
                    s = st["s_t"][parity][rows, p * QUERY_GROUP:(p + 1) * QUERY_GROUP] + bias_t[h, rows, :]
                    blocks.append(s if allowed is None else jnp.where(allowed, s, MASK_VALUE))
                m = jnp.max(functools.reduce(jnp.maximum, blocks), axis=0, keepdims=True)
                m = jnp.maximum(m, sink2)
                st["blocks"].append(blocks)
                st["m"].append(m)
                st["sink_p"].append(jnp.exp2(sink2 - m))
                yield

    def probabilities(grp, kh, st):
        st["p_t"] = []
        for blocks, m in zip(st["blocks"], st["m"]):
            st["p_t"].append(jnp.concatenate([jnp.exp2(b - m).astype(_BF16) for b in blocks], axis=0))
            yield

    def value_dots(grp, kh, st):
        band = slice(grp * QUERY_GROUP, grp * QUERY_GROUP + BAND_KEYS)
        st["acc"] = []
        for parity in range(HEAD_PAIR):
            p_t = jnp.concatenate(st["p_t"][parity * n_pairs:(parity + 1) * n_pairs], axis=1)
            st["acc"].append(_dot(vt_buf[kh, :, band], p_t))
            yield

    def finish(grp, kh, st):
        r0, c0 = grp * QUERY_GROUP, kh * GQA_GROUP * HEAD_DIM
        q_lanes = slice(r0, r0 + QUERY_GROUP)
        for p in range(n_pairs):
            lanes = slice(p * QUERY_GROUP, (p + 1) * QUERY_GROUP)
            heads = []
            for parity in range(HEAD_PAIR):
                acc = st["acc"][parity]
                row_sum = acc[HEAD_DIM:HEAD_DIM + 1, lanes] + st["sink_p"][parity * n_pairs + p]
                heads.append(acc[0:HEAD_DIM, lanes] * (1.0 / row_sum))
            out = jnp.concatenate(heads, axis=0)
            rows = slice(c0 + p * LANES, c0 + (p + 1) * LANES)
            zt[rows, q_lanes] = (out * sgt[rows, q_lanes]).astype(_BF16)
            yield
        st.clear()

    stages = (score_dots, score_max, probabilities, value_dots, finish)
    steps = [(grp, kh) for grp in range(tm // QUERY_GROUP) for kh in range(N_KV_HEADS)]
    state = [{} for _ in steps]
    n_regions = len(steps) + len(stages) - 1
    fill = len(stages) - 1
    per_part = len(steps) // n_parts

    def advance(gen, n):
        for _ in range(n):
            if next(gen, gen) is gen:
                return
            yield

    def spread(gen, first, count, per_region):
        for r in range(first, first + count):
            extras[r].append(advance(gen, per_region))

    extras = [[] for _ in range(n_regions + 1)]
    for j in range(n_parts):
        r0 = j * per_part
        if j + 1 < n_parts:
            n_dots = 2 * (ATTN_WIDTH // PROJ_FEATURES) + 2
            spread(project_part(j + 1), r0, per_part - 2, pl.cdiv(n_dots, per_part - 2))
        if j + 2 < n_parts:
            spread(norm_part(j + 2), r0 + per_part - 2, 2, PROJ_ROWS // CHUNK // 2)
    for grp in range(tm // QUERY_GROUP):
        extras[min((grp + 1) * N_KV_HEADS + fill, n_regions)].append(output_part(grp))

    run_region([norm_part(0)])
    run_region([project_part(0)] + ([norm_part(1)] if n_parts > 1 else []))
    for r in range(n_regions):
        run_region([stage(*steps[r - j], state[r - j]) for j, stage in enumerate(stages) if 0 <= r - j < len(steps)]
                   + extras[r])
    run_region(extras[n_regions])

    @pl.when(t == nt - 1)
    def _emit_window():
        last = hbuf[tm - WINDOW:tm, :]
        kw_ref[0] = _dot(last, wk_ref[...])
        vw_ref[0] = _dot(last, wv_ref[...])

    for buf in (k_lo, k_hi):
        buf[:, 0:WINDOW, :] = buf[:, tm:tm + WINDOW, :]
    vt_buf[:, :, 0:WINDOW] = vt_buf[:, :, tm:tm + WINDOW]


ATTN_COLS = dict(q=0, k=ATTN_WIDTH, v=ATTN_WIDTH + KV_WIDTH, gate=ATTN_WIDTH + 2 * KV_WIDTH)


def _transposed_attn_w_in(win):
    t = lambda c0, width: jnp.swapaxes(win[:, :, c0:c0 + width], 1, 2).astype(_BF16)
    return t(ATTN_COLS["q"], ATTN_WIDTH), t(ATTN_COLS["v"], KV_WIDTH), t(ATTN_COLS["gate"], ATTN_WIDTH)


def _attn_prompt_layer(x, mods, norm_g, win, win_t, sinks, wout, final_g, windows, *, i, row0, tm):
    B, T, D = x.shape
    nt = T // tm
    j = i // 2
    final_norm = final_g is not None
    wq_t, wv_t, wg_t = win_t
    kv_cols = lambda c0: pl.BlockSpec((None, D, KV_WIDTH), lambda b, t: (j, 0, c0 // KV_WIDTH))
    in_specs = [
        pl.BlockSpec(memory_space=pltpu.SMEM),
        pl.BlockSpec((1, tm, D), lambda b, t: (b, t, 0)),
        _mod_spec(i, row0),
        _stacked(i, 1, D),
        _stacked(j, ATTN_WIDTH, D), kv_cols(ATTN_COLS["k"]), _stacked(j, KV_WIDTH, D), _stacked(j, ATTN_WIDTH, D),
        kv_cols(ATTN_COLS["v"]),
        _stacked(j, ATTN_WIDTH, D),
    ]
    args = [sinks, x, mods, norm_g, wq_t, win, wv_t, wg_t, win, wout]
    if final_norm:
        in_specs.append(pl.BlockSpec((1, D), lambda b, t: (0, 0)))
        args.append(final_g)
    window_spec = pl.BlockSpec((None, 1, WINDOW, KV_WIDTH), lambda b, t: (j, b, 0, 0))
    window_shape = jax.ShapeDtypeStruct((win.shape[0], B, WINDOW, KV_WIDTH), _F32)
    carry_specs, carry_args, aliases = _carry(windows, first_output=1)
    return pl.pallas_call(
        functools.partial(_attn_prompt_kernel, tm=tm, nt=nt, layer=j, final_norm=final_norm,
                          n_carried=len(carry_args)),
        grid=(B, nt),
        in_specs=in_specs + carry_specs,
        out_specs=[pl.BlockSpec((1, tm, D), lambda b, t: (b, t, 0)), window_spec, window_spec],
        out_shape=[jax.ShapeDtypeStruct((B, T, D), _F32), window_shape, window_shape],
        input_output_aliases=aliases(len(args)),
        scratch_shapes=[
            pltpu.VMEM((ATTN_WIDTH, tm), _BF16),
            pltpu.VMEM((N_KV_HEADS, WINDOW + tm, LANES), _BF16),
            pltpu.VMEM((N_KV_HEADS, WINDOW + tm, LANES), _BF16),
            pltpu.VMEM((N_KV_HEADS, VT_ROWS, WINDOW + tm), _BF16),
            pltpu.VMEM((ATTN_WIDTH, tm), _F32),
            pltpu.VMEM((ATTN_WIDTH, tm), _BF16),
            pltpu.VMEM((N_HEADS, BAND_KEYS, QUERY_GROUP), _F32),
            pltpu.VMEM((tm, D), _BF16),
        ],
        compiler_params=pltpu.CompilerParams(
            dimension_semantics=("arbitrary", "arbitrary"), vmem_limit_bytes=VMEM_LIMIT_BYTES),
        name="attn_layer_prompt",
    )(*args, *carry_args)


def kernel(x_prompt, x_sample, c_prompt, c_sample, cache_pool, cache_k, cache_v, norm_g, ada_w, ada_b,
           pool_w_in, pool_w_grp, pool_scale, pool_w_out, attn_w_in, attn_sinks, attn_w_out, final_g):
    n_prompt, n_sample = x_prompt.shape[0], x_sample.shape[0]
    mods = _modulation(jnp.concatenate([c_prompt, c_sample], axis=0), ada_w, ada_b)
    mods = mods.reshape(DEPTH, n_prompt + n_sample, 3, D_MODEL)

    norm_g = norm_g.reshape(DEPTH, 1, D_MODEL)
    final_g = final_g.reshape(1, D_MODEL)
    pool_w_in, pool_w_grp, pool_w_out = (w.astype(_BF16) for w in (pool_w_in, pool_w_grp, pool_w_out))
    pool_scale = pool_scale.reshape(-1, 1, POOL_WIDTH)
    attn_w_in_t = _transposed_attn_w_in(attn_w_in)
    attn_w_in, attn_w_out = attn_w_in.astype(_BF16), attn_w_out.astype(_BF16)
    hist = jnp.pad(cache_pool, ((0, 0), (0, 0), (HIST_ROWS - POOL_HIST, 0), (0, 0)))
    cache_k = cache_k.reshape(cache_k.shape[:3] + (KV_WIDTH,))
    cache_v = cache_v.reshape(cache_v.shape[:3] + (KV_WIDTH,))

    def trunk(x, row0, sample):
        n_batch, rows = x.shape[:2]
        states = jnp.zeros((pool_w_in.shape[0], n_batch, HIST_ROWS, POOL_WIDTH), _F32)
        windows = [jnp.zeros((attn_w_in.shape[0], n_batch, WINDOW, KV_WIDTH), _F32) for _ in range(2)]
        for i in range(DEPTH):
            if i % 2 == 0:
                x, states = _pool_layer(x, mods, norm_g, pool_w_in, pool_w_grp, pool_scale, pool_w_out,
                                        hist if sample else None, states, i=i, row0=row0,
                                        tm=rows if sample else min(POOL_PROMPT_TILE, rows))
            else:
                fg = final_g if i == DEPTH - 1 else None
                if sample:
                    x, *windows = _attn_sample_layer(x, mods, norm_g, attn_w_in, attn_sinks, attn_w_out,
                                                     cache_k, cache_v, fg, windows, i=i, row0=row0)
                else:
                    x, *windows = _attn_prompt_layer(x, mods, norm_g, attn_w_in, attn_w_in_t, attn_sinks, attn_w_out, fg,
                                                     windows, i=i, row0=row0, tm=min(ATTN_PROMPT_TILE, rows))
        k_win, v_win = (w.reshape(w.shape[:3] + (N_KV_HEADS, HEAD_DIM)) for w in windows)
        return x, states[:, :, HIST_ROWS - POOL_HIST:], k_win, v_win

    y_prompt, pool_p, k_p, v_p = trunk(x_prompt, 0, False)
    y_sample, pool_s, k_s, v_s = trunk(x_sample, n_prompt, True)
    return (y_prompt, y_sample, pool_p, k_p, v_p, pool_s, k_s, v_s)
```

```python
import functools

import numpy as np
import jax
import jax.numpy as jnp
from jax import lax
from jax.experimental import pallas as pl
from jax.experimental.pallas import tpu as pltpu

D_MODEL = 1024
DEPTH = 4
PAST_LEN = 1024
CHUNK = 64
N_HEADS = 16
HEAD_DIM = 64
N_KV_HEADS = 4
GQA_GROUP = N_HEADS // N_KV_HEADS
ATTN_WIDTH = N_HEADS * HEAD_DIM
KV_WIDTH = N_KV_HEADS * HEAD_DIM
WINDOW = 128
WINDOW_CHUNKS = WINDOW // CHUNK
POOL_WIDTH = D_MODEL
POOL_WINDOWS = (2, 4, 8, 16)
POOL_GROUP_WIDTH = POOL_WIDTH // len(POOL_WINDOWS)
POOL_HIST = max(POOL_WINDOWS) - 1
NORM_EPS = 1e-6
MASK_VALUE = -1e30
LOG2_E = 1.4426950408889634

LANES = 128
SUBLANES = 8
VMEM_LIMIT_BYTES = 56 * 1024 * 1024

HIST_ROWS = 2 * SUBLANES
HEAD_PAIR = LANES // HEAD_DIM
BAND_KEYS = 2 * WINDOW
POOL_PROMPT_TILE = 1024
ATTN_PROMPT_TILE = 1024
QUERY_GROUP = 2 * CHUNK
PROJ_ROWS = 256
PROJ_FEATURES = 512

_F32 = jnp.float32
_BF16 = jnp.bfloat16


def _alibi_slopes():
    h = np.arange(1, N_HEADS + 1, dtype=np.float32)
    return [float(s) for s in np.exp2(np.float32(-8.0) * h / np.float32(N_HEADS))]


def _rms(x):
    return x * lax.rsqrt(jnp.mean(x * x, axis=-1, keepdims=True) + NORM_EPS)


def _modulated_norm(x, g, shift, scale):
    return (_rms(x) * g) * (1.0 + scale) + shift


def _silu(x):
    h = 0.5 * x
    return h + h * jnp.tanh(h)


def _emit_round_robin(gens):
    live = list(gens)
    while live:
        live = [g for g in live if next(g, live) is not live]


def _dot(a, b):
    return jnp.dot(a, b, preferred_element_type=_F32)


def _dot_nt(a, b):
    return lax.dot_general(a, b, (((1,), (1,)), ((), ())), preferred_element_type=_F32)


def _mod_kernel(c_ref, w_ref, b_ref, o_ref):
    c = c_ref[...]
    o_ref[0] = _dot(_silu(c).astype(_BF16), w_ref[0].astype(_BF16)) + b_ref[0]


def _modulation(c_all, ada_w, ada_b):
    rows = c_all.shape[0]
    n_col_tiles = 3
    return pl.pallas_call(
        _mod_kernel,
        grid=(DEPTH, n_col_tiles),
        in_specs=[
            pl.BlockSpec((rows, D_MODEL), lambda i, n: (0, 0)),
            pl.BlockSpec((1, D_MODEL, D_MODEL), lambda i, n: (i, 0, n)),
            pl.BlockSpec((1, 1, D_MODEL), lambda i, n: (i, 0, n)),
        ],
        out_specs=pl.BlockSpec((1, rows, D_MODEL), lambda i, n: (i, 0, n)),
        out_shape=jax.ShapeDtypeStruct((DEPTH, rows, 3 * D_MODEL), _F32),
        compiler_params=pltpu.CompilerParams(
            dimension_semantics=("arbitrary", "arbitrary"), vmem_limit_bytes=VMEM_LIMIT_BYTES),
        name="adaln_modulation",
    )(c_all, ada_w, ada_b.reshape(DEPTH, 1, 3 * D_MODEL))


def _pool_kernel(*refs, n_seg, seg, nt, sample, n_carried):
    refs = list(refs)
    x_ref, mod_ref, g_ref, win_ref, wgrp_ref, psc_ref, wout_ref = refs[:7]
    del refs[:7]
    if sample:
        hist_ref = refs.pop(0)
    del refs[:n_carried]
    xo_ref, st_ref, ubuf, sgbuf, pbuf, zbuf, hbuf = refs
    t = pl.program_id(1)
    tm = n_seg * seg

    @pl.when(t == 0)
    def _load_history():
        if sample:
            ubuf[:, 0:HIST_ROWS, :] = hist_ref[...]
        else:
            ubuf[:, 0:HIST_ROWS, :] = jnp.zeros((n_seg, HIST_ROWS, POOL_WIDTH), _F32)

    part = min(PROJ_ROWS, tm)
    piece = min(part, seg)
    rc = min(CHUNK, piece)
    group_cols = [slice(g * POOL_GROUP_WIDTH, (g + 1) * POOL_GROUP_WIDTH) for g in range(len(POOL_WINDOWS))]

    def pieces(p, n):
        return [(r // seg, r % seg, r - p * part) for r in range(p * part, (p + 1) * part, n)]


    def norm_part(p):
        for b, o, i in pieces(p, rc):
            h = _modulated_norm(x_ref[b, o:o + rc, :], g_ref[...], mod_ref[b, 0:1, :], mod_ref[b, 1:2, :])
            hbuf[p * part + i:p * part + i + rc, :] = h.astype(_BF16)
            yield

    def project_part(p):
        rows = slice(p * part, (p + 1) * part)
        hb = hbuf[rows, :]
        u = _dot(hb, win_ref[:, 0:POOL_WIDTH])
        for b, o, i in pieces(p, piece):
            ubuf[b, HIST_ROWS + o:HIST_ROWS + o + piece, :] = u[i:i + piece]
        yield
        sgbuf[rows, :] = _silu(_dot(hb, win_ref[:, POOL_WIDTH:2 * POOL_WIDTH]))
        yield

    def pool_part(p):
        for b, o, i in pieces(p, rc):
            rows = slice(p * part + i, p * part + i + rc)
            for cols, w in zip(group_cols, POOL_WINDOWS):
                ext = ubuf[b, o:o + HIST_ROWS + rc, cols]
                s = ext
                k = 1
                while k < w:
                    s = s + pltpu.roll(s, k, axis=0)
                    k *= 2
                s, u = s[HIST_ROWS:], ext[HIST_ROWS:]
                if sample or o > 0:
                    pooled = s * (1.0 / w) - u
                else:
                    pos = t * seg + lax.broadcasted_iota(jnp.int32, (rc, POOL_GROUP_WIDTH), 0)
                    pooled = s / jnp.minimum(pos + 1, w).astype(_F32) - u
                pbuf[rows, cols] = pooled.astype(_BF16)
            yield

    def mix_part(p):
        rows = slice(p * part, (p + 1) * part)
        for g, cols in enumerate(group_cols):
            z = _dot(pbuf[rows, cols], wgrp_ref[g]) * psc_ref[:, cols]
            zbuf[rows, cols] = (z * sgbuf[rows, cols]).astype(_BF16)
            yield

    def output_part(p):
        y = _dot(zbuf[p * part:(p + 1) * part, :], wout_ref[...])
        for b, o, i in pieces(p, piece):
            xo_ref[b, o:o + piece, :] = x_ref[b, o:o + piece, :] + mod_ref[b, 2:3, :] * y[i:i + piece]
        yield

    stages = (norm_part, project_part, pool_part, mix_part, output_part)
    n_parts = tm // part
    for r in range(n_parts + len(stages) - 1):
        _emit_round_robin([stage(r - j) for j, stage in enumerate(stages) if 0 <= r - j < n_parts])

    @pl.when(t == nt - 1)
    def _emit_state():
        st_ref[...] = ubuf[:, seg:seg + HIST_ROWS, :]

    ubuf[:, 0:HIST_ROWS, :] = ubuf[:, seg:seg + HIST_ROWS, :]


def _stacked(index, *shape):
    return pl.BlockSpec((None,) + shape, lambda b, t: (index,) + (0,) * len(shape))


def _mod_spec(i, row0):
    return pl.BlockSpec((None, 1, 3, D_MODEL), lambda b, t: (i, row0 + b, 0, 0))


def _carry(carried, first_output):
    carried = list(carried)
    specs = [pl.BlockSpec(memory_space=pl.ANY)] * len(carried)
    return specs, carried, lambda n_in: {n_in + k: first_output + k for k in range(len(carried))}


def _pool_layer(x, mods, norm_g, win, wgrp, psc, wout, hist, states, *, i, row0, tm):
    B, T, D = x.shape
    j = i // 2
    n_layers = win.shape[0]
    sample = hist is not None
    n_seg, seg = (B, T) if sample else (1, tm)
    grid = (B // n_seg, T // seg)
    assert row0 % n_seg == 0
    in_specs = [
        pl.BlockSpec((n_seg, seg, D), lambda b, t: (b, t, 0)),
        pl.BlockSpec((None, n_seg, 3, D), lambda b, t: (i, row0 // n_seg + b, 0, 0)),
        _stacked(i, 1, D),
        _stacked(j, D, 2 * POOL_WIDTH),
        _stacked(j, len(POOL_WINDOWS), POOL_GROUP_WIDTH, POOL_GROUP_WIDTH),
        _stacked(j, 1, POOL_WIDTH),
        _stacked(j, POOL_WIDTH, D),
    ]
    args = [x, mods, norm_g, win, wgrp, psc, wout]
    if sample:
        in_specs.append(pl.BlockSpec((None, n_seg, HIST_ROWS, POOL_WIDTH), lambda b, t: (j, b, 0, 0)))
        args.append(hist)
    tm = n_seg * seg
    carry_specs, carry_args, aliases = _carry([states], first_output=1)
    return pl.pallas_call(
        functools.partial(_pool_kernel, n_seg=n_seg, seg=seg, nt=grid[1], sample=sample, n_carried=len(carry_args)),
        grid=grid,
        in_specs=in_specs + carry_specs,
        out_specs=[
            pl.BlockSpec((n_seg, seg, D), lambda b, t: (b, t, 0)),
            pl.BlockSpec((None, n_seg, HIST_ROWS, POOL_WIDTH), lambda b, t: (j, b, 0, 0)),
        ],
        out_shape=[
            jax.ShapeDtypeStruct((B, T, D), _F32),
            jax.ShapeDtypeStruct((n_layers, B, HIST_ROWS, POOL_WIDTH), _F32),
        ],
        input_output_aliases=aliases(len(args)),
        scratch_shapes=[
            pltpu.VMEM((n_seg, HIST_ROWS + seg, POOL_WIDTH), _F32),
            pltpu.VMEM((tm, POOL_WIDTH), _F32),
            pltpu.VMEM((tm, POOL_WIDTH), _BF16),
            pltpu.VMEM((tm, POOL_WIDTH), _BF16),
            pltpu.VMEM((tm, D), _BF16),
        ],
        compiler_params=pltpu.CompilerParams(
            dimension_semantics=("arbitrary", "arbitrary"), vmem_limit_bytes=VMEM_LIMIT_BYTES),
        name="pool_layer_sample" if sample else "pool_layer_prompt",
    )(*args, *carry_args)


def _store_split_heads(dst_lo, dst_hi, rows, val):
    n = val.shape[0]
    lo = lax.broadcasted_iota(jnp.int32, (n, LANES), 1) < HEAD_DIM
    zero = jnp.zeros((n, LANES), _F32)
    for j in range(KV_WIDTH // LANES):
        a = val[:, j * LANES:(j + 1) * LANES]
        ar = pltpu.roll(a, HEAD_DIM, axis=1)
        dst_lo[HEAD_PAIR * j, rows, :] = jnp.where(lo, a, zero).astype(_BF16)
        dst_hi[HEAD_PAIR * j, rows, :] = jnp.where(lo, zero, ar).astype(_BF16)
        dst_lo[HEAD_PAIR * j + 1, rows, :] = jnp.where(lo, ar, zero).astype(_BF16)
        dst_hi[HEAD_PAIR * j + 1, rows, :] = jnp.where(lo, zero, a).astype(_BF16)


def _attn_sample_kernel(*refs, n_batch, tm, layer, final_norm, n_carried):
    refs = list(refs)
    sink_ref, x_ref, mod_ref, g_ref, win_ref, wout_ref, ck_ref, cv_ref = refs[:8]
    del refs[:8]
    if final_norm:
        fg_ref = refs.pop(0)
    del refs[:n_carried]
    xo_ref, kw_ref, vw_ref, qbuf, k_lo, k_hi, v_lo, v_hi, sgbuf, zbuf, bias_ref, hbuf = refs
    batch_rows = [slice(b * tm, (b + 1) * tm) for b in range(n_batch)]

    for b, rows in enumerate(batch_rows):
        h = _modulated_norm(x_ref[b], g_ref[...], mod_ref[b, 0:1, :], mod_ref[b, 1:2, :])
        hbuf[rows, :] = h.astype(_BF16)
    hb = hbuf[...]
    q0, k0, v0, g0 = 0, ATTN_WIDTH, ATTN_WIDTH + KV_WIDTH, ATTN_WIDTH + 2 * KV_WIDTH
    qbuf[...] = (_dot(hb, win_ref[:, q0:k0]) * (HEAD_DIM ** -0.5 * LOG2_E)).astype(_BF16)
    k = _dot(hb, win_ref[:, k0:v0])
    v = _dot(hb, win_ref[:, v0:g0])
    sgbuf[...] = _silu(_dot(hb, win_ref[:, g0:g0 + ATTN_WIDTH]))
    for b, rows in enumerate(batch_rows):
        band0 = b * BAND_KEYS
        _store_split_heads(k_lo, k_hi, slice(band0, band0 + WINDOW), ck_ref[b])
        _store_split_heads(v_lo, v_hi, slice(band0, band0 + WINDOW), cv_ref[b])
        _store_split_heads(k_lo, k_hi, slice(band0 + WINDOW, band0 + WINDOW + tm), k[rows])
        _store_split_heads(v_lo, v_hi, slice(band0 + WINDOW, band0 + WINDOW + tm), v[rows])
        for buf in (k_lo, k_hi, v_lo, v_hi):
            buf[:, band0 + WINDOW + tm:band0 + BAND_KEYS, :] = jnp.zeros(
                (N_KV_HEADS, BAND_KEYS - WINDOW - tm, LANES), _BF16)
        kw_ref[b, 0:WINDOW - tm, :] = ck_ref[b, tm:WINDOW, :]
        vw_ref[b, 0:WINDOW - tm, :] = cv_ref[b, tm:WINDOW, :]
        kw_ref[b, WINDOW - tm:WINDOW, :] = k[rows]
        vw_ref[b, WINDOW - tm:WINDOW, :] = v[rows]

    qi = lax.broadcasted_iota(jnp.int32, (tm, BAND_KEYS), 0)
    sj = lax.broadcasted_iota(jnp.int32, (tm, BAND_KEYS), 1)
    dist = jnp.abs(WINDOW + qi - sj).astype(_F32)
    for h, slope in enumerate(_alibi_slopes()):
        bias_ref[h] = dist * (-slope * LOG2_E)

    lane = lax.broadcasted_iota(jnp.int32, (tm, LANES), 1)
    lo_lanes = lax.broadcasted_iota(jnp.int32, (HEAD_PAIR * tm, LANES), 1) < HEAD_DIM
    key_lo_lanes = lax.broadcasted_iota(jnp.int32, (BAND_KEYS, LANES), 1) < HEAD_DIM
    ones_lo = jnp.where(key_lo_lanes, 1.0, 0.0).astype(_BF16)
    ones_hi = jnp.where(key_lo_lanes, 0.0, 1.0).astype(_BF16)

    qc = WINDOW // CHUNK
    masks = []
    for col in range(BAND_KEYS // LANES):
        conds = []
        for half in range(LANES // CHUNK):
            kc = (col * LANES) // CHUNK + half
            if not (qc - WINDOW_CHUNKS <= kc <= qc):
                conds.append(lane >= CHUNK if half == 0 else lane < CHUNK)
        if col * LANES + LANES > WINDOW + tm:
            conds.append(lane < WINDOW + tm - col * LANES)
        masks.append(functools.reduce(jnp.logical_and, conds) if conds else None)

    n_pairs = GQA_GROUP // HEAD_PAIR

    def score_dots(b, kh, st):
        c0 = kh * GQA_GROUP * HEAD_DIM
        qs = jnp.concatenate(
            [qbuf[batch_rows[b], c0 + p * LANES:c0 + (p + 1) * LANES] for p in range(n_pairs)], axis=0)
        st["s"] = [_dot_nt(qs, kbuf[kh, b * BAND_KEYS:(b + 1) * BAND_KEYS, :]) for kbuf in (k_lo, k_hi)]
        yield

    def probabilities(b, kh, st):
        st["probs"], st["sink_p"] = [], []
        for parity in range(HEAD_PAIR):
            p_blocks, sink_blocks = [], []
            for p in range(n_pairs):
                h = kh * GQA_GROUP + p * HEAD_PAIR + parity
                sink2 = sink_ref[layer, h] * LOG2_E
                cols = []
                for col, allowed in enumerate(masks):
                    lanes = slice(col * LANES, (col + 1) * LANES)
                    s = st["s"][parity][p * tm:(p + 1) * tm, lanes] + bias_ref[h, :, lanes]
                    cols.append(s if allowed is None else jnp.where(allowed, s, MASK_VALUE))
                m = jnp.maximum(jnp.max(functools.reduce(jnp.maximum, cols), axis=-1, keepdims=True), sink2)
                sink_blocks.append(jnp.exp2(sink2 - m))
                p_blocks.append(jnp.concatenate([jnp.exp2(c - m).astype(_BF16) for c in cols], axis=1))
                yield
            st["probs"].append(jnp.concatenate(p_blocks, axis=0))
            st["sink_p"].append(jnp.concatenate(sink_blocks, axis=0))

    def value_dots(b, kh, st):
        band = slice(b * BAND_KEYS, (b + 1) * BAND_KEYS)
        st["acc"] = (_dot(st["probs"][0], jnp.concatenate([v_lo[kh, band, :], ones_lo], axis=1))
                     + _dot(st["probs"][1], jnp.concatenate([v_hi[kh, band, :], ones_hi], axis=1)))
        yield

    def finish(b, kh, st):
        rows, c0 = batch_rows[b], kh * GQA_GROUP * HEAD_DIM
        acc = st["acc"]
        o = acc[:, :LANES] / (acc[:, LANES:] + jnp.where(lo_lanes, st["sink_p"][0], st["sink_p"][1]))
        for p in range(n_pairs):
            cols = slice(c0 + p * LANES, c0 + (p + 1) * LANES)
            zbuf[rows, cols] = (o[p * tm:(p + 1) * tm] * sgbuf[rows, cols]).astype(_BF16)
        st.clear()
        yield

    stages = (score_dots, probabilities, value_dots, finish)
    steps = [(b, kh) for b in range(n_batch) for kh in range(N_KV_HEADS)]
    state = [{} for _ in steps]
    for r in range(len(steps) + len(stages) - 1):
        _emit_round_robin(
            [stage(*steps[r - j], state[r - j]) for j, stage in enumerate(stages) if 0 <= r - j < len(steps)])

    y = _dot(zbuf[...], wout_ref[...])
    for b, rows in enumerate(batch_rows):
        xn = x_ref[b] + mod_ref[b, 2:3, :] * y[rows]
        if final_norm:
            xn = _rms(xn) * fg_ref[...]
        xo_ref[b] = xn


def _attn_sample_layer(x, mods, norm_g, win, sinks, wout, cache_k, cache_v, final_g, windows, *, i, row0):
    B, tm, D = x.shape
    assert tm < CHUNK and PAST_LEN % CHUNK == 0 and row0 % B == 0
    j = i // 2
    n_layers = win.shape[0]
    final_norm = final_g is not None
    whole = lambda *shape: pl.BlockSpec(shape, lambda b, t: (0,) * len(shape))
    in_specs = [
        pl.BlockSpec(memory_space=pltpu.SMEM),
        whole(B, tm, D),
        pl.BlockSpec((None, B, 3, D), lambda b, t: (i, row0 // B, 0, 0)),
        _stacked(i, 1, D),
        _stacked(j, D, 2 * ATTN_WIDTH + 2 * KV_WIDTH),
        _stacked(j, ATTN_WIDTH, D),
        _stacked(j, B, WINDOW, KV_WIDTH),
        _stacked(j, B, WINDOW, KV_WIDTH),
    ]
    args = [sinks, x, mods, norm_g, win, wout, cache_k, cache_v]
    if final_norm:
        in_specs.append(whole(1, D))
        args.append(final_g)
    window_shape = jax.ShapeDtypeStruct((n_layers, B, WINDOW, KV_WIDTH), _F32)
    carry_specs, carry_args, aliases = _carry(windows, first_output=1)
    return pl.pallas_call(
        functools.partial(_attn_sample_kernel, n_batch=B, tm=tm, layer=j, final_norm=final_norm,
                          n_carried=len(carry_args)),
        grid=(1, 1),
        in_specs=in_specs + carry_specs,
        out_specs=[whole(B, tm, D), _stacked(j, B, WINDOW, KV_WIDTH), _stacked(j, B, WINDOW, KV_WIDTH)],
        out_shape=[jax.ShapeDtypeStruct((B, tm, D), _F32), window_shape, window_shape],
        input_output_aliases=aliases(len(args)),
        scratch_shapes=[
            pltpu.VMEM((B * tm, ATTN_WIDTH), _BF16),
            pltpu.VMEM((N_KV_HEADS, B * BAND_KEYS, LANES), _BF16),
            pltpu.VMEM((N_KV_HEADS, B * BAND_KEYS, LANES), _BF16),
            pltpu.VMEM((N_KV_HEADS, B * BAND_KEYS, LANES), _BF16),
            pltpu.VMEM((N_KV_HEADS, B * BAND_KEYS, LANES), _BF16),
            pltpu.VMEM((B * tm, ATTN_WIDTH), _F32),
            pltpu.VMEM((B * tm, ATTN_WIDTH), _BF16),
            pltpu.VMEM((N_HEADS, tm, BAND_KEYS), _F32),
            pltpu.VMEM((B * tm, D), _BF16),
        ],
        compiler_params=pltpu.CompilerParams(
            dimension_semantics=("arbitrary", "arbitrary"), vmem_limit_bytes=VMEM_LIMIT_BYTES),
        name="attn_layer_sample",
    )(*args, *carry_args)


SUM_ROWS = 2 * SUBLANES
VT_ROWS = HEAD_DIM + SUM_ROWS


def _dot_tn(a, b):
    return lax.dot_general(a, b, (((0,), (0,)), ((), ())), preferred_element_type=_F32)


def _attn_prompt_kernel(*refs, tm, nt, layer, final_norm, n_carried):
    refs = list(refs)
    (sink_ref, x_ref, mod_ref, g_ref, wqt_ref, wk_ref, wvt_ref, wgt_ref, wv_ref, wout_ref) = refs[:10]
    del refs[:10]
    if final_norm:
        fg_ref = refs.pop(0)
    del refs[:n_carried]
    xo_ref, kw_ref, vw_ref, qt, k_lo, k_hi, vt_buf, sgt, zt, bias_t, hbuf = refs
    t = pl.program_id(1)
    n_pairs = GQA_GROUP // HEAD_PAIR
    slopes = _alibi_slopes()

    @pl.when(t == 0)
    def _reset_window():
        for buf in (k_lo, k_hi):
            buf[:, 0:WINDOW, :] = jnp.zeros((N_KV_HEADS, WINDOW, LANES), _BF16)
        row = lax.broadcasted_iota(jnp.int32, (VT_ROWS, WINDOW + tm), 0)
        ones_rows = (row >= HEAD_DIM) & (row < HEAD_DIM + SUBLANES)
        for kh in range(N_KV_HEADS):
            vt_buf[kh] = jnp.where(ones_rows, 1.0, 0.0).astype(_BF16)
        si = lax.broadcasted_iota(jnp.int32, (BAND_KEYS, QUERY_GROUP), 0)
        qj = lax.broadcasted_iota(jnp.int32, (BAND_KEYS, QUERY_GROUP), 1)
        dist = jnp.abs(WINDOW + qj - si).astype(_F32)
        for h in range(N_HEADS):
            bias_t[h] = dist * (-slopes[h] * LOG2_E)

    shift, scale, gmod = mod_ref[0, 0:1, :], mod_ref[0, 1:2, :], mod_ref[0, 2:3, :]
    n_parts = tm // PROJ_ROWS


    run_region = _emit_round_robin

    def norm_part(j):
        for r0 in range(j * PROJ_ROWS, (j + 1) * PROJ_ROWS, CHUNK):
            rows = slice(r0, r0 + CHUNK)
            hbuf[rows, :] = _modulated_norm(x_ref[0, rows, :], g_ref[...], shift, scale).astype(_BF16)
            yield

    def project_part(j):
        rows = slice(j * PROJ_ROWS, (j + 1) * PROJ_ROWS)
        new = slice(WINDOW + j * PROJ_ROWS, WINDOW + (j + 1) * PROJ_ROWS)
        hb = hbuf[rows, :]
        hb_t = hb.T
        for f0 in range(0, ATTN_WIDTH, PROJ_FEATURES):
            feats = slice(f0, f0 + PROJ_FEATURES)
            sgt[feats, rows] = _silu(_dot(wgt_ref[feats, :], hb_t))
            yield
        for f0 in range(0, ATTN_WIDTH, PROJ_FEATURES):
            feats = slice(f0, f0 + PROJ_FEATURES)
            qt[feats, rows] = (_dot(wqt_ref[feats, :], hb_t) * (HEAD_DIM ** -0.5 * LOG2_E)).astype(_BF16)
            yield
        vt = _dot(wvt_ref[...], hb_t)
        for kh in range(N_KV_HEADS):
            vt_buf[kh, 0:HEAD_DIM, new] = vt[kh * HEAD_DIM:(kh + 1) * HEAD_DIM, :].astype(_BF16)
        yield
        _store_split_heads(k_lo, k_hi, new, _dot(hb, wk_ref[...]))
        yield

    def output_part(grp):
        rows = slice(grp * QUERY_GROUP, (grp + 1) * QUERY_GROUP)
        y = _dot_tn(zt[:, rows], wout_ref[...])
        xn = x_ref[0, rows, :] + gmod * y
        if final_norm:
            xn = _rms(xn) * fg_ref[...]
        xo_ref[0, rows, :] = xn
        yield

    lane = lax.broadcasted_iota(jnp.int32, (CHUNK, LANES), 1)

    def key_chunk_mask(grp, kc):
        conds = []
        for i in range(QUERY_GROUP // CHUNK):
            qc = WINDOW_CHUNKS + i
            if not (qc - WINDOW_CHUNKS <= kc <= qc):
                conds.append(lane >= CHUNK if i == 0 else lane < CHUNK)
        if grp == 0 and kc < WINDOW_CHUNKS:
            conds.append(t > 0)
        return functools.reduce(jnp.logical_and, conds) if conds else None


    def score_dots(grp, kh, st):
        r0, c0 = grp * QUERY_GROUP, kh * GQA_GROUP * HEAD_DIM
        q_lanes, band = slice(r0, r0 + QUERY_GROUP), slice(r0, r0 + BAND_KEYS)
        q_rhs = jnp.concatenate([qt[c0 + p * LANES:c0 + (p + 1) * LANES, q_lanes] for p in range(n_pairs)], axis=1)
        st["s_t"] = []
        for kbuf in (k_lo, k_hi):
            st["s_t"].append(_dot(kbuf[kh, band, :], q_rhs))
            yield

    def score_max(grp, kh, st):
        masks = [key_chunk_mask(grp, kc) for kc in range(BAND_KEYS // CHUNK)]
        st["blocks"], st["m"], st["sink_p"] = [], [], []
        for parity in range(HEAD_PAIR):
            for p in range(n_pairs):
                h = kh * GQA_GROUP + p * HEAD_PAIR + parity
                sink2 = sink_ref[layer, h] * LOG2_E
                blocks = []
                for kc, allowed in enumerate(masks):
                    rows = slice(kc * CHUNK, (kc + 1) * CHUNK)
                    s = st["s_t"][parity][rows, p * QUERY_GROUP:(p + 1) * QUERY_GROUP] + bias_t[h, rows, :]
                    blocks.append(s if allowed is None else jnp.where(allowed, s, MASK_VALUE))
                m = jnp.max(functools.reduce(jnp.maximum, blocks), axis=0, keepdims=True)
                m = jnp.maximum(m, sink2)
                st["blocks"].append(blocks)
                st["m"].append(m)
                st["sink_p"].append(jnp.exp2(sink2 - m))
                yield

    def probabilities(grp, kh, st):
        st["p_t"] = []
        for blocks, m in zip(st["blocks"], st["m"]):
            st["p_t"].append(jnp.concatenate([jnp.exp2(b - m).astype(_BF16) for b in blocks], axis=0))
            yield

    def value_dots(grp, kh, st):
        band = slice(grp * QUERY_GROUP, grp * QUERY_GROUP + BAND_KEYS)
        st["acc"] = []
        for parity in range(HEAD_PAIR):
            p_t = jnp.concatenate(st["p_t"][parity * n_pairs:(parity + 1) * n_pairs], axis=1)
            st["acc"].append(_dot(vt_buf[kh, :, band], p_t))
            yield

    def finish(grp, kh, st):
        r0, c0 = grp * QUERY_GROUP, kh * GQA_GROUP * HEAD_DIM
        q_lanes = slice(r0, r0 + QUERY_GROUP)
        for p in range(n_pairs):
            lanes = slice(p * QUERY_GROUP, (p + 1) * QUERY_GROUP)
            heads = []
            for parity in range(HEAD_PAIR):
                acc = st["acc"][parity]
                row_sum = acc[HEAD_DIM:HEAD_DIM + 1, lanes] + st["sink_p"][parity * n_pairs + p]
                heads.append(acc[0:HEAD_DIM, lanes] * (1.0 / row_sum))
            out = jnp.concatenate(heads, axis=0)
            rows = slice(c0 + p * LANES, c0 + (p + 1) * LANES)
            zt[rows, q_lanes] = (out * sgt[rows, q_lanes]).astype(_BF16)
            yield
        st.clear()

    stages = (score_dots, score_max, probabilities, value_dots, finish)
    steps = [(grp, kh) for grp in range(tm // QUERY_GROUP) for kh in range(N_KV_HEADS)]
    state = [{} for _ in steps]
    n_regions = len(steps) + len(stages) - 1
    fill = len(stages) - 1
    per_part = len(steps) // n_parts

    def advance(gen, n):
        for _ in range(n):
            if next(gen, gen) is gen:
                return
            yield

    def spread(gen, first, count, per_region):
        for r in range(first, first + count):
            extras[r].append(advance(gen, per_region))

    extras = [[] for _ in range(n_regions + 1)]
    for j in range(n_parts):
        r0 = j * per_part
        if j + 1 < n_parts:
            n_dots = 2 * (ATTN_WIDTH // PROJ_FEATURES) + 2
            spread(project_part(j + 1), r0, per_part - 2, pl.cdiv(n_dots, per_part - 2))
        if j + 2 < n_parts:
            spread(norm_part(j + 2), r0 + per_part - 2, 2, PROJ_ROWS // CHUNK // 2)
    for grp in range(tm // QUERY_GROUP):
        extras[min((grp + 1) * N_KV_HEADS + fill, n_regions)].append(output_part(grp))

    run_region([norm_part(0)])
    run_region([project_part(0)] + ([norm_part(1)] if n_parts > 1 else []))
    for r in range(n_regions):
        run_region([stage(*steps[r - j], state[r - j]) for j, stage in enumerate(stages) if 0 <= r - j < len(steps)]
                   + extras[r])
    run_region(extras[n_regions])

    @pl.when(t == nt - 1)
    def _emit_window():
        last = hbuf[tm - WINDOW:tm, :]
        kw_ref[0] = _dot(last, wk_ref[...])
        vw_ref[0] = _dot(last, wv_ref[...])

    for buf in (k_lo, k_hi):
        buf[:, 0:WINDOW, :] = buf[:, tm:tm + WINDOW, :]
    vt_buf[:, :, 0:WINDOW] = vt_buf[:, :, tm:tm + WINDOW]


ATTN_COLS = dict(q=0, k=ATTN_WIDTH, v=ATTN_WIDTH + KV_WIDTH, gate=ATTN_WIDTH + 2 * KV_WIDTH)


def _transposed_attn_w_in(win):
    t = lambda c0, width: jnp.swapaxes(win[:, :, c0:c0 + width], 1, 2).astype(_BF16)
    return t(ATTN_COLS["q"], ATTN_WIDTH), t(ATTN_COLS["v"], KV_WIDTH), t(ATTN_COLS["gate"], ATTN_WIDTH)


def _attn_prompt_layer(x, mods, norm_g, win, win_t, sinks, wout, final_g, windows, *, i, row0, tm):
    B, T, D = x.shape
    nt = T // tm
    j = i // 2
    final_norm = final_g is not None
    wq_t, wv_t, wg_t = win_t
    kv_cols = lambda c0: pl.BlockSpec((None, D, KV_WIDTH), lambda b, t: (j, 0, c0 // KV_WIDTH))
    in_specs = [
        pl.BlockSpec(memory_space=pltpu.SMEM),
        pl.BlockSpec((1, tm, D), lambda b, t: (b, t, 0)),
        _mod_spec(i, row0),
        _stacked(i, 1, D),
        _stacked(j, ATTN_WIDTH, D), kv_cols(ATTN_COLS["k"]), _stacked(j, KV_WIDTH, D), _stacked(j, ATTN_WIDTH, D),
        kv_cols(ATTN_COLS["v"]),
        _stacked(j, ATTN_WIDTH, D),
    ]
    args = [sinks, x, mods, norm_g, wq_t, win, wv_t, wg_t, win, wout]
    if final_norm:
        in_specs.append(pl.BlockSpec((1, D), lambda b, t: (0, 0)))
        args.append(final_g)
    window_spec = pl.BlockSpec((None, 1, WINDOW, KV_WIDTH), lambda b, t: (j, b, 0, 0))
    window_shape = jax.ShapeDtypeStruct((win.shape[0], B, WINDOW, KV_WIDTH), _F32)
    carry_specs, carry_args, aliases = _carry(windows, first_output=1)
    return pl.pallas_call(
        functools.partial(_attn_prompt_kernel, tm=tm, nt=nt, layer=j, final_norm=final_norm,
                          n_carried=len(carry_args)),
        grid=(B, nt),
        in_specs=in_specs + carry_specs,
        out_specs=[pl.BlockSpec((1, tm, D), lambda b, t: (b, t, 0)), window_spec, window_spec],
        out_shape=[jax.ShapeDtypeStruct((B, T, D), _F32), window_shape, window_shape],
        input_output_aliases=aliases(len(args)),
        scratch_shapes=[
            pltpu.VMEM((ATTN_WIDTH, tm), _BF16),
            pltpu.VMEM((N_KV_HEADS, WINDOW + tm, LANES), _BF16),
            pltpu.VMEM((N_KV_HEADS, WINDOW + tm, LANES), _BF16),
            pltpu.VMEM((N_KV_HEADS, VT_ROWS, WINDOW + tm), _BF16),
            pltpu.VMEM((ATTN_WIDTH, tm), _F32),
            pltpu.VMEM((ATTN_WIDTH, tm), _BF16),
            pltpu.VMEM((N_HEADS, BAND_KEYS, QUERY_GROUP), _F32),
            pltpu.VMEM((tm, D), _BF16),
        ],
        compiler_params=pltpu.CompilerParams(
            dimension_semantics=("arbitrary", "arbitrary"), vmem_limit_bytes=VMEM_LIMIT_BYTES),
        name="attn_layer_prompt",
    )(*args, *carry_args)


def kernel(x_prompt, x_sample, c_prompt, c_sample, cache_pool, cache_k, cache_v, norm_g, ada_w, ada_b,
           pool_w_in, pool_w_grp, pool_scale, pool_w_out, attn_w_in, attn_sinks, attn_w_out, final_g):
    n_prompt, n_sample = x_prompt.shape[0], x_sample.shape[0]
    mods = _modulation(jnp.concatenate([c_prompt, c_sample], axis=0), ada_w, ada_b)
    mods = mods.reshape(DEPTH, n_prompt + n_sample, 3, D_MODEL)

    norm_g = norm_g.reshape(DEPTH, 1, D_MODEL)
    final_g = final_g.reshape(1, D_MODEL)
    pool_w_in, pool_w_grp, pool_w_out = (w.astype(_BF16) for w in (pool_w_in, pool_w_grp, pool_w_out))
    pool_scale = pool_scale.reshape(-1, 1, POOL_WIDTH)
    attn_w_in_t = _transposed_attn_w_in(attn_w_in)
    attn_w_in, attn_w_out = attn_w_in.astype(_BF16), attn_w_out.astype(_BF16)
    hist = jnp.pad(cache_pool, ((0, 0), (0, 0), (HIST_ROWS - POOL_HIST, 0), (0, 0)))
    cache_k = cache_k.reshape(cache_k.shape[:3] + (KV_WIDTH,))
    cache_v = cache_v.reshape(cache_v.shape[:3] + (KV_WIDTH,))

    def trunk(x, row0, sample):
        n_batch, rows = x.shape[:2]
        states = jnp.zeros((pool_w_in.shape[0], n_batch, HIST_ROWS, POOL_WIDTH), _F32)
        windows = [jnp.zeros((attn_w_in.shape[0], n_batch, WINDOW, KV_WIDTH), _F32) for _ in range(2)]
        for i in range(DEPTH):
            if i % 2 == 0:
                x, states = _pool_layer(x, mods, norm_g, pool_w_in, pool_w_grp, pool_scale, pool_w_out,
                                        hist if sample else None, states, i=i, row0=row0,
                                        tm=rows if sample else min(POOL_PROMPT_TILE, rows))
            else:
                fg = final_g if i == DEPTH - 1 else None
                if sample:
                    x, *windows = _attn_sample_layer(x, mods, norm_g, attn_w_in, attn_sinks, attn_w_out,
                                                     cache_k, cache_v, fg, windows, i=i, row0=row0)
                else:
                    x, *windows = _attn_prompt_layer(x, mods, norm_g, attn_w_in, attn_w_in_t, attn_sinks, attn_w_out, fg,
                                                     windows, i=i, row0=row0, tm=min(ATTN_PROMPT_TILE, rows))
        k_win, v_win = (w.reshape(w.shape[:3] + (N_KV_HEADS, HEAD_DIM)) for w in windows)
        return x, states[:, :, HIST_ROWS - POOL_HIST:], k_win, v_win

    y_prompt, pool_p, k_p, v_p = trunk(x_prompt, 0, False)
    y_sample, pool_s, k_s, v_s = trunk(x_sample, n_prompt, True)
    return (y_prompt, y_sample, pool_p, k_p, v_p, pool_s, k_s, v_s)
```

```python
import functools

import numpy as np
import jax
import jax.numpy as jnp
from jax import lax
from jax.experimental import pallas as pl
from jax.experimental.pallas import tpu as pltpu

D_MODEL = 1024
DEPTH = 4
PAST_LEN = 1024
CHUNK = 64
N_HEADS = 16
HEAD_DIM = 64
N_KV_HEADS = 4
GQA_GROUP = N_HEADS // N_KV_HEADS
ATTN_WIDTH = N_HEADS * HEAD_DIM
KV_WIDTH = N_KV_HEADS * HEAD_DIM
WINDOW = 128
WINDOW_CHUNKS = WINDOW // CHUNK
POOL_WIDTH = D_MODEL
POOL_WINDOWS = (2, 4, 8, 16)
POOL_GROUP_WIDTH = POOL_WIDTH // len(POOL_WINDOWS)
POOL_HIST = max(POOL_WINDOWS) - 1
NORM_EPS = 1e-6
MASK_VALUE = -1e30
LOG2_E = 1.4426950408889634

LANES = 128
SUBLANES = 8
VMEM_LIMIT_BYTES = 56 * 1024 * 1024

HIST_ROWS = 2 * SUBLANES
HEAD_PAIR = LANES // HEAD_DIM
BAND_KEYS = 2 * WINDOW
POOL_PROMPT_TILE = 1024
ATTN_PROMPT_TILE = 1024
QUERY_GROUP = 2 * CHUNK
PROJ_ROWS = 256
PROJ_FEATURES = 512

_F32 = jnp.float32
_BF16 = jnp.bfloat16


def _alibi_slopes():
    h = np.arange(1, N_HEADS + 1, dtype=np.float32)
    return [float(s) for s in np.exp2(np.float32(-8.0) * h / np.float32(N_HEADS))]


def _rms(x):
    return x * lax.rsqrt(jnp.mean(x * x, axis=-1, keepdims=True) + NORM_EPS)


def _modulated_norm(x, g, shift, scale):
    return (_rms(x) * g) * (1.0 + scale) + shift


def _silu(x):
    h = 0.5 * x
    return h + h * jnp.tanh(h)


def _emit_round_robin(gens):
    live = list(gens)
    while live:
        live = [g for g in live if next(g, live) is not live]


def _dot(a, b):
    return jnp.dot(a, b, preferred_element_type=_F32)


def _dot_nt(a, b):
    return lax.dot_general(a, b, (((1,), (1,)), ((), ())), preferred_element_type=_F32)


def _mod_kernel(c_ref, w_ref, b_ref, o_ref):
    c = c_ref[...]
    o_ref[0] = _dot(_silu(c).astype(_BF16), w_ref[0].astype(_BF16)) + b_ref[0]


def _modulation(c_all, ada_w, ada_b):
    rows = c_all.shape[0]
    n_col_tiles = 3
    return pl.pallas_call(
        _mod_kernel,
        grid=(DEPTH, n_col_tiles),
        in_specs=[
            pl.BlockSpec((rows, D_MODEL), lambda i, n: (0, 0)),
            pl.BlockSpec((1, D_MODEL, D_MODEL), lambda i, n: (i, 0, n)),
            pl.BlockSpec((1, 1, D_MODEL), lambda i, n: (i, 0, n)),
        ],
        out_specs=pl.BlockSpec((1, rows, D_MODEL), lambda i, n: (i, 0, n)),
        out_shape=jax.ShapeDtypeStruct((DEPTH, rows, 3 * D_MODEL), _F32),
        compiler_params=pltpu.CompilerParams(
            dimension_semantics=("arbitrary", "arbitrary"), vmem_limit_bytes=VMEM_LIMIT_BYTES),
        name="adaln_modulation",
    )(c_all, ada_w, ada_b.reshape(DEPTH, 1, 3 * D_MODEL))


def _pool_kernel(*refs, n_seg, seg, nt, sample, n_carried):
    refs = list(refs)
    x_ref, mod_ref, g_ref, win_ref, wgrp_ref, psc_ref, wout_ref = refs[:7]
    del refs[:7]
    if sample:
        hist_ref = refs.pop(0)
    del refs[:n_carried]
    xo_ref, st_ref, ubuf, sgbuf, pbuf, zbuf, hbuf = refs
    t = pl.program_id(1)
    tm = n_seg * seg

    @pl.when(t == 0)
    def _load_history():
        if sample:
            ubuf[:, 0:HIST_ROWS, :] = hist_ref[...]
        else:
            ubuf[:, 0:HIST_ROWS, :] = jnp.zeros((n_seg, HIST_ROWS, POOL_WIDTH), _F32)

    part = min(PROJ_ROWS, tm)
    piece = min(part, seg)
    rc = min(CHUNK, piece)
    group_cols = [slice(g * POOL_GROUP_WIDTH, (g + 1) * POOL_GROUP_WIDTH) for g in range(len(POOL_WINDOWS))]

    def pieces(p, n):
        return [(r // seg, r % seg, r - p * part) for r in range(p * part, (p + 1) * part, n)]


    def norm_part(p):
        for b, o, i in pieces(p, rc):
            h = _modulated_norm(x_ref[b, o:o + rc, :], g_ref[...], mod_ref[b, 0:1, :], mod_ref[b, 1:2, :])
            hbuf[p * part + i:p * part + i + rc, :] = h.astype(_BF16)
            yield

    def project_part(p):
        rows = slice(p * part, (p + 1) * part)
        hb = hbuf[rows, :]
        u = _dot(hb, win_ref[:, 0:POOL_WIDTH])
        for b, o, i in pieces(p, piece):
            ubuf[b, HIST_ROWS + o:HIST_ROWS + o + piece, :] = u[i:i + piece]
        yield
        sgbuf[rows, :] = _dot(hb, win_ref[:, POOL_WIDTH:2 * POOL_WIDTH])
        yield

    def pool_part(p):
        for b, o, i in pieces(p, rc):
            rows = slice(p * part + i, p * part + i + rc)
            sgbuf[rows, :] = _silu(sgbuf[rows, :])
            for cols, w in zip(group_cols, POOL_WINDOWS):
                ext = ubuf[b, o:o + HIST_ROWS + rc, cols]
                s = ext
                k = 1
                while k < w:
                    s = s + pltpu.roll(s, k, axis=0)
                    k *= 2
                s, u = s[HIST_ROWS:], ext[HIST_ROWS:]
                if sample or o > 0:
                    pooled = s * (1.0 / w) - u
                else:
                    pos = t * seg + lax.broadcasted_iota(jnp.int32, (rc, POOL_GROUP_WIDTH), 0)
                    pooled = s / jnp.minimum(pos + 1, w).astype(_F32) - u
                pbuf[rows, cols] = pooled.astype(_BF16)
            yield

    def mix_part(p):
        rows = slice(p * part, (p + 1) * part)
        for g, cols in enumerate(group_cols):
            z = _dot(pbuf[rows, cols], wgrp_ref[g]) * psc_ref[:, cols]
            zbuf[rows, cols] = (z * sgbuf[rows, cols]).astype(_BF16)
            yield

    def output_part(p):
        y = _dot(zbuf[p * part:(p + 1) * part, :], wout_ref[...])
        for b, o, i in pieces(p, piece):
            xo_ref[b, o:o + piece, :] = x_ref[b, o:o + piece, :] + mod_ref[b, 2:3, :] * y[i:i + piece]
        yield

    stages = (norm_part, project_part, pool_part, mix_part, output_part)
    n_parts = tm // part
    for r in range(n_parts + len(stages) - 1):
        _emit_round_robin([stage(r - j) for j, stage in enumerate(stages) if 0 <= r - j < n_parts])

    @pl.when(t == nt - 1)
    def _emit_state():
        st_ref[...] = ubuf[:, seg:seg + HIST_ROWS, :]

    ubuf[:, 0:HIST_ROWS, :] = ubuf[:, seg:seg + HIST_ROWS, :]


def _stacked(index, *shape):
    return pl.BlockSpec((None,) + shape, lambda b, t: (index,) + (0,) * len(shape))


def _mod_spec(i, row0):
    return pl.BlockSpec((None, 1, 3, D_MODEL), lambda b, t: (i, row0 + b, 0, 0))


def _carry(carried, first_output):
    carried = list(carried or ())
    specs = [pl.BlockSpec(memory_space=pl.ANY)] * len(carried)
    return specs, carried, lambda n_in: {n_in + k: first_output + k for k in range(len(carried))}


def _pool_layer(x, mods, norm_g, win, wgrp, psc, wout, hist, states, *, i, row0, tm):
    B, T, D = x.shape
    j = i // 2
    n_layers = win.shape[0]
    sample = hist is not None
    n_seg, seg = (B, T) if sample else (1, tm)
    grid = (B // n_seg, T // seg)
    assert row0 % n_seg == 0
    in_specs = [
        pl.BlockSpec((n_seg, seg, D), lambda b, t: (b, t, 0)),
        pl.BlockSpec((None, n_seg, 3, D), lambda b, t: (i, row0 // n_seg + b, 0, 0)),
        _stacked(i, 1, D),
        _stacked(j, D, 2 * POOL_WIDTH),
        _stacked(j, len(POOL_WINDOWS), POOL_GROUP_WIDTH, POOL_GROUP_WIDTH),
        _stacked(j, 1, POOL_WIDTH),
        _stacked(j, POOL_WIDTH, D),
    ]
    args = [x, mods, norm_g, win, wgrp, psc, wout]
    if sample:
        in_specs.append(pl.BlockSpec((None, n_seg, HIST_ROWS, POOL_WIDTH), lambda b, t: (j, b, 0, 0)))
        args.append(hist)
    tm = n_seg * seg
    carry_specs, carry_args, aliases = _carry(None if states is None else [states], first_output=1)
    return pl.pallas_call(
        functools.partial(_pool_kernel, n_seg=n_seg, seg=seg, nt=grid[1], sample=sample, n_carried=len(carry_args)),
        grid=grid,
        in_specs=in_specs + carry_specs,
        out_specs=[
            pl.BlockSpec((n_seg, seg, D), lambda b, t: (b, t, 0)),
            pl.BlockSpec((None, n_seg, HIST_ROWS, POOL_WIDTH), lambda b, t: (j, b, 0, 0)),
        ],
        out_shape=[
            jax.ShapeDtypeStruct((B, T, D), _F32),
            jax.ShapeDtypeStruct((n_layers, B, HIST_ROWS, POOL_WIDTH), _F32),
        ],
        input_output_aliases=aliases(len(args)),
        scratch_shapes=[
            pltpu.VMEM((n_seg, HIST_ROWS + seg, POOL_WIDTH), _F32),
            pltpu.VMEM((tm, POOL_WIDTH), _F32),
            pltpu.VMEM((tm, POOL_WIDTH), _BF16),
            pltpu.VMEM((tm, POOL_WIDTH), _BF16),
            pltpu.VMEM((tm, D), _BF16),
        ],
        compiler_params=pltpu.CompilerParams(
            dimension_semantics=("arbitrary", "arbitrary"), vmem_limit_bytes=VMEM_LIMIT_BYTES),
        name="pool_layer_sample" if sample else "pool_layer_prompt",
    )(*args, *carry_args)


def _store_split_heads(dst_lo, dst_hi, rows, val):
    n = val.shape[0]
    lo = lax.broadcasted_iota(jnp.int32, (n, LANES), 1) < HEAD_DIM
    zero = jnp.zeros((n, LANES), _F32)
    for j in range(KV_WIDTH // LANES):
        a = val[:, j * LANES:(j + 1) * LANES]
        ar = pltpu.roll(a, HEAD_DIM, axis=1)
        dst_lo[HEAD_PAIR * j, rows, :] = jnp.where(lo, a, zero).astype(_BF16)
        dst_hi[HEAD_PAIR * j, rows, :] = jnp.where(lo, zero, ar).astype(_BF16)
        dst_lo[HEAD_PAIR * j + 1, rows, :] = jnp.where(lo, ar, zero).astype(_BF16)
        dst_hi[HEAD_PAIR * j + 1, rows, :] = jnp.where(lo, zero, a).astype(_BF16)


def _attn_sample_kernel(*refs, n_batch, tm, layer, final_norm, n_carried):
    refs = list(refs)
    sink_ref, x_ref, mod_ref, g_ref, win_ref, wout_ref, ck_ref, cv_ref = refs[:8]
    del refs[:8]
    if final_norm:
        fg_ref = refs.pop(0)
    del refs[:n_carried]
    xo_ref, kw_ref, vw_ref, qbuf, k_lo, k_hi, v_lo, v_hi, sgbuf, zbuf, bias_ref, hbuf = refs
    batch_rows = [slice(b * tm, (b + 1) * tm) for b in range(n_batch)]

    for b, rows in enumerate(batch_rows):
        h = _modulated_norm(x_ref[b], g_ref[...], mod_ref[b, 0:1, :], mod_ref[b, 1:2, :])
        hbuf[rows, :] = h.astype(_BF16)
    hb = hbuf[...]
    q0, k0, v0, g0 = 0, ATTN_WIDTH, ATTN_WIDTH + KV_WIDTH, ATTN_WIDTH + 2 * KV_WIDTH
    qbuf[...] = (_dot(hb, win_ref[:, q0:k0]) * (HEAD_DIM ** -0.5 * LOG2_E)).astype(_BF16)
    k = _dot(hb, win_ref[:, k0:v0])
    v = _dot(hb, win_ref[:, v0:g0])
    sgbuf[...] = _silu(_dot(hb, win_ref[:, g0:g0 + ATTN_WIDTH]))
    for b, rows in enumerate(batch_rows):
        band0 = b * BAND_KEYS
        _store_split_heads(k_lo, k_hi, slice(band0, band0 + WINDOW), ck_ref[b])
        _store_split_heads(v_lo, v_hi, slice(band0, band0 + WINDOW), cv_ref[b])
        _store_split_heads(k_lo, k_hi, slice(band0 + WINDOW, band0 + WINDOW + tm), k[rows])
        _store_split_heads(v_lo, v_hi, slice(band0 + WINDOW, band0 + WINDOW + tm), v[rows])
        for buf in (k_lo, k_hi, v_lo, v_hi):
            buf[:, band0 + WINDOW + tm:band0 + BAND_KEYS, :] = jnp.zeros(
                (N_KV_HEADS, BAND_KEYS - WINDOW - tm, LANES), _BF16)
        kw_ref[b, 0:WINDOW - tm, :] = ck_ref[b, tm:WINDOW, :]
        vw_ref[b, 0:WINDOW - tm, :] = cv_ref[b, tm:WINDOW, :]
        kw_ref[b, WINDOW - tm:WINDOW, :] = k[rows]
        vw_ref[b, WINDOW - tm:WINDOW, :] = v[rows]

    qi = lax.broadcasted_iota(jnp.int32, (tm, BAND_KEYS), 0)
    sj = lax.broadcasted_iota(jnp.int32, (tm, BAND_KEYS), 1)
    dist = jnp.abs(WINDOW + qi - sj).astype(_F32)
    for h, slope in enumerate(_alibi_slopes()):
        bias_ref[h] = dist * (-slope * LOG2_E)

    lane = lax.broadcasted_iota(jnp.int32, (tm, LANES), 1)
    lo_lanes = lax.broadcasted_iota(jnp.int32, (HEAD_PAIR * tm, LANES), 1) < HEAD_DIM
    key_lo_lanes = lax.broadcasted_iota(jnp.int32, (BAND_KEYS, LANES), 1) < HEAD_DIM
    ones_lo = jnp.where(key_lo_lanes, 1.0, 0.0).astype(_BF16)
    ones_hi = jnp.where(key_lo_lanes, 0.0, 1.0).astype(_BF16)

    qc = WINDOW // CHUNK
    masks = []
    for col in range(BAND_KEYS // LANES):
        conds = []
        for half in range(LANES // CHUNK):
            kc = (col * LANES) // CHUNK + half
            if not (qc - WINDOW_CHUNKS <= kc <= qc):
                conds.append(lane >= CHUNK if half == 0 else lane < CHUNK)
        if col * LANES + LANES > WINDOW + tm:
            conds.append(lane < WINDOW + tm - col * LANES)
        masks.append(functools.reduce(jnp.logical_and, conds) if conds else None)

    n_pairs = GQA_GROUP // HEAD_PAIR

    def score_dots(b, kh, st):
        c0 = kh * GQA_GROUP * HEAD_DIM
        qs = jnp.concatenate(
            [qbuf[batch_rows[b], c0 + p * LANES:c0 + (p + 1) * LANES] for p in range(n_pairs)], axis=0)
        st["s"] = [_dot_nt(qs, kbuf[kh, b * BAND_KEYS:(b + 1) * BAND_KEYS, :]) for kbuf in (k_lo, k_hi)]
        yield

    def probabilities(b, kh, st):
        st["probs"], st["sink_p"] = [], []
        for parity in range(HEAD_PAIR):
            p_blocks, sink_blocks = [], []
            for p in range(n_pairs):
                h = kh * GQA_GROUP + p * HEAD_PAIR + parity
                sink2 = sink_ref[layer, h] * LOG2_E
                cols = []
                for col, allowed in enumerate(masks):
                    lanes = slice(col * LANES, (col + 1) * LANES)
                    s = st["s"][parity][p * tm:(p + 1) * tm, lanes] + bias_ref[h, :, lanes]
                    cols.append(s if allowed is None else jnp.where(allowed, s, MASK_VALUE))
                m = jnp.maximum(jnp.max(functools.reduce(jnp.maximum, cols), axis=-1, keepdims=True), sink2)
                sink_blocks.append(jnp.exp2(sink2 - m))
                p_blocks.append(jnp.concatenate([jnp.exp2(c - m).astype(_BF16) for c in cols], axis=1))
                yield
            st["probs"].append(jnp.concatenate(p_blocks, axis=0))
            st["sink_p"].append(jnp.concatenate(sink_blocks, axis=0))

    def value_dots(b, kh, st):
        band = slice(b * BAND_KEYS, (b + 1) * BAND_KEYS)
        st["acc"] = (_dot(st["probs"][0], jnp.concatenate([v_lo[kh, band, :], ones_lo], axis=1))
                     + _dot(st["probs"][1], jnp.concatenate([v_hi[kh, band, :], ones_hi], axis=1)))
        yield

    def finish(b, kh, st):
        rows, c0 = batch_rows[b], kh * GQA_GROUP * HEAD_DIM
        acc = st["acc"]
        o = acc[:, :LANES] / (acc[:, LANES:] + jnp.where(lo_lanes, st["sink_p"][0], st["sink_p"][1]))
        for p in range(n_pairs):
            cols = slice(c0 + p * LANES, c0 + (p + 1) * LANES)
            zbuf[rows, cols] = (o[p * tm:(p + 1) * tm] * sgbuf[rows, cols]).astype(_BF16)
        st.clear()
        yield

    stages = (score_dots, probabilities, value_dots, finish)
    steps = [(b, kh) for b in range(n_batch) for kh in range(N_KV_HEADS)]
    state = [{} for _ in steps]
    for r in range(len(steps) + len(stages) - 1):
        _emit_round_robin(
            [stage(*steps[r - j], state[r - j]) for j, stage in enumerate(stages) if 0 <= r - j < len(steps)])

    y = _dot(zbuf[...], wout_ref[...])
    for b, rows in enumerate(batch_rows):
        xn = x_ref[b] + mod_ref[b, 2:3, :] * y[rows]
        if final_norm:
            xn = _rms(xn) * fg_ref[...]
        xo_ref[b] = xn


def _attn_sample_layer(x, mods, norm_g, win, sinks, wout, cache_k, cache_v, final_g, windows, *, i, row0):
    B, tm, D = x.shape
    assert tm < CHUNK and PAST_LEN % CHUNK == 0 and row0 % B == 0
    j = i // 2
    n_layers = win.shape[0]
    final_norm = final_g is not None
    whole = lambda *shape: pl.BlockSpec(shape, lambda b, t: (0,) * len(shape))
    in_specs = [
        pl.BlockSpec(memory_space=pltpu.SMEM),
        whole(B, tm, D),
        pl.BlockSpec((None, B, 3, D), lambda b, t: (i, row0 // B, 0, 0)),
        _stacked(i, 1, D),
        _stacked(j, D, 2 * ATTN_WIDTH + 2 * KV_WIDTH),
        _stacked(j, ATTN_WIDTH, D),
        _stacked(j, B, WINDOW, KV_WIDTH),
        _stacked(j, B, WINDOW, KV_WIDTH),
    ]
    args = [sinks, x, mods, norm_g, win, wout, cache_k, cache_v]
    if final_norm:
        in_specs.append(whole(1, D))
        args.append(final_g)
    window_shape = jax.ShapeDtypeStruct((n_layers, B, WINDOW, KV_WIDTH), _F32)
    carry_specs, carry_args, aliases = _carry(windows, first_output=1)
    return pl.pallas_call(
        functools.partial(_attn_sample_kernel, n_batch=B, tm=tm, layer=j, final_norm=final_norm,
                          n_carried=len(carry_args)),
        grid=(1, 1),
        in_specs=in_specs + carry_specs,
        out_specs=[whole(B, tm, D), _stacked(j, B, WINDOW, KV_WIDTH), _stacked(j, B, WINDOW, KV_WIDTH)],
        out_shape=[jax.ShapeDtypeStruct((B, tm, D), _F32), window_shape, window_shape],
        input_output_aliases=aliases(len(args)),
        scratch_shapes=[
            pltpu.VMEM((B * tm, ATTN_WIDTH), _BF16),
            pltpu.VMEM((N_KV_HEADS, B * BAND_KEYS, LANES), _BF16),
            pltpu.VMEM((N_KV_HEADS, B * BAND_KEYS, LANES), _BF16),
            pltpu.VMEM((N_KV_HEADS, B * BAND_KEYS, LANES), _BF16),
            pltpu.VMEM((N_KV_HEADS, B * BAND_KEYS, LANES), _BF16),
            pltpu.VMEM((B * tm, ATTN_WIDTH), _F32),
            pltpu.VMEM((B * tm, ATTN_WIDTH), _BF16),
            pltpu.VMEM((N_HEADS, tm, BAND_KEYS), _F32),
            pltpu.VMEM((B * tm, D), _BF16),
        ],
        compiler_params=pltpu.CompilerParams(
            dimension_semantics=("arbitrary", "arbitrary"), vmem_limit_bytes=VMEM_LIMIT_BYTES),
        name="attn_layer_sample",
    )(*args, *carry_args)


SUM_ROWS = 2 * SUBLANES
VT_ROWS = HEAD_DIM + SUM_ROWS


def _dot_tn(a, b):
    return lax.dot_general(a, b, (((0,), (0,)), ((), ())), preferred_element_type=_F32)


def _attn_prompt_kernel(*refs, tm, nt, layer, final_norm, n_carried):
    refs = list(refs)
    n_col_blocks = ATTN_WIDTH // PROJ_FEATURES
    sink_ref, x_ref, mod_ref, g_ref = refs[:4]
    del refs[:4]
    wq_refs, wg_refs = refs[:n_col_blocks], refs[n_col_blocks:2 * n_col_blocks]
    del refs[:2 * n_col_blocks]
    wk_ref, wvt_ref, wv_ref, wout_ref = refs[:4]
    del refs[:4]
    if final_norm:
        fg_ref = refs.pop(0)
    del refs[:n_carried]
    xo_ref, kw_ref, vw_ref, qt, k_lo, k_hi, vt_buf, sgt, zt, bias_t, hbuf = refs
    t = pl.program_id(1)
    n_pairs = GQA_GROUP // HEAD_PAIR
    slopes = _alibi_slopes()

    @pl.when(t == 0)
    def _reset_window():
        for buf in (k_lo, k_hi):
            buf[:, 0:WINDOW, :] = jnp.zeros((N_KV_HEADS, WINDOW, LANES), _BF16)
        row = lax.broadcasted_iota(jnp.int32, (VT_ROWS, WINDOW + tm), 0)
        ones_rows = (row >= HEAD_DIM) & (row < HEAD_DIM + SUBLANES)
        for kh in range(N_KV_HEADS):
            vt_buf[kh] = jnp.where(ones_rows, 1.0, 0.0).astype(_BF16)
        si = lax.broadcasted_iota(jnp.int32, (BAND_KEYS, QUERY_GROUP), 0)
        qj = lax.broadcasted_iota(jnp.int32, (BAND_KEYS, QUERY_GROUP), 1)
        dist = jnp.abs(WINDOW + qj - si).astype(_F32)
        for h in range(N_HEADS):
            bias_t[h] = dist * (-slopes[h] * LOG2_E)

    shift, scale, gmod = mod_ref[0, 0:1, :], mod_ref[0, 1:2, :], mod_ref[0, 2:3, :]
    n_parts = tm // PROJ_ROWS


    run_region = _emit_round_robin

    def norm_part(j):
        for r0 in range(j * PROJ_ROWS, (j + 1) * PROJ_ROWS, CHUNK):
            rows = slice(r0, r0 + CHUNK)
            hbuf[rows, :] = _modulated_norm(x_ref[0, rows, :], g_ref[...], shift, scale).astype(_BF16)
            yield

    def project_part(j):
        rows = slice(j * PROJ_ROWS, (j + 1) * PROJ_ROWS)
        new = slice(WINDOW + j * PROJ_ROWS, WINDOW + (j + 1) * PROJ_ROWS)
        hb = hbuf[rows, :]
        for n, w_ref in enumerate(wg_refs):
            feats = slice(n * PROJ_FEATURES, (n + 1) * PROJ_FEATURES)
            sgt[feats, rows] = _dot(hb, w_ref[...]).T
            yield
        for n, w_ref in enumerate(wq_refs):
            feats = slice(n * PROJ_FEATURES, (n + 1) * PROJ_FEATURES)
            qt[feats, rows] = (_dot(hb, w_ref[...]).T * (HEAD_DIM ** -0.5 * LOG2_E)).astype(_BF16)
            yield
        vt = _dot_nt(wvt_ref[...], hb)
        for kh in range(N_KV_HEADS):
            vt_buf[kh, 0:HEAD_DIM, new] = vt[kh * HEAD_DIM:(kh + 1) * HEAD_DIM, :].astype(_BF16)
        yield
        _store_split_heads(k_lo, k_hi, new, _dot(hb, wk_ref[...]))
        yield

    def silu_part(j, f_lo=0, f_hi=ATTN_WIDTH):
        rows = slice(j * PROJ_ROWS, (j + 1) * PROJ_ROWS)
        for f0 in range(f_lo, f_hi, LANES):
            feats = slice(f0, f0 + LANES)
            sgt[feats, rows] = _silu(sgt[feats, rows])
            yield

    def output_part(grp):
        rows = slice(grp * QUERY_GROUP, (grp + 1) * QUERY_GROUP)
        y = _dot_tn(zt[:, rows], wout_ref[...])
        xn = x_ref[0, rows, :] + gmod * y
        if final_norm:
            xn = _rms(xn) * fg_ref[...]
        xo_ref[0, rows, :] = xn
        yield

    lane = lax.broadcasted_iota(jnp.int32, (CHUNK, LANES), 1)

    def key_chunk_mask(grp, kc):
        conds = []
        for i in range(QUERY_GROUP // CHUNK):
            qc = WINDOW_CHUNKS + i
            if not (qc - WINDOW_CHUNKS <= kc <= qc):
                conds.append(lane >= CHUNK if i == 0 else lane < CHUNK)
        if grp == 0 and kc < WINDOW_CHUNKS:
            conds.append(t > 0)
        return functools.reduce(jnp.logical_and, conds) if conds else None


    def score_dots(grp, kh, st):
        r0, c0 = grp * QUERY_GROUP, kh * GQA_GROUP * HEAD_DIM
        q_lanes, band = slice(r0, r0 + QUERY_GROUP), slice(r0, r0 + BAND_KEYS)
        q_rhs = jnp.concatenate([qt[c0 + p * LANES:c0 + (p + 1) * LANES, q_lanes] for p in range(n_pairs)], axis=1)
        st["s_t"] = []
        for kbuf in (k_lo, k_hi):
            st["s_t"].append(_dot(kbuf[kh, band, :], q_rhs))
            yield

    def score_max(grp, kh, st):
        masks = [key_chunk_mask(grp, kc) for kc in range(BAND_KEYS // CHUNK)]
        st["blocks"], st["m"], st["sink_p"] = [], [], []
        for parity in range(HEAD_PAIR):
            for p in range(n_pairs):
                h = kh * GQA_GROUP + p * HEAD_PAIR + parity
                sink2 = sink_ref[layer, h] * LOG2_E
                blocks = []
                for kc, allowed in enumerate(masks):
                    rows = slice(kc * CHUNK, (kc + 1) * CHUNK)
                    s = st["s_t"][parity][rows, p * QUERY_GROUP:(p + 1) * QUERY_GROUP] + bias_t[h, rows, :]
                    blocks.append(s if allowed is None else jnp.where(allowed, s, MASK_VALUE))
                m = jnp.max(functools.reduce(jnp.maximum, blocks), axis=0, keepdims=True)
                m = jnp.maximum(m, sink2)
                st["blocks"].append(blocks)
                st["m"].append(m)
                st["sink_p"].append(jnp.exp2(sink2 - m))
                yield

    def probabilities(grp, kh, st):
        st["p_t"] = []
        for blocks, m in zip(st["blocks"], st["m"]):
            st["p_t"].append(jnp.concatenate([jnp.exp2(b - m).astype(_BF16) for b in blocks], axis=0))
            yield

    def value_dots(grp, kh, st):
        band = slice(grp * QUERY_GROUP, grp * QUERY_GROUP + BAND_KEYS)
        st["acc"] = []
        for parity in range(HEAD_PAIR):
            p_t = jnp.concatenate(st["p_t"][parity * n_pairs:(parity + 1) * n_pairs], axis=1)
            st["acc"].append(_dot(vt_buf[kh, :, band], p_t))
            yield

    def finish(grp, kh, st):
        r0, c0 = grp * QUERY_GROUP, kh * GQA_GROUP * HEAD_DIM
        q_lanes = slice(r0, r0 + QUERY_GROUP)
        for p in range(n_pairs):
            lanes = slice(p * QUERY_GROUP, (p + 1) * QUERY_GROUP)
            heads = []
            for parity in range(HEAD_PAIR):
                acc = st["acc"][parity]
                row_sum = acc[HEAD_DIM:HEAD_DIM + 1, lanes] + st["sink_p"][parity * n_pairs + p]
                heads.append(acc[0:HEAD_DIM, lanes] * (1.0 / row_sum))
            out = jnp.concatenate(heads, axis=0)
            rows = slice(c0 + p * LANES, c0 + (p + 1) * LANES)
            zt[rows, q_lanes] = (out * sgt[rows, q_lanes]).astype(_BF16)
            yield
        st.clear()

    stages = (score_dots, score_max, probabilities, value_dots, finish)
    steps = [(grp, kh) for grp in range(tm // QUERY_GROUP) for kh in range(N_KV_HEADS)]
    state = [{} for _ in steps]
    n_regions = len(steps) + len(stages) - 1
    fill = len(stages) - 1
    per_part = len(steps) // n_parts

    def advance(gen, n):
        for _ in range(n):
            if next(gen, gen) is gen:
                return
            yield

    def spread(gen, first, count, per_region):
        for r in range(first, first + count):
            extras[r].append(advance(gen, per_region))

    extras = [[] for _ in range(n_regions + 1)]
    for j in range(n_parts):
        r0 = j * per_part
        if j > 0:
            spread(silu_part(j), r0, fill, ATTN_WIDTH // LANES // fill)
        if j + 1 < n_parts:
            n_dots = 2 * (ATTN_WIDTH // PROJ_FEATURES) + 2
            spread(project_part(j + 1), r0, per_part - 2, pl.cdiv(n_dots, per_part - 2))
        if j + 2 < n_parts:
            spread(norm_part(j + 2), r0 + per_part - 2, 2, PROJ_ROWS // CHUNK // 2)
    for grp in range(tm // QUERY_GROUP):
        extras[min((grp + 1) * N_KV_HEADS + fill, n_regions)].append(output_part(grp))

    run_region([norm_part(0)])
    first = project_part(0)
    run_region([advance(first, ATTN_WIDTH // PROJ_FEATURES)] + ([norm_part(1)] if n_parts > 1 else []))
    run_region([first, silu_part(0)])
    for r in range(n_regions):
        run_region([stage(*steps[r - j], state[r - j]) for j, stage in enumerate(stages) if 0 <= r - j < len(steps)]
                   + extras[r])
    run_region(extras[n_regions])

    @pl.when(t == nt - 1)
    def _emit_window():
        last = hbuf[tm - WINDOW:tm, :]
        kw_ref[0] = _dot(last, wk_ref[...])
        vw_ref[0] = _dot(last, wv_ref[...])

    for buf in (k_lo, k_hi):
        buf[:, 0:WINDOW, :] = buf[:, tm:tm + WINDOW, :]
    vt_buf[:, :, 0:WINDOW] = vt_buf[:, :, tm:tm + WINDOW]


def _split_attn_w_in(win):
    k0, v0, g0 = ATTN_WIDTH, ATTN_WIDTH + KV_WIDTH, ATTN_WIDTH + 2 * KV_WIDTH
    wk, wv = win[:, :, k0:v0], win[:, :, v0:g0]
    return win, wk, jnp.swapaxes(wv, 1, 2), wv


def _attn_prompt_layer(x, mods, norm_g, win_parts, sinks, wout, final_g, windows, *, i, row0, tm):
    B, T, D = x.shape
    nt = T // tm
    j = i // 2
    final_norm = final_g is not None
    win, wk, wv_t, wv = win_parts
    q0, g0 = 0, ATTN_WIDTH + 2 * KV_WIDTH
    n_col_blocks = ATTN_WIDTH // PROJ_FEATURES
    assert g0 % PROJ_FEATURES == 0
    col_block = lambda c: pl.BlockSpec((None, D, PROJ_FEATURES), lambda b, t: (j, 0, c))
    in_specs = [
        pl.BlockSpec(memory_space=pltpu.SMEM),
        pl.BlockSpec((1, tm, D), lambda b, t: (b, t, 0)),
        _mod_spec(i, row0),
        _stacked(i, 1, D),
        *[col_block(q0 // PROJ_FEATURES + n) for n in range(n_col_blocks)],
        *[col_block(g0 // PROJ_FEATURES + n) for n in range(n_col_blocks)],
        _stacked(j, D, KV_WIDTH), _stacked(j, KV_WIDTH, D), _stacked(j, D, KV_WIDTH),
        _stacked(j, ATTN_WIDTH, D),
    ]
    args = [sinks, x, mods, norm_g, *([win] * (2 * n_col_blocks)), wk, wv_t, wv, wout]
    if final_norm:
        in_specs.append(pl.BlockSpec((1, D), lambda b, t: (0, 0)))
        args.append(final_g)
    window_spec = pl.BlockSpec((None, 1, WINDOW, KV_WIDTH), lambda b, t: (j, b, 0, 0))
    window_shape = jax.ShapeDtypeStruct((win.shape[0], B, WINDOW, KV_WIDTH), _F32)
    carry_specs, carry_args, aliases = _carry(windows, first_output=1)
    return pl.pallas_call(
        functools.partial(_attn_prompt_kernel, tm=tm, nt=nt, layer=j, final_norm=final_norm,
                          n_carried=len(carry_args)),
        grid=(B, nt),
        in_specs=in_specs + carry_specs,
        out_specs=[pl.BlockSpec((1, tm, D), lambda b, t: (b, t, 0)), window_spec, window_spec],
        out_shape=[jax.ShapeDtypeStruct((B, T, D), _F32), window_shape, window_shape],
        input_output_aliases=aliases(len(args)),
        scratch_shapes=[
            pltpu.VMEM((ATTN_WIDTH, tm), _BF16),
            pltpu.VMEM((N_KV_HEADS, WINDOW + tm, LANES), _BF16),
            pltpu.VMEM((N_KV_HEADS, WINDOW + tm, LANES), _BF16),
            pltpu.VMEM((N_KV_HEADS, VT_ROWS, WINDOW + tm), _BF16),
            pltpu.VMEM((ATTN_WIDTH, tm), _F32),
            pltpu.VMEM((ATTN_WIDTH, tm), _BF16),
            pltpu.VMEM((N_HEADS, BAND_KEYS, QUERY_GROUP), _F32),
            pltpu.VMEM((tm, D), _BF16),
        ],
        compiler_params=pltpu.CompilerParams(
            dimension_semantics=("arbitrary", "arbitrary"), vmem_limit_bytes=VMEM_LIMIT_BYTES),
        name="attn_layer_prompt",
    )(*args, *carry_args)


def kernel(x_prompt, x_sample, c_prompt, c_sample, cache_pool, cache_k, cache_v, norm_g, ada_w, ada_b,
           pool_w_in, pool_w_grp, pool_scale, pool_w_out, attn_w_in, attn_sinks, attn_w_out, final_g):
    n_prompt, n_sample = x_prompt.shape[0], x_sample.shape[0]
    mods = _modulation(jnp.concatenate([c_prompt, c_sample], axis=0), ada_w, ada_b)
    mods = mods.reshape(DEPTH, n_prompt + n_sample, 3, D_MODEL)

    norm_g = norm_g.reshape(DEPTH, 1, D_MODEL)
    final_g = final_g.reshape(1, D_MODEL)
    pool_w_in, pool_w_grp, pool_w_out = (w.astype(_BF16) for w in (pool_w_in, pool_w_grp, pool_w_out))
    pool_scale = pool_scale.reshape(-1, 1, POOL_WIDTH)
    attn_w_in, attn_w_out = attn_w_in.astype(_BF16), attn_w_out.astype(_BF16)
    attn_w_in_parts = _split_attn_w_in(attn_w_in)
    hist = jnp.pad(cache_pool, ((0, 0), (0, 0), (HIST_ROWS - POOL_HIST, 0), (0, 0)))
    cache_k = cache_k.reshape(cache_k.shape[:3] + (KV_WIDTH,))
    cache_v = cache_v.reshape(cache_v.shape[:3] + (KV_WIDTH,))

    def trunk(x, row0, sample):
        rows = x.shape[1]
        states, windows = None, None
        for i in range(DEPTH):
            if i % 2 == 0:
                x, states = _pool_layer(x, mods, norm_g, pool_w_in, pool_w_grp, pool_scale, pool_w_out,
                                        hist if sample else None, states, i=i, row0=row0,
                                        tm=rows if sample else min(POOL_PROMPT_TILE, rows))
            else:
                fg = final_g if i == DEPTH - 1 else None
                if sample:
                    x, *windows = _attn_sample_layer(x, mods, norm_g, attn_w_in, attn_sinks, attn_w_out,
                                                     cache_k, cache_v, fg, windows, i=i, row0=row0)
                else:
                    x, *windows = _attn_prompt_layer(x, mods, norm_g, attn_w_in_parts, attn_sinks, attn_w_out, fg,
                                                     windows, i=i, row0=row0, tm=min(ATTN_PROMPT_TILE, rows))
        k_win, v_win = (w.reshape(w.shape[:3] + (N_KV_HEADS, HEAD_DIM)) for w in windows)
        return x, states[:, :, HIST_ROWS - POOL_HIST:], k_win, v_win

    y_prompt, pool_p, k_p, v_p = trunk(x_prompt, 0, False)
    y_sample, pool_s, k_s, v_s = trunk(x_sample, n_prompt, True)
    return (y_prompt, y_sample, pool_p, k_p, v_p, pool_s, k_s, v_s)
```

```python
import functools

import numpy as np
import jax
import jax.numpy as jnp
from jax import lax
from jax.experimental import pallas as pl
from jax.experimental.pallas import tpu as pltpu

D_MODEL = 1024
DEPTH = 4
PAST_LEN = 1024
CHUNK = 64
N_HEADS = 16
HEAD_DIM = 64
N_KV_HEADS = 4
GQA_GROUP = N_HEADS // N_KV_HEADS
ATTN_WIDTH = N_HEADS * HEAD_DIM
KV_WIDTH = N_KV_HEADS * HEAD_DIM
WINDOW = 128
WINDOW_CHUNKS = WINDOW // CHUNK
POOL_WIDTH = D_MODEL
POOL_WINDOWS = (2, 4, 8, 16)
POOL_GROUP_WIDTH = POOL_WIDTH // len(POOL_WINDOWS)
POOL_HIST = max(POOL_WINDOWS) - 1
NORM_EPS = 1e-6
MASK_VALUE = -1e30
LOG2_E = 1.4426950408889634

LANES = 128
SUBLANES = 8
VMEM_LIMIT_BYTES = 56 * 1024 * 1024

HIST_ROWS = 2 * SUBLANES
HEAD_PAIR = LANES // HEAD_DIM
BAND_KEYS = 2 * WINDOW
POOL_PROMPT_TILE = 1024
ATTN_PROMPT_TILE = 1024
QUERY_GROUP = 2 * CHUNK
PROJ_ROWS = 256
PROJ_FEATURES = 512

_F32 = jnp.float32
_BF16 = jnp.bfloat16


def _alibi_slopes():
    h = np.arange(1, N_HEADS + 1, dtype=np.float32)
    return [float(s) for s in np.exp2(np.float32(-8.0) * h / np.float32(N_HEADS))]


def _rms(x):
    return x * lax.rsqrt(jnp.mean(x * x, axis=-1, keepdims=True) + NORM_EPS)


def _modulated_norm(x, g, shift, scale):
    return (_rms(x) * g) * (1.0 + scale) + shift


def _silu(x):
    h = 0.5 * x
    return h + h * jnp.tanh(h)


def _emit_round_robin(gens):
    live = list(gens)
    while live:
        live = [g for g in live if next(g, live) is not live]


def _dot(a, b):
    return jnp.dot(a, b, preferred_element_type=_F32)


def _dot_nt(a, b):
    return lax.dot_general(a, b, (((1,), (1,)), ((), ())), preferred_element_type=_F32)


def _mod_kernel(c_ref, w_ref, b_ref, o_ref):
    c = c_ref[...]
    o_ref[0] = _dot(_silu(c).astype(_BF16), w_ref[0].astype(_BF16)) + b_ref[0]


def _modulation(c_all, ada_w, ada_b):
    rows = c_all.shape[0]
    n_col_tiles = 3
    return pl.pallas_call(
        _mod_kernel,
        grid=(DEPTH, n_col_tiles),
        in_specs=[
            pl.BlockSpec((rows, D_MODEL), lambda i, n: (0, 0)),
            pl.BlockSpec((1, D_MODEL, D_MODEL), lambda i, n: (i, 0, n)),
            pl.BlockSpec((1, 1, D_MODEL), lambda i, n: (i, 0, n)),
        ],
        out_specs=pl.BlockSpec((1, rows, D_MODEL), lambda i, n: (i, 0, n)),
        out_shape=jax.ShapeDtypeStruct((DEPTH, rows, 3 * D_MODEL), _F32),
        compiler_params=pltpu.CompilerParams(
            dimension_semantics=("arbitrary", "arbitrary"), vmem_limit_bytes=VMEM_LIMIT_BYTES),
        name="adaln_modulation",
    )(c_all, ada_w, ada_b.reshape(DEPTH, 1, 3 * D_MODEL))


def _pool_kernel(*refs, n_seg, seg, nt, sample, n_carried):
    refs = list(refs)
    x_ref, mod_ref, g_ref, win_ref, wgrp_ref, psc_ref, wout_ref = refs[:7]
    del refs[:7]
    if sample:
        hist_ref = refs.pop(0)
    del refs[:n_carried]
    xo_ref, st_ref, ubuf, sgbuf, pbuf, zbuf, hbuf = refs
    t = pl.program_id(1)
    tm = n_seg * seg

    @pl.when(t == 0)
    def _load_history():
        if sample:
            ubuf[:, 0:HIST_ROWS, :] = hist_ref[...]
        else:
            ubuf[:, 0:HIST_ROWS, :] = jnp.zeros((n_seg, HIST_ROWS, POOL_WIDTH), _F32)

    part = min(PROJ_ROWS, tm)
    piece = min(part, seg)
    rc = min(CHUNK, piece)
    group_cols = [slice(g * POOL_GROUP_WIDTH, (g + 1) * POOL_GROUP_WIDTH) for g in range(len(POOL_WINDOWS))]

    def pieces(p, n):
        return [(r // seg, r % seg, r - p * part) for r in range(p * part, (p + 1) * part, n)]


    def norm_part(p):
        for b, o, i in pieces(p, rc):
            h = _modulated_norm(x_ref[b, o:o + rc, :], g_ref[...], mod_ref[b, 0:1, :], mod_ref[b, 1:2, :])
            hbuf[p * part + i:p * part + i + rc, :] = h.astype(_BF16)
            yield

    def project_part(p):
        rows = slice(p * part, (p + 1) * part)
        hb = hbuf[rows, :]
        u = _dot(hb, win_ref[:, 0:POOL_WIDTH])
        for b, o, i in pieces(p, piece):
            ubuf[b, HIST_ROWS + o:HIST_ROWS + o + piece, :] = u[i:i + piece]
        yield
        sgbuf[rows, :] = _dot(hb, win_ref[:, POOL_WIDTH:2 * POOL_WIDTH])
        yield

    def pool_part(p):
        for b, o, i in pieces(p, rc):
            rows = slice(p * part + i, p * part + i + rc)
            sgbuf[rows, :] = _silu(sgbuf[rows, :])
            for cols, w in zip(group_cols, POOL_WINDOWS):
                ext = ubuf[b, o:o + HIST_ROWS + rc, cols]
                s = ext
                k = 1
                while k < w:
                    s = s + pltpu.roll(s, k, axis=0)
                    k *= 2
                s, u = s[HIST_ROWS:], ext[HIST_ROWS:]
                if sample or o > 0:
                    pooled = s * (1.0 / w) - u
                else:
                    pos = t * seg + lax.broadcasted_iota(jnp.int32, (rc, POOL_GROUP_WIDTH), 0)
                    pooled = s / jnp.minimum(pos + 1, w).astype(_F32) - u
                pbuf[rows, cols] = pooled.astype(_BF16)
            yield

    def mix_part(p):
        rows = slice(p * part, (p + 1) * part)
        for g, cols in enumerate(group_cols):
            z = _dot(pbuf[rows, cols], wgrp_ref[g]) * psc_ref[:, cols]
            zbuf[rows, cols] = (z * sgbuf[rows, cols]).astype(_BF16)
            yield

    def output_part(p):
        y = _dot(zbuf[p * part:(p + 1) * part, :], wout_ref[...])
        for b, o, i in pieces(p, piece):
            xo_ref[b, o:o + piece, :] = x_ref[b, o:o + piece, :] + mod_ref[b, 2:3, :] * y[i:i + piece]
        yield

    stages = (norm_part, project_part, pool_part, mix_part, output_part)
    n_parts = tm // part
    for r in range(n_parts + len(stages) - 1):
        _emit_round_robin([stage(r - j) for j, stage in enumerate(stages) if 0 <= r - j < n_parts])

    @pl.when(t == nt - 1)
    def _emit_state():
        st_ref[...] = ubuf[:, seg:seg + HIST_ROWS, :]

    ubuf[:, 0:HIST_ROWS, :] = ubuf[:, seg:seg + HIST_ROWS, :]


def _stacked(index, *shape):
    return pl.BlockSpec((None,) + shape, lambda b, t: (index,) + (0,) * len(shape))


def _mod_spec(i, row0):
    return pl.BlockSpec((None, 1, 3, D_MODEL), lambda b, t: (i, row0 + b, 0, 0))


def _carry(carried, first_output):
    carried = list(carried or ())
    specs = [pl.BlockSpec(memory_space=pl.ANY)] * len(carried)
    return specs, carried, lambda n_in: {n_in + k: first_output + k for k in range(len(carried))}


def _pool_layer(x, mods, norm_g, win, wgrp, psc, wout, hist, states, *, i, row0, tm):
    B, T, D = x.shape
    j = i // 2
    n_layers = win.shape[0]
    sample = hist is not None
    n_seg, seg = (B, T) if sample else (1, tm)
    grid = (B // n_seg, T // seg)
    assert row0 % n_seg == 0
    in_specs = [
        pl.BlockSpec((n_seg, seg, D), lambda b, t: (b, t, 0)),
        pl.BlockSpec((None, n_seg, 3, D), lambda b, t: (i, row0 // n_seg + b, 0, 0)),
        _stacked(i, 1, D),
        _stacked(j, D, 2 * POOL_WIDTH),
        _stacked(j, len(POOL_WINDOWS), POOL_GROUP_WIDTH, POOL_GROUP_WIDTH),
        _stacked(j, 1, POOL_WIDTH),
        _stacked(j, POOL_WIDTH, D),
    ]
    args = [x, mods, norm_g, win, wgrp, psc, wout]
    if sample:
        in_specs.append(pl.BlockSpec((None, n_seg, HIST_ROWS, POOL_WIDTH), lambda b, t: (j, b, 0, 0)))
        args.append(hist)
    tm = n_seg * seg
    carry_specs, carry_args, aliases = _carry(None if states is None else [states], first_output=1)
    return pl.pallas_call(
        functools.partial(_pool_kernel, n_seg=n_seg, seg=seg, nt=grid[1], sample=sample, n_carried=len(carry_args)),
        grid=grid,
        in_specs=in_specs + carry_specs,
        out_specs=[
            pl.BlockSpec((n_seg, seg, D), lambda b, t: (b, t, 0)),
            pl.BlockSpec((None, n_seg, HIST_ROWS, POOL_WIDTH), lambda b, t: (j, b, 0, 0)),
        ],
        out_shape=[
            jax.ShapeDtypeStruct((B, T, D), _F32),
            jax.ShapeDtypeStruct((n_layers, B, HIST_ROWS, POOL_WIDTH), _F32),
        ],
        input_output_aliases=aliases(len(args)),
        scratch_shapes=[
            pltpu.VMEM((n_seg, HIST_ROWS + seg, POOL_WIDTH), _F32),
            pltpu.VMEM((tm, POOL_WIDTH), _F32),
            pltpu.VMEM((tm, POOL_WIDTH), _BF16),
            pltpu.VMEM((tm, POOL_WIDTH), _BF16),
            pltpu.VMEM((tm, D), _BF16),
        ],
        compiler_params=pltpu.CompilerParams(
            dimension_semantics=("arbitrary", "arbitrary"), vmem_limit_bytes=VMEM_LIMIT_BYTES),
        name="pool_layer_sample" if sample else "pool_layer_prompt",
    )(*args, *carry_args)


def _store_split_heads(dst_lo, dst_hi, rows, val):
    n = val.shape[0]
    lo = lax.broadcasted_iota(jnp.int32, (n, LANES), 1) < HEAD_DIM
    zero = jnp.zeros((n, LANES), _F32)
    for j in range(KV_WIDTH // LANES):
        a = val[:, j * LANES:(j + 1) * LANES]
        ar = pltpu.roll(a, HEAD_DIM, axis=1)
        dst_lo[HEAD_PAIR * j, rows, :] = jnp.where(lo, a, zero).astype(_BF16)
        dst_hi[HEAD_PAIR * j, rows, :] = jnp.where(lo, zero, ar).astype(_BF16)
        dst_lo[HEAD_PAIR * j + 1, rows, :] = jnp.where(lo, ar, zero).astype(_BF16)
        dst_hi[HEAD_PAIR * j + 1, rows, :] = jnp.where(lo, zero, a).astype(_BF16)


def _attn_sample_kernel(*refs, n_batch, tm, layer, final_norm, n_carried):
    refs = list(refs)
    sink_ref, x_ref, mod_ref, g_ref, win_ref, wout_ref, ck_ref, cv_ref = refs[:8]
    del refs[:8]
    if final_norm:
        fg_ref = refs.pop(0)
    del refs[:n_carried]
    xo_ref, kw_ref, vw_ref, qbuf, k_lo, k_hi, v_lo, v_hi, sgbuf, zbuf, bias_ref, hbuf = refs
    batch_rows = [slice(b * tm, (b + 1) * tm) for b in range(n_batch)]

    for b, rows in enumerate(batch_rows):
        h = _modulated_norm(x_ref[b], g_ref[...], mod_ref[b, 0:1, :], mod_ref[b, 1:2, :])
        hbuf[rows, :] = h.astype(_BF16)
    hb = hbuf[...]
    q0, k0, v0, g0 = 0, ATTN_WIDTH, ATTN_WIDTH + KV_WIDTH, ATTN_WIDTH + 2 * KV_WIDTH
    qbuf[...] = (_dot(hb, win_ref[:, q0:k0]) * (HEAD_DIM ** -0.5 * LOG2_E)).astype(_BF16)
    k = _dot(hb, win_ref[:, k0:v0])
    v = _dot(hb, win_ref[:, v0:g0])
    sgbuf[...] = _silu(_dot(hb, win_ref[:, g0:g0 + ATTN_WIDTH]))
    for b, rows in enumerate(batch_rows):
        band0 = b * BAND_KEYS
        _store_split_heads(k_lo, k_hi, slice(band0, band0 + WINDOW), ck_ref[b])
        _store_split_heads(v_lo, v_hi, slice(band0, band0 + WINDOW), cv_ref[b])
        _store_split_heads(k_lo, k_hi, slice(band0 + WINDOW, band0 + WINDOW + tm), k[rows])
        _store_split_heads(v_lo, v_hi, slice(band0 + WINDOW, band0 + WINDOW + tm), v[rows])
        for buf in (k_lo, k_hi, v_lo, v_hi):
            buf[:, band0 + WINDOW + tm:band0 + BAND_KEYS, :] = jnp.zeros(
                (N_KV_HEADS, BAND_KEYS - WINDOW - tm, LANES), _BF16)
        kw_ref[b, 0:WINDOW - tm, :] = ck_ref[b, tm:WINDOW, :]
        vw_ref[b, 0:WINDOW - tm, :] = cv_ref[b, tm:WINDOW, :]
        kw_ref[b, WINDOW - tm:WINDOW, :] = k[rows]
        vw_ref[b, WINDOW - tm:WINDOW, :] = v[rows]

    qi = lax.broadcasted_iota(jnp.int32, (tm, BAND_KEYS), 0)
    sj = lax.broadcasted_iota(jnp.int32, (tm, BAND_KEYS), 1)
    dist = jnp.abs(WINDOW + qi - sj).astype(_F32)
    for h, slope in enumerate(_alibi_slopes()):
        bias_ref[h] = dist * (-slope * LOG2_E)

    lane = lax.broadcasted_iota(jnp.int32, (tm, LANES), 1)
    lo_lanes = lax.broadcasted_iota(jnp.int32, (HEAD_PAIR * tm, LANES), 1) < HEAD_DIM
    key_lo_lanes = lax.broadcasted_iota(jnp.int32, (BAND_KEYS, LANES), 1) < HEAD_DIM
    ones_lo = jnp.where(key_lo_lanes, 1.0, 0.0).astype(_BF16)
    ones_hi = jnp.where(key_lo_lanes, 0.0, 1.0).astype(_BF16)

    qc = WINDOW // CHUNK
    masks = []
    for col in range(BAND_KEYS // LANES):
        conds = []
        for half in range(LANES // CHUNK):
            kc = (col * LANES) // CHUNK + half
            if not (qc - WINDOW_CHUNKS <= kc <= qc):
                conds.append(lane >= CHUNK if half == 0 else lane < CHUNK)
        if col * LANES + LANES > WINDOW + tm:
            conds.append(lane < WINDOW + tm - col * LANES)
        masks.append(functools.reduce(jnp.logical_and, conds) if conds else None)

    n_pairs = GQA_GROUP // HEAD_PAIR

    def score_dots(b, kh, st):
        c0 = kh * GQA_GROUP * HEAD_DIM
        qs = jnp.concatenate(
            [qbuf[batch_rows[b], c0 + p * LANES:c0 + (p + 1) * LANES] for p in range(n_pairs)], axis=0)
        st["s"] = [_dot_nt(qs, kbuf[kh, b * BAND_KEYS:(b + 1) * BAND_KEYS, :]) for kbuf in (k_lo, k_hi)]
        yield

    def probabilities(b, kh, st):
        st["probs"], st["sink_p"] = [], []
        for parity in range(HEAD_PAIR):
            p_blocks, sink_blocks = [], []
            for p in range(n_pairs):
                h = kh * GQA_GROUP + p * HEAD_PAIR + parity
                sink2 = sink_ref[layer, h] * LOG2_E
                cols = []
                for col, allowed in enumerate(masks):
                    lanes = slice(col * LANES, (col + 1) * LANES)
                    s = st["s"][parity][p * tm:(p + 1) * tm, lanes] + bias_ref[h, :, lanes]
                    cols.append(s if allowed is None else jnp.where(allowed, s, MASK_VALUE))
                m = jnp.maximum(jnp.max(functools.reduce(jnp.maximum, cols), axis=-1, keepdims=True), sink2)
                sink_blocks.append(jnp.exp2(sink2 - m))
                p_blocks.append(jnp.concatenate([jnp.exp2(c - m).astype(_BF16) for c in cols], axis=1))
                yield
            st["probs"].append(jnp.concatenate(p_blocks, axis=0))
            st["sink_p"].append(jnp.concatenate(sink_blocks, axis=0))

    def value_dots(b, kh, st):
        band = slice(b * BAND_KEYS, (b + 1) * BAND_KEYS)
        st["acc"] = (_dot(st["probs"][0], jnp.concatenate([v_lo[kh, band, :], ones_lo], axis=1))
                     + _dot(st["probs"][1], jnp.concatenate([v_hi[kh, band, :], ones_hi], axis=1)))
        yield

    def finish(b, kh, st):
        rows, c0 = batch_rows[b], kh * GQA_GROUP * HEAD_DIM
        acc = st["acc"]
        o = acc[:, :LANES] / (acc[:, LANES:] + jnp.where(lo_lanes, st["sink_p"][0], st["sink_p"][1]))
        for p in range(n_pairs):
            cols = slice(c0 + p * LANES, c0 + (p + 1) * LANES)
            zbuf[rows, cols] = (o[p * tm:(p + 1) * tm] * sgbuf[rows, cols]).astype(_BF16)
        st.clear()
        yield

    stages = (score_dots, probabilities, value_dots, finish)
    steps = [(b, kh) for b in range(n_batch) for kh in range(N_KV_HEADS)]
    state = [{} for _ in steps]
    for r in range(len(steps) + len(stages) - 1):
        _emit_round_robin(
            [stage(*steps[r - j], state[r - j]) for j, stage in enumerate(stages) if 0 <= r - j < len(steps)])

    y = _dot(zbuf[...], wout_ref[...])
    for b, rows in enumerate(batch_rows):
        xn = x_ref[b] + mod_ref[b, 2:3, :] * y[rows]
        if final_norm:
            xn = _rms(xn) * fg_ref[...]
        xo_ref[b] = xn


def _attn_sample_layer(x, mods, norm_g, win, sinks, wout, cache_k, cache_v, final_g, windows, *, i, row0):
    B, tm, D = x.shape
    assert tm < CHUNK and PAST_LEN % CHUNK == 0 and row0 % B == 0
    j = i // 2
    n_layers = win.shape[0]
    final_norm = final_g is not None
    whole = lambda *shape: pl.BlockSpec(shape, lambda b, t: (0,) * len(shape))
    in_specs = [
        pl.BlockSpec(memory_space=pltpu.SMEM),
        whole(B, tm, D),
        pl.BlockSpec((None, B, 3, D), lambda b, t: (i, row0 // B, 0, 0)),
        _stacked(i, 1, D),
        _stacked(j, D, 2 * ATTN_WIDTH + 2 * KV_WIDTH),
        _stacked(j, ATTN_WIDTH, D),
        _stacked(j, B, WINDOW, KV_WIDTH),
        _stacked(j, B, WINDOW, KV_WIDTH),
    ]
    args = [sinks, x, mods, norm_g, win, wout, cache_k, cache_v]
    if final_norm:
        in_specs.append(whole(1, D))
        args.append(final_g)
    window_shape = jax.ShapeDtypeStruct((n_layers, B, WINDOW, KV_WIDTH), _F32)
    carry_specs, carry_args, aliases = _carry(windows, first_output=1)
    return pl.pallas_call(
        functools.partial(_attn_sample_kernel, n_batch=B, tm=tm, layer=j, final_norm=final_norm,
                          n_carried=len(carry_args)),
        grid=(1, 1),
        in_specs=in_specs + carry_specs,
        out_specs=[whole(B, tm, D), _stacked(j, B, WINDOW, KV_WIDTH), _stacked(j, B, WINDOW, KV_WIDTH)],
        out_shape=[jax.ShapeDtypeStruct((B, tm, D), _F32), window_shape, window_shape],
        input_output_aliases=aliases(len(args)),
        scratch_shapes=[
            pltpu.VMEM((B * tm, ATTN_WIDTH), _BF16),
            pltpu.VMEM((N_KV_HEADS, B * BAND_KEYS, LANES), _BF16),
            pltpu.VMEM((N_KV_HEADS, B * BAND_KEYS, LANES), _BF16),
            pltpu.VMEM((N_KV_HEADS, B * BAND_KEYS, LANES), _BF16),
            pltpu.VMEM((N_KV_HEADS, B * BAND_KEYS, LANES), _BF16),
            pltpu.VMEM((B * tm, ATTN_WIDTH), _F32),
            pltpu.VMEM((B * tm, ATTN_WIDTH), _BF16),
            pltpu.VMEM((N_HEADS, tm, BAND_KEYS), _F32),
            pltpu.VMEM((B * tm, D), _BF16),
        ],
        compiler_params=pltpu.CompilerParams(
            dimension_semantics=("arbitrary", "arbitrary"), vmem_limit_bytes=VMEM_LIMIT_BYTES),
        name="attn_layer_sample",
    )(*args, *carry_args)


SUM_ROWS = 2 * SUBLANES
VT_ROWS = HEAD_DIM + SUM_ROWS


def _dot_tn(a, b):
    return lax.dot_general(a, b, (((0,), (0,)), ((), ())), preferred_element_type=_F32)


def _attn_prompt_kernel(*refs, tm, nt, layer, final_norm, n_carried):
    refs = list(refs)
    n_col_blocks = ATTN_WIDTH // PROJ_FEATURES
    sink_ref, x_ref, mod_ref, g_ref = refs[:4]
    del refs[:4]
    wq_refs, wg_refs = refs[:n_col_blocks], refs[n_col_blocks:2 * n_col_blocks]
    del refs[:2 * n_col_blocks]
    wk_ref, wvt_ref, wv_ref, wout_ref = refs[:4]
    del refs[:4]
    if final_norm:
        fg_ref = refs.pop(0)
    del refs[:n_carried]
    xo_ref, kw_ref, vw_ref, qt, k_lo, k_hi, vt_buf, sgt, zt, bias_t, hbuf = refs
    t = pl.program_id(1)
    n_pairs = GQA_GROUP // HEAD_PAIR
    slopes = _alibi_slopes()

    @pl.when((pl.program_id(0) == 0) & (t == 0))
    def _fill_constants():
        row = lax.broadcasted_iota(jnp.int32, (VT_ROWS, WINDOW + tm), 0)
        ones_rows = (row >= HEAD_DIM) & (row < HEAD_DIM + SUBLANES)
        for kh in range(N_KV_HEADS):
            vt_buf[kh] = jnp.where(ones_rows, 1.0, 0.0).astype(_BF16)
        si = lax.broadcasted_iota(jnp.int32, (BAND_KEYS, QUERY_GROUP), 0)
        qj = lax.broadcasted_iota(jnp.int32, (BAND_KEYS, QUERY_GROUP), 1)
        dist = jnp.abs(WINDOW + qj - si).astype(_F32)
        for h in range(N_HEADS):
            bias_t[h] = dist * (-slopes[h] * LOG2_E)

    @pl.when(t == 0)
    def _reset_window():
        for buf in (k_lo, k_hi):
            buf[:, 0:WINDOW, :] = jnp.zeros((N_KV_HEADS, WINDOW, LANES), _BF16)
        vt_buf[:, 0:HEAD_DIM, 0:WINDOW] = jnp.zeros((N_KV_HEADS, HEAD_DIM, WINDOW), _BF16)

    shift, scale, gmod = mod_ref[0, 0:1, :], mod_ref[0, 1:2, :], mod_ref[0, 2:3, :]
    n_parts = tm // PROJ_ROWS


    run_region = _emit_round_robin

    def norm_part(j):
        for r0 in range(j * PROJ_ROWS, (j + 1) * PROJ_ROWS, CHUNK):
            rows = slice(r0, r0 + CHUNK)
            hbuf[rows, :] = _modulated_norm(x_ref[0, rows, :], g_ref[...], shift, scale).astype(_BF16)
            yield

    def project_part(j):
        rows = slice(j * PROJ_ROWS, (j + 1) * PROJ_ROWS)
        new = slice(WINDOW + j * PROJ_ROWS, WINDOW + (j + 1) * PROJ_ROWS)
        hb = hbuf[rows, :]
        for n, w_ref in enumerate(wg_refs):
            feats = slice(n * PROJ_FEATURES, (n + 1) * PROJ_FEATURES)
            sgt[feats, rows] = _dot(hb, w_ref[...]).T
            yield
        for n, w_ref in enumerate(wq_refs):
            feats = slice(n * PROJ_FEATURES, (n + 1) * PROJ_FEATURES)
            qt[feats, rows] = (_dot(hb, w_ref[...]).T * (HEAD_DIM ** -0.5 * LOG2_E)).astype(_BF16)
            yield
        vt = _dot_nt(wvt_ref[...], hb)
        for kh in range(N_KV_HEADS):
            vt_buf[kh, 0:HEAD_DIM, new] = vt[kh * HEAD_DIM:(kh + 1) * HEAD_DIM, :].astype(_BF16)
        yield
        _store_split_heads(k_lo, k_hi, new, _dot(hb, wk_ref[...]))
        yield

    def silu_part(j, f_lo=0, f_hi=ATTN_WIDTH):
        rows = slice(j * PROJ_ROWS, (j + 1) * PROJ_ROWS)
        for f0 in range(f_lo, f_hi, LANES):
            feats = slice(f0, f0 + LANES)
            sgt[feats, rows] = _silu(sgt[feats, rows])
            yield

    def output_part(grp):
        rows = slice(grp * QUERY_GROUP, (grp + 1) * QUERY_GROUP)
        y = _dot_tn(zt[:, rows], wout_ref[...])
        xn = x_ref[0, rows, :] + gmod * y
        if final_norm:
            xn = _rms(xn) * fg_ref[...]
        xo_ref[0, rows, :] = xn
        yield

    lane = lax.broadcasted_iota(jnp.int32, (CHUNK, LANES), 1)

    def key_chunk_mask(grp, kc):
        conds = []
        for i in range(QUERY_GROUP // CHUNK):
            qc = WINDOW_CHUNKS + i
            if not (qc - WINDOW_CHUNKS <= kc <= qc):
                conds.append(lane >= CHUNK if i == 0 else lane < CHUNK)
        if grp == 0 and kc < WINDOW_CHUNKS:
            conds.append(t > 0)
        return functools.reduce(jnp.logical_and, conds) if conds else None


    def score_dots(grp, kh, st):
        r0, c0 = grp * QUERY_GROUP, kh * GQA_GROUP * HEAD_DIM
        q_lanes, band = slice(r0, r0 + QUERY_GROUP), slice(r0, r0 + BAND_KEYS)
        q_rhs = jnp.concatenate([qt[c0 + p * LANES:c0 + (p + 1) * LANES, q_lanes] for p in range(n_pairs)], axis=1)
        st["s_t"] = []
        for kbuf in (k_lo, k_hi):
            st["s_t"].append(_dot(kbuf[kh, band, :], q_rhs))
            yield

    def score_max(grp, kh, st):
        masks = [key_chunk_mask(grp, kc) for kc in range(BAND_KEYS // CHUNK)]
        st["blocks"], st["m"], st["sink_p"] = [], [], []
        for parity in range(HEAD_PAIR):
            for p in range(n_pairs):
                h = kh * GQA_GROUP + p * HEAD_PAIR + parity
                sink2 = sink_ref[layer, h] * LOG2_E
                blocks = []
                for kc, allowed in enumerate(masks):
                    rows = slice(kc * CHUNK, (kc + 1) * CHUNK)
                    s = st["s_t"][parity][rows, p * QUERY_GROUP:(p + 1) * QUERY_GROUP] + bias_t[h, rows, :]
                    blocks.append(s if allowed is None else jnp.where(allowed, s, MASK_VALUE))
                m = jnp.max(functools.reduce(jnp.maximum, blocks), axis=0, keepdims=True)
                m = jnp.maximum(m, sink2)
                st["blocks"].append(blocks)
                st["m"].append(m)
                st["sink_p"].append(jnp.exp2(sink2 - m))
                yield

    def probabilities(grp, kh, st):
        st["p_t"] = []
        for blocks, m in zip(st["blocks"], st["m"]):
            st["p_t"].append(jnp.concatenate([jnp.exp2(b - m).astype(_BF16) for b in blocks], axis=0))
            yield

    def value_dots(grp, kh, st):
        band = slice(grp * QUERY_GROUP, grp * QUERY_GROUP + BAND_KEYS)
        st["acc"] = []
        for parity in range(HEAD_PAIR):
            p_t = jnp.concatenate(st["p_t"][parity * n_pairs:(parity + 1) * n_pairs], axis=1)
            st["acc"].append(_dot(vt_buf[kh, :, band], p_t))
            yield

    def finish(grp, kh, st):
        r0, c0 = grp * QUERY_GROUP, kh * GQA_GROUP * HEAD_DIM
        q_lanes = slice(r0, r0 + QUERY_GROUP)
        for p in range(n_pairs):
            lanes = slice(p * QUERY_GROUP, (p + 1) * QUERY_GROUP)
            heads = []
            for parity in range(HEAD_PAIR):
                acc = st["acc"][parity]
                row_sum = acc[HEAD_DIM:HEAD_DIM + 1, lanes] + st["sink_p"][parity * n_pairs + p]
                heads.append(acc[0:HEAD_DIM, lanes] * (1.0 / row_sum))
            out = jnp.concatenate(heads, axis=0)
            rows = slice(c0 + p * LANES, c0 + (p + 1) * LANES)
            zt[rows, q_lanes] = (out * sgt[rows, q_lanes]).astype(_BF16)
            yield
        st.clear()

    stages = (score_dots, score_max, probabilities, value_dots, finish)
    steps = [(grp, kh) for grp in range(tm // QUERY_GROUP) for kh in range(N_KV_HEADS)]
    state = [{} for _ in steps]
    n_regions = len(steps) + len(stages) - 1
    fill = len(stages) - 1
    per_part = len(steps) // n_parts

    def advance(gen, n):
        for _ in range(n):
            if next(gen, gen) is gen:
                return
            yield

    def spread(gen, first, count, per_region):
        for r in range(first, first + count):
            extras[r].append(advance(gen, per_region))

    extras = [[] for _ in range(n_regions + 1)]
    for j in range(n_parts):
        r0 = j * per_part
        if j > 0:
            spread(silu_part(j), r0, fill, ATTN_WIDTH // LANES // fill)
        if j + 1 < n_parts:
            n_dots = 2 * (ATTN_WIDTH // PROJ_FEATURES) + 2
            spread(project_part(j + 1), r0, per_part - 2, pl.cdiv(n_dots, per_part - 2))
        if j + 2 < n_parts:
            spread(norm_part(j + 2), r0 + per_part - 2, 2, PROJ_ROWS // CHUNK // 2)
    for grp in range(tm // QUERY_GROUP):
        extras[min((grp + 1) * N_KV_HEADS + fill, n_regions)].append(output_part(grp))

    run_region([norm_part(0)])
    first = project_part(0)
    run_region([advance(first, ATTN_WIDTH // PROJ_FEATURES)] + ([norm_part(1)] if n_parts > 1 else []))
    run_region([first, silu_part(0)])
    for r in range(n_regions):
        run_region([stage(*steps[r - j], state[r - j]) for j, stage in enumerate(stages) if 0 <= r - j < len(steps)]
                   + extras[r])
    run_region(extras[n_regions])

    @pl.when(t == nt - 1)
    def _emit_window():
        last = hbuf[tm - WINDOW:tm, :]
        kw_ref[0] = _dot(last, wk_ref[...])
        vw_ref[0] = _dot(last, wv_ref[...])

    for buf in (k_lo, k_hi):
        buf[:, 0:WINDOW, :] = buf[:, tm:tm + WINDOW, :]
    vt_buf[:, :, 0:WINDOW] = vt_buf[:, :, tm:tm + WINDOW]


def _split_attn_w_in(win):
    k0, v0, g0 = ATTN_WIDTH, ATTN_WIDTH + KV_WIDTH, ATTN_WIDTH + 2 * KV_WIDTH
    wk, wv = win[:, :, k0:v0], win[:, :, v0:g0]
    return win, wk, jnp.swapaxes(wv, 1, 2), wv


def _attn_prompt_layer(x, mods, norm_g, win_parts, sinks, wout, final_g, windows, *, i, row0, tm):
    B, T, D = x.shape
    nt = T // tm
    j = i // 2
    final_norm = final_g is not None
    win, wk, wv_t, wv = win_parts
    q0, g0 = 0, ATTN_WIDTH + 2 * KV_WIDTH
    n_col_blocks = ATTN_WIDTH // PROJ_FEATURES
    assert g0 % PROJ_FEATURES == 0
    col_block = lambda c: pl.BlockSpec((None, D, PROJ_FEATURES), lambda b, t: (j, 0, c))
    in_specs = [
        pl.BlockSpec(memory_space=pltpu.SMEM),
        pl.BlockSpec((1, tm, D), lambda b, t: (b, t, 0)),
        _mod_spec(i, row0),
        _stacked(i, 1, D),
        *[col_block(q0 // PROJ_FEATURES + n) for n in range(n_col_blocks)],
        *[col_block(g0 // PROJ_FEATURES + n) for n in range(n_col_blocks)],
        _stacked(j, D, KV_WIDTH), _stacked(j, KV_WIDTH, D), _stacked(j, D, KV_WIDTH),
        _stacked(j, ATTN_WIDTH, D),
    ]
    args = [sinks, x, mods, norm_g, *([win] * (2 * n_col_blocks)), wk, wv_t, wv, wout]
    if final_norm:
        in_specs.append(pl.BlockSpec((1, D), lambda b, t: (0, 0)))
        args.append(final_g)
    window_spec = pl.BlockSpec((None, 1, WINDOW, KV_WIDTH), lambda b, t: (j, b, 0, 0))
    window_shape = jax.ShapeDtypeStruct((win.shape[0], B, WINDOW, KV_WIDTH), _F32)
    carry_specs, carry_args, aliases = _carry(windows, first_output=1)
    return pl.pallas_call(
        functools.partial(_attn_prompt_kernel, tm=tm, nt=nt, layer=j, final_norm=final_norm,
                          n_carried=len(carry_args)),
        grid=(B, nt),
        in_specs=in_specs + carry_specs,
        out_specs=[pl.BlockSpec((1, tm, D), lambda b, t: (b, t, 0)), window_spec, window_spec],
        out_shape=[jax.ShapeDtypeStruct((B, T, D), _F32), window_shape, window_shape],
        input_output_aliases=aliases(len(args)),
        scratch_shapes=[
            pltpu.VMEM((ATTN_WIDTH, tm), _BF16),
            pltpu.VMEM((N_KV_HEADS, WINDOW + tm, LANES), _BF16),
            pltpu.VMEM((N_KV_HEADS, WINDOW + tm, LANES), _BF16),
            pltpu.VMEM((N_KV_HEADS, VT_ROWS, WINDOW + tm), _BF16),
            pltpu.VMEM((ATTN_WIDTH, tm), _F32),
            pltpu.VMEM((ATTN_WIDTH, tm), _BF16),
            pltpu.VMEM((N_HEADS, BAND_KEYS, QUERY_GROUP), _F32),
            pltpu.VMEM((tm, D), _BF16),
        ],
        compiler_params=pltpu.CompilerParams(
            dimension_semantics=("arbitrary", "arbitrary"), vmem_limit_bytes=VMEM_LIMIT_BYTES),
        name="attn_layer_prompt",
    )(*args, *carry_args)


def kernel(x_prompt, x_sample, c_prompt, c_sample, cache_pool, cache_k, cache_v, norm_g, ada_w, ada_b,
           pool_w_in, pool_w_grp, pool_scale, pool_w_out, attn_w_in, attn_sinks, attn_w_out, final_g):
    n_prompt, n_sample = x_prompt.shape[0], x_sample.shape[0]
    mods = _modulation(jnp.concatenate([c_prompt, c_sample], axis=0), ada_w, ada_b)
    mods = mods.reshape(DEPTH, n_prompt + n_sample, 3, D_MODEL)

    norm_g = norm_g.reshape(DEPTH, 1, D_MODEL)
    final_g = final_g.reshape(1, D_MODEL)
    pool_w_in, pool_w_grp, pool_w_out = (w.astype(_BF16) for w in (pool_w_in, pool_w_grp, pool_w_out))
    pool_scale = pool_scale.reshape(-1, 1, POOL_WIDTH)
    attn_w_in, attn_w_out = attn_w_in.astype(_BF16), attn_w_out.astype(_BF16)
    attn_w_in_parts = _split_attn_w_in(attn_w_in)
    hist = jnp.pad(cache_pool, ((0, 0), (0, 0), (HIST_ROWS - POOL_HIST, 0), (0, 0)))
    cache_k = cache_k.reshape(cache_k.shape[:3] + (KV_WIDTH,))
    cache_v = cache_v.reshape(cache_v.shape[:3] + (KV_WIDTH,))

    def trunk(x, row0, sample):
        rows = x.shape[1]
        states, windows = None, None
        for i in range(DEPTH):
            if i % 2 == 0:
                x, states = _pool_layer(x, mods, norm_g, pool_w_in, pool_w_grp, pool_scale, pool_w_out,
                                        hist if sample else None, states, i=i, row0=row0,
                                        tm=rows if sample else min(POOL_PROMPT_TILE, rows))
            else:
                fg = final_g if i == DEPTH - 1 else None
                if sample:
                    x, *windows = _attn_sample_layer(x, mods, norm_g, attn_w_in, attn_sinks, attn_w_out,
                                                     cache_k, cache_v, fg, windows, i=i, row0=row0)
                else:
                    x, *windows = _attn_prompt_layer(x, mods, norm_g, attn_w_in_parts, attn_sinks, attn_w_out, fg,
                                                     windows, i=i, row0=row0, tm=min(ATTN_PROMPT_TILE, rows))
        k_win, v_win = (w.reshape(w.shape[:3] + (N_KV_HEADS, HEAD_DIM)) for w in windows)
        return x, states[:, :, HIST_ROWS - POOL_HIST:], k_win, v_win

    y_prompt, pool_p, k_p, v_p = trunk(x_prompt, 0, False)
    y_sample, pool_s, k_s, v_s = trunk(x_sample, n_prompt, True)
    return (y_prompt, y_sample, pool_p, k_p, v_p, pool_s, k_s, v_s)
```

```python
import functools

import numpy as np
import jax
import jax.numpy as jnp
from jax import lax
from jax.experimental import pallas as pl
from jax.experimental.pallas import tpu as pltpu

D_MODEL = 1024
DEPTH = 4
PAST_LEN = 1024
CHUNK = 64
N_HEADS = 16
HEAD_DIM = 64
N_KV_HEADS = 4
GQA_GROUP = N_HEADS // N_KV_HEADS
ATTN_WIDTH = N_HEADS * HEAD_DIM
KV_WIDTH = N_KV_HEADS * HEAD_DIM
WINDOW = 128
WINDOW_CHUNKS = WINDOW // CHUNK
POOL_WIDTH = D_MODEL
POOL_WINDOWS = (2, 4, 8, 16)
POOL_GROUP_WIDTH = POOL_WIDTH // len(POOL_WINDOWS)
POOL_HIST = max(POOL_WINDOWS) - 1
NORM_EPS = 1e-6
MASK_VALUE = -1e30
LOG2_E = 1.4426950408889634

LANES = 128
SUBLANES = 8
VMEM_LIMIT_BYTES = 56 * 1024 * 1024

HIST_ROWS = 2 * SUBLANES
HEAD_PAIR = LANES // HEAD_DIM
BAND_KEYS = 2 * WINDOW
POOL_PROMPT_TILE = 1024
ATTN_PROMPT_TILE = 1024
QUERY_GROUP = 2 * CHUNK
PROJ_ROWS = 256
PROJ_FEATURES = 512

_F32 = jnp.float32
_BF16 = jnp.bfloat16


def _alibi_slopes():
    h = np.arange(1, N_HEADS + 1, dtype=np.float32)
    return [float(s) for s in np.exp2(np.float32(-8.0) * h / np.float32(N_HEADS))]


def _rms(x):
    return x * lax.rsqrt(jnp.mean(x * x, axis=-1, keepdims=True) + NORM_EPS)


def _modulated_norm(x, g, shift, scale):
    return (_rms(x) * g) * (1.0 + scale) + shift


def _silu(x):
    h = 0.5 * x
    return h + h * jnp.tanh(h)


def _emit_round_robin(gens):
    live = list(gens)
    while live:
        live = [g for g in live if next(g, live) is not live]


def _dot(a, b):
    return jnp.dot(a, b, preferred_element_type=_F32)


def _dot_nt(a, b):
    return lax.dot_general(a, b, (((1,), (1,)), ((), ())), preferred_element_type=_F32)


def _mod_kernel(c_ref, w_ref, b_ref, o_ref):
    c = c_ref[...]
    o_ref[0] = _dot(_silu(c).astype(_BF16), w_ref[0].astype(_BF16)) + b_ref[0]


def _modulation(c_all, ada_w, ada_b):
    rows = c_all.shape[0]
    n_col_tiles, cols = 1, 3 * D_MODEL
    return pl.pallas_call(
        _mod_kernel,
        grid=(DEPTH, n_col_tiles),
        in_specs=[
            pl.BlockSpec((rows, D_MODEL), lambda i, n: (0, 0)),
            pl.BlockSpec((1, D_MODEL, cols), lambda i, n: (i, 0, n)),
            pl.BlockSpec((1, 1, cols), lambda i, n: (i, 0, n)),
        ],
        out_specs=pl.BlockSpec((1, rows, cols), lambda i, n: (i, 0, n)),
        out_shape=jax.ShapeDtypeStruct((DEPTH, rows, 3 * D_MODEL), _F32),
        compiler_params=pltpu.CompilerParams(
            dimension_semantics=("arbitrary", "arbitrary"), vmem_limit_bytes=VMEM_LIMIT_BYTES),
        name="adaln_modulation",
    )(c_all, ada_w, ada_b.reshape(DEPTH, 1, 3 * D_MODEL))


def _pool_kernel(*refs, n_seg, seg, nt, sample, n_carried):
    refs = list(refs)
    x_ref, mod_ref, g_ref, win_ref, wgrp_ref, psc_ref, wout_ref = refs[:7]
    del refs[:7]
    if sample:
        hist_ref = refs.pop(0)
    del refs[:n_carried]
    xo_ref, st_ref, ubuf, sgbuf, pbuf, zbuf, hbuf = refs
    t = pl.program_id(1)
    tm = n_seg * seg

    @pl.when(t == 0)
    def _load_history():
        if sample:
            ubuf[:, 0:HIST_ROWS, :] = hist_ref[...]
        else:
            ubuf[:, 0:HIST_ROWS, :] = jnp.zeros((n_seg, HIST_ROWS, POOL_WIDTH), _F32)

    part = min(PROJ_ROWS, tm)
    piece = min(part, seg)
    rc = min(CHUNK, piece)
    group_cols = [slice(g * POOL_GROUP_WIDTH, (g + 1) * POOL_GROUP_WIDTH) for g in range(len(POOL_WINDOWS))]

    def pieces(p, n):
        return [(r // seg, r % seg, r - p * part) for r in range(p * part, (p + 1) * part, n)]


    def norm_part(p):
        for b, o, i in pieces(p, rc):
            h = _modulated_norm(x_ref[b, o:o + rc, :], g_ref[...], mod_ref[b, 0:1, :], mod_ref[b, 1:2, :])
            hbuf[p * part + i:p * part + i + rc, :] = h.astype(_BF16)
            yield

    def project_part(p):
        rows = slice(p * part, (p + 1) * part)
        hb = hbuf[rows, :]
        u = _dot(hb, win_ref[:, 0:POOL_WIDTH])
        for b, o, i in pieces(p, piece):
            ubuf[b, HIST_ROWS + o:HIST_ROWS + o + piece, :] = u[i:i + piece]
        yield
        sgbuf[rows, :] = _dot(hb, win_ref[:, POOL_WIDTH:2 * POOL_WIDTH])
        yield

    def pool_part(p):
        for b, o, i in pieces(p, rc):
            rows = slice(p * part + i, p * part + i + rc)
            sgbuf[rows, :] = _silu(sgbuf[rows, :])
            for cols, w in zip(group_cols, POOL_WINDOWS):
                ext = ubuf[b, o:o + HIST_ROWS + rc, cols]
                s = ext
                k = 1
                while k < w:
                    s = s + pltpu.roll(s, k, axis=0)
                    k *= 2
                s, u = s[HIST_ROWS:], ext[HIST_ROWS:]
                if sample or o > 0:
                    pooled = s * (1.0 / w) - u
                else:
                    pos = t * seg + lax.broadcasted_iota(jnp.int32, (rc, POOL_GROUP_WIDTH), 0)
                    pooled = s / jnp.minimum(pos + 1, w).astype(_F32) - u
                pbuf[rows, cols] = pooled.astype(_BF16)
            yield

    def mix_part(p):
        rows = slice(p * part, (p + 1) * part)
        for g, cols in enumerate(group_cols):
            z = _dot(pbuf[rows, cols], wgrp_ref[g]) * psc_ref[:, cols]
            zbuf[rows, cols] = (z * sgbuf[rows, cols]).astype(_BF16)
            yield

    def output_part(p):
        y = _dot(zbuf[p * part:(p + 1) * part, :], wout_ref[...])
        for b, o, i in pieces(p, piece):
            xo_ref[b, o:o + piece, :] = x_ref[b, o:o + piece, :] + mod_ref[b, 2:3, :] * y[i:i + piece]
        yield

    stages = (norm_part, project_part, pool_part, mix_part, output_part)
    n_parts = tm // part
    for r in range(n_parts + len(stages) - 1):
        _emit_round_robin([stage(r - j) for j, stage in enumerate(stages) if 0 <= r - j < n_parts])

    @pl.when(t == nt - 1)
    def _emit_state():
        st_ref[...] = ubuf[:, seg:seg + HIST_ROWS, :]

    ubuf[:, 0:HIST_ROWS, :] = ubuf[:, seg:seg + HIST_ROWS, :]


def _stacked(index, *shape):
    return pl.BlockSpec((None,) + shape, lambda b, t: (index,) + (0,) * len(shape))


def _mod_spec(i, row0):
    return pl.BlockSpec((None, 1, 3, D_MODEL), lambda b, t: (i, row0 + b, 0, 0))


def _carry(carried, first_output):
    carried = list(carried or ())
    specs = [pl.BlockSpec(memory_space=pl.ANY)] * len(carried)
    return specs, carried, lambda n_in: {n_in + k: first_output + k for k in range(len(carried))}


def _pool_layer(x, mods, norm_g, win, wgrp, psc, wout, hist, states, *, i, row0, tm):
    B, T, D = x.shape
    j = i // 2
    n_layers = win.shape[0]
    sample = hist is not None
    n_seg, seg = (B, T) if sample else (1, tm)
    grid = (B // n_seg, T // seg)
    assert row0 % n_seg == 0
    in_specs = [
        pl.BlockSpec((n_seg, seg, D), lambda b, t: (b, t, 0)),
        pl.BlockSpec((None, n_seg, 3, D), lambda b, t: (i, row0 // n_seg + b, 0, 0)),
        _stacked(i, 1, D),
        _stacked(j, D, 2 * POOL_WIDTH),
        _stacked(j, len(POOL_WINDOWS), POOL_GROUP_WIDTH, POOL_GROUP_WIDTH),
        _stacked(j, 1, POOL_WIDTH),
        _stacked(j, POOL_WIDTH, D),
    ]
    args = [x, mods, norm_g, win, wgrp, psc, wout]
    if sample:
        in_specs.append(pl.BlockSpec((None, n_seg, HIST_ROWS, POOL_WIDTH), lambda b, t: (j, b, 0, 0)))
        args.append(hist)
    tm = n_seg * seg
    carry_specs, carry_args, aliases = _carry(None if states is None else [states], first_output=1)
    return pl.pallas_call(
        functools.partial(_pool_kernel, n_seg=n_seg, seg=seg, nt=grid[1], sample=sample, n_carried=len(carry_args)),
        grid=grid,
        in_specs=in_specs + carry_specs,
        out_specs=[
            pl.BlockSpec((n_seg, seg, D), lambda b, t: (b, t, 0)),
            pl.BlockSpec((None, n_seg, HIST_ROWS, POOL_WIDTH), lambda b, t: (j, b, 0, 0)),
        ],
        out_shape=[
            jax.ShapeDtypeStruct((B, T, D), _F32),
            jax.ShapeDtypeStruct((n_layers, B, HIST_ROWS, POOL_WIDTH), _F32),
        ],
        input_output_aliases=aliases(len(args)),
        scratch_shapes=[
            pltpu.VMEM((n_seg, HIST_ROWS + seg, POOL_WIDTH), _F32),
            pltpu.VMEM((tm, POOL_WIDTH), _F32),
            pltpu.VMEM((tm, POOL_WIDTH), _BF16),
            pltpu.VMEM((tm, POOL_WIDTH), _BF16),
            pltpu.VMEM((tm, D), _BF16),
        ],
        compiler_params=pltpu.CompilerParams(
            dimension_semantics=("arbitrary", "arbitrary"), vmem_limit_bytes=VMEM_LIMIT_BYTES),
        name="pool_layer_sample" if sample else "pool_layer_prompt",
    )(*args, *carry_args)


def _store_split_heads(dst_lo, dst_hi, rows, val):
    n = val.shape[0]
    lo = lax.broadcasted_iota(jnp.int32, (n, LANES), 1) < HEAD_DIM
    zero = jnp.zeros((n, LANES), _F32)
    for j in range(KV_WIDTH // LANES):
        a = val[:, j * LANES:(j + 1) * LANES]
        ar = pltpu.roll(a, HEAD_DIM, axis=1)
        dst_lo[HEAD_PAIR * j, rows, :] = jnp.where(lo, a, zero).astype(_BF16)
        dst_hi[HEAD_PAIR * j, rows, :] = jnp.where(lo, zero, ar).astype(_BF16)
        dst_lo[HEAD_PAIR * j + 1, rows, :] = jnp.where(lo, ar, zero).astype(_BF16)
        dst_hi[HEAD_PAIR * j + 1, rows, :] = jnp.where(lo, zero, a).astype(_BF16)


def _attn_sample_kernel(*refs, n_batch, tm, layer, final_norm, n_carried):
    refs = list(refs)
    sink_ref, x_ref, mod_ref, g_ref, win_ref, wout_ref, ck_ref, cv_ref = refs[:8]
    del refs[:8]
    if final_norm:
        fg_ref = refs.pop(0)
    del refs[:n_carried]
    xo_ref, kw_ref, vw_ref, qbuf, k_lo, k_hi, v_lo, v_hi, sgbuf, zbuf, bias_ref, hbuf = refs
    batch_rows = [slice(b * tm, (b + 1) * tm) for b in range(n_batch)]

    for b, rows in enumerate(batch_rows):
        h = _modulated_norm(x_ref[b], g_ref[...], mod_ref[b, 0:1, :], mod_ref[b, 1:2, :])
        hbuf[rows, :] = h.astype(_BF16)
    hb = hbuf[...]
    q0, k0, v0, g0 = 0, ATTN_WIDTH, ATTN_WIDTH + KV_WIDTH, ATTN_WIDTH + 2 * KV_WIDTH
    qbuf[...] = (_dot(hb, win_ref[:, q0:k0]) * (HEAD_DIM ** -0.5 * LOG2_E)).astype(_BF16)
    k = _dot(hb, win_ref[:, k0:v0])
    v = _dot(hb, win_ref[:, v0:g0])
    sgbuf[...] = _silu(_dot(hb, win_ref[:, g0:g0 + ATTN_WIDTH]))
    for b, rows in enumerate(batch_rows):
        band0 = b * BAND_KEYS
        _store_split_heads(k_lo, k_hi, slice(band0, band0 + WINDOW), ck_ref[b])
        _store_split_heads(v_lo, v_hi, slice(band0, band0 + WINDOW), cv_ref[b])
        _store_split_heads(k_lo, k_hi, slice(band0 + WINDOW, band0 + WINDOW + tm), k[rows])
        _store_split_heads(v_lo, v_hi, slice(band0 + WINDOW, band0 + WINDOW + tm), v[rows])
        for buf in (k_lo, k_hi, v_lo, v_hi):
            buf[:, band0 + WINDOW + tm:band0 + BAND_KEYS, :] = jnp.zeros(
                (N_KV_HEADS, BAND_KEYS - WINDOW - tm, LANES), _BF16)
        kw_ref[b, 0:WINDOW - tm, :] = ck_ref[b, tm:WINDOW, :]
        vw_ref[b, 0:WINDOW - tm, :] = cv_ref[b, tm:WINDOW, :]
        kw_ref[b, WINDOW - tm:WINDOW, :] = k[rows]
        vw_ref[b, WINDOW - tm:WINDOW, :] = v[rows]

    qi = lax.broadcasted_iota(jnp.int32, (tm, BAND_KEYS), 0)
    sj = lax.broadcasted_iota(jnp.int32, (tm, BAND_KEYS), 1)
    dist = jnp.abs(WINDOW + qi - sj).astype(_F32)
    for h, slope in enumerate(_alibi_slopes()):
        bias_ref[h] = dist * (-slope * LOG2_E)

    lane = lax.broadcasted_iota(jnp.int32, (tm, LANES), 1)
    lo_lanes = lax.broadcasted_iota(jnp.int32, (HEAD_PAIR * tm, LANES), 1) < HEAD_DIM
    key_lo_lanes = lax.broadcasted_iota(jnp.int32, (BAND_KEYS, LANES), 1) < HEAD_DIM
    ones_lo = jnp.where(key_lo_lanes, 1.0, 0.0).astype(_BF16)
    ones_hi = jnp.where(key_lo_lanes, 0.0, 1.0).astype(_BF16)

    qc = WINDOW // CHUNK
    masks = []
    for col in range(BAND_KEYS // LANES):
        conds = []
        for half in range(LANES // CHUNK):
            kc = (col * LANES) // CHUNK + half
            if not (qc - WINDOW_CHUNKS <= kc <= qc):
                conds.append(lane >= CHUNK if half == 0 else lane < CHUNK)
        if col * LANES + LANES > WINDOW + tm:
            conds.append(lane < WINDOW + tm - col * LANES)
        masks.append(functools.reduce(jnp.logical_and, conds) if conds else None)

    n_pairs = GQA_GROUP // HEAD_PAIR

    def score_dots(b, kh, st):
        c0 = kh * GQA_GROUP * HEAD_DIM
        qs = jnp.concatenate(
            [qbuf[batch_rows[b], c0 + p * LANES:c0 + (p + 1) * LANES] for p in range(n_pairs)], axis=0)
        st["s"] = [_dot_nt(qs, kbuf[kh, b * BAND_KEYS:(b + 1) * BAND_KEYS, :]) for kbuf in (k_lo, k_hi)]
        yield

    def probabilities(b, kh, st):
        st["probs"], st["sink_p"] = [], []
        for parity in range(HEAD_PAIR):
            p_blocks, sink_blocks = [], []
            for p in range(n_pairs):
                h = kh * GQA_GROUP + p * HEAD_PAIR + parity
                sink2 = sink_ref[layer, h] * LOG2_E
                cols = []
                for col, allowed in enumerate(masks):
                    lanes = slice(col * LANES, (col + 1) * LANES)
                    s = st["s"][parity][p * tm:(p + 1) * tm, lanes] + bias_ref[h, :, lanes]
                    cols.append(s if allowed is None else jnp.where(allowed, s, MASK_VALUE))
                m = jnp.maximum(jnp.max(functools.reduce(jnp.maximum, cols), axis=-1, keepdims=True), sink2)
                sink_blocks.append(jnp.exp2(sink2 - m))
                p_blocks.append(jnp.concatenate([jnp.exp2(c - m).astype(_BF16) for c in cols], axis=1))
                yield
            st["probs"].append(jnp.concatenate(p_blocks, axis=0))
            st["sink_p"].append(jnp.concatenate(sink_blocks, axis=0))

    def value_dots(b, kh, st):
        band = slice(b * BAND_KEYS, (b + 1) * BAND_KEYS)
        st["acc"] = (_dot(st["probs"][0], jnp.concatenate([v_lo[kh, band, :], ones_lo], axis=1))
                     + _dot(st["probs"][1], jnp.concatenate([v_hi[kh, band, :], ones_hi], axis=1)))
        yield

    def finish(b, kh, st):
        rows, c0 = batch_rows[b], kh * GQA_GROUP * HEAD_DIM
        acc = st["acc"]
        o = acc[:, :LANES] / (acc[:, LANES:] + jnp.where(lo_lanes, st["sink_p"][0], st["sink_p"][1]))
        for p in range(n_pairs):
            cols = slice(c0 + p * LANES, c0 + (p + 1) * LANES)
            zbuf[rows, cols] = (o[p * tm:(p + 1) * tm] * sgbuf[rows, cols]).astype(_BF16)
        st.clear()
        yield

    stages = (score_dots, probabilities, value_dots, finish)
    steps = [(b, kh) for b in range(n_batch) for kh in range(N_KV_HEADS)]
    state = [{} for _ in steps]
    for r in range(len(steps) + len(stages) - 1):
        _emit_round_robin(
            [stage(*steps[r - j], state[r - j]) for j, stage in enumerate(stages) if 0 <= r - j < len(steps)])

    y = _dot(zbuf[...], wout_ref[...])
    for b, rows in enumerate(batch_rows):
        xn = x_ref[b] + mod_ref[b, 2:3, :] * y[rows]
        if final_norm:
            xn = _rms(xn) * fg_ref[...]
        xo_ref[b] = xn


def _attn_sample_layer(x, mods, norm_g, win, sinks, wout, cache_k, cache_v, final_g, windows, *, i, row0):
    B, tm, D = x.shape
    assert tm < CHUNK and PAST_LEN % CHUNK == 0 and row0 % B == 0
    j = i // 2
    n_layers = win.shape[0]
    final_norm = final_g is not None
    whole = lambda *shape: pl.BlockSpec(shape, lambda b, t: (0,) * len(shape))
    in_specs = [
        pl.BlockSpec(memory_space=pltpu.SMEM),
        whole(B, tm, D),
        pl.BlockSpec((None, B, 3, D), lambda b, t: (i, row0 // B, 0, 0)),
        _stacked(i, 1, D),
        _stacked(j, D, 2 * ATTN_WIDTH + 2 * KV_WIDTH),
        _stacked(j, ATTN_WIDTH, D),
        _stacked(j, B, WINDOW, KV_WIDTH),
        _stacked(j, B, WINDOW, KV_WIDTH),
    ]
    args = [sinks, x, mods, norm_g, win, wout, cache_k, cache_v]
    if final_norm:
        in_specs.append(whole(1, D))
        args.append(final_g)
    window_shape = jax.ShapeDtypeStruct((n_layers, B, WINDOW, KV_WIDTH), _F32)
    carry_specs, carry_args, aliases = _carry(windows, first_output=1)
    return pl.pallas_call(
        functools.partial(_attn_sample_kernel, n_batch=B, tm=tm, layer=j, final_norm=final_norm,
                          n_carried=len(carry_args)),
        grid=(1, 1),
        in_specs=in_specs + carry_specs,
        out_specs=[whole(B, tm, D), _stacked(j, B, WINDOW, KV_WIDTH), _stacked(j, B, WINDOW, KV_WIDTH)],
        out_shape=[jax.ShapeDtypeStruct((B, tm, D), _F32), window_shape, window_shape],
        input_output_aliases=aliases(len(args)),
        scratch_shapes=[
            pltpu.VMEM((B * tm, ATTN_WIDTH), _BF16),
            pltpu.VMEM((N_KV_HEADS, B * BAND_KEYS, LANES), _BF16),
            pltpu.VMEM((N_KV_HEADS, B * BAND_KEYS, LANES), _BF16),
            pltpu.VMEM((N_KV_HEADS, B * BAND_KEYS, LANES), _BF16),
            pltpu.VMEM((N_KV_HEADS, B * BAND_KEYS, LANES), _BF16),
            pltpu.VMEM((B * tm, ATTN_WIDTH), _F32),
            pltpu.VMEM((B * tm, ATTN_WIDTH), _BF16),
            pltpu.VMEM((N_HEADS, tm, BAND_KEYS), _F32),
            pltpu.VMEM((B * tm, D), _BF16),
        ],
        compiler_params=pltpu.CompilerParams(
            dimension_semantics=("arbitrary", "arbitrary"), vmem_limit_bytes=VMEM_LIMIT_BYTES),
        name="attn_layer_sample",
    )(*args, *carry_args)


SUM_ROWS = 2 * SUBLANES
VT_ROWS = HEAD_DIM + SUM_ROWS


def _dot_tn(a, b):
    return lax.dot_general(a, b, (((0,), (0,)), ((), ())), preferred_element_type=_F32)


def _attn_prompt_kernel(*refs, tm, nt, layer, final_norm, n_carried):
    refs = list(refs)
    n_col_blocks = ATTN_WIDTH // PROJ_FEATURES
    sink_ref, x_ref, mod_ref, g_ref = refs[:4]
    del refs[:4]
    wq_refs, wg_refs = refs[:n_col_blocks], refs[n_col_blocks:2 * n_col_blocks]
    del refs[:2 * n_col_blocks]
    wk_ref, wvt_ref, wv_ref, wout_ref = refs[:4]
    del refs[:4]
    if final_norm:
        fg_ref = refs.pop(0)
    del refs[:n_carried]
    xo_ref, kw_ref, vw_ref, qt, k_lo, k_hi, vt_buf, sgt, zt, bias_t, hbuf = refs
    t = pl.program_id(1)
    n_pairs = GQA_GROUP // HEAD_PAIR
    slopes = _alibi_slopes()

    @pl.when((pl.program_id(0) == 0) & (t == 0))
    def _fill_constants():
        row = lax.broadcasted_iota(jnp.int32, (VT_ROWS, WINDOW + tm), 0)
        ones_rows = (row >= HEAD_DIM) & (row < HEAD_DIM + SUBLANES)
        for kh in range(N_KV_HEADS):
            vt_buf[kh] = jnp.where(ones_rows, 1.0, 0.0).astype(_BF16)
        si = lax.broadcasted_iota(jnp.int32, (BAND_KEYS, QUERY_GROUP), 0)
        qj = lax.broadcasted_iota(jnp.int32, (BAND_KEYS, QUERY_GROUP), 1)
        dist = jnp.abs(WINDOW + qj - si).astype(_F32)
        for h in range(N_HEADS):
            bias_t[h] = dist * (-slopes[h] * LOG2_E)

    @pl.when(t == 0)
    def _reset_window():
        for buf in (k_lo, k_hi):
            buf[:, 0:WINDOW, :] = jnp.zeros((N_KV_HEADS, WINDOW, LANES), _BF16)
        vt_buf[:, 0:HEAD_DIM, 0:WINDOW] = jnp.zeros((N_KV_HEADS, HEAD_DIM, WINDOW), _BF16)

    shift, scale, gmod = mod_ref[0, 0:1, :], mod_ref[0, 1:2, :], mod_ref[0, 2:3, :]
    n_parts = tm // PROJ_ROWS


    run_region = _emit_round_robin

    def norm_part(j):
        for r0 in range(j * PROJ_ROWS, (j + 1) * PROJ_ROWS, CHUNK):
            rows = slice(r0, r0 + CHUNK)
            hbuf[rows, :] = _modulated_norm(x_ref[0, rows, :], g_ref[...], shift, scale).astype(_BF16)
            yield

    def project_part(j):
        rows = slice(j * PROJ_ROWS, (j + 1) * PROJ_ROWS)
        new = slice(WINDOW + j * PROJ_ROWS, WINDOW + (j + 1) * PROJ_ROWS)
        hb = hbuf[rows, :]
        for n, w_ref in enumerate(wg_refs):
            feats = slice(n * PROJ_FEATURES, (n + 1) * PROJ_FEATURES)
            sgt[feats, rows] = _dot(hb, w_ref[...]).T
            yield
        for n, w_ref in enumerate(wq_refs):
            feats = slice(n * PROJ_FEATURES, (n + 1) * PROJ_FEATURES)
            qt[feats, rows] = (_dot(hb, w_ref[...]).T * (HEAD_DIM ** -0.5 * LOG2_E)).astype(_BF16)
            yield
        vt = _dot_nt(wvt_ref[...], hb)
        for kh in range(N_KV_HEADS):
            vt_buf[kh, 0:HEAD_DIM, new] = vt[kh * HEAD_DIM:(kh + 1) * HEAD_DIM, :].astype(_BF16)
        yield
        _store_split_heads(k_lo, k_hi, new, _dot(hb, wk_ref[...]))
        yield

    def silu_part(j, f_lo=0, f_hi=ATTN_WIDTH):
        rows = slice(j * PROJ_ROWS, (j + 1) * PROJ_ROWS)
        for f0 in range(f_lo, f_hi, LANES):
            feats = slice(f0, f0 + LANES)
            sgt[feats, rows] = _silu(sgt[feats, rows])
            yield

    def output_part(grp):
        rows = slice(grp * QUERY_GROUP, (grp + 1) * QUERY_GROUP)
        y = _dot_tn(zt[:, rows], wout_ref[...])
        xn = x_ref[0, rows, :] + gmod * y
        if final_norm:
            xn = _rms(xn) * fg_ref[...]
        xo_ref[0, rows, :] = xn
        yield

    lane = lax.broadcasted_iota(jnp.int32, (CHUNK, LANES), 1)

    def key_chunk_mask(grp, kc):
        conds = []
        for i in range(QUERY_GROUP // CHUNK):
            qc = WINDOW_CHUNKS + i
            if not (qc - WINDOW_CHUNKS <= kc <= qc):
                conds.append(lane >= CHUNK if i == 0 else lane < CHUNK)
        if grp == 0 and kc < WINDOW_CHUNKS:
            conds.append(t > 0)
        return functools.reduce(jnp.logical_and, conds) if conds else None


    def score_dots(grp, kh, st):
        r0, c0 = grp * QUERY_GROUP, kh * GQA_GROUP * HEAD_DIM
        q_lanes, band = slice(r0, r0 + QUERY_GROUP), slice(r0, r0 + BAND_KEYS)
        q_rhs = jnp.concatenate([qt[c0 + p * LANES:c0 + (p + 1) * LANES, q_lanes] for p in range(n_pairs)], axis=1)
        st["s_t"] = []
        for kbuf in (k_lo, k_hi):
            st["s_t"].append(_dot(kbuf[kh, band, :], q_rhs))
            yield

    def score_max(grp, kh, st):
        masks = [key_chunk_mask(grp, kc) for kc in range(BAND_KEYS // CHUNK)]
        st["blocks"], st["m"], st["sink_p"] = [], [], []
        for parity in range(HEAD_PAIR):
            for p in range(n_pairs):
                h = kh * GQA_GROUP + p * HEAD_PAIR + parity
                sink2 = sink_ref[layer, h] * LOG2_E
                blocks = []
                for kc, allowed in enumerate(masks):
                    rows = slice(kc * CHUNK, (kc + 1) * CHUNK)
                    s = st["s_t"][parity][rows, p * QUERY_GROUP:(p + 1) * QUERY_GROUP] + bias_t[h, rows, :]
                    blocks.append(s if allowed is None else jnp.where(allowed, s, MASK_VALUE))
                m = jnp.max(functools.reduce(jnp.maximum, blocks), axis=0, keepdims=True)
                m = jnp.maximum(m, sink2)
                st["blocks"].append(blocks)
                st["m"].append(m)
                st["sink_p"].append(jnp.exp2(sink2 - m))
                yield

    def probabilities(grp, kh, st):
        st["p_t"] = []
        for blocks, m in zip(st["blocks"], st["m"]):
            st["p_t"].append(jnp.concatenate([jnp.exp2(b - m).astype(_BF16) for b in blocks], axis=0))
            yield

    def value_dots(grp, kh, st):
        band = slice(grp * QUERY_GROUP, grp * QUERY_GROUP + BAND_KEYS)
        st["acc"] = []
        for parity in range(HEAD_PAIR):
            p_t = jnp.concatenate(st["p_t"][parity * n_pairs:(parity + 1) * n_pairs], axis=1)
            st["acc"].append(_dot(vt_buf[kh, :, band], p_t))
            yield

    def finish(grp, kh, st):
        r0, c0 = grp * QUERY_GROUP, kh * GQA_GROUP * HEAD_DIM
        q_lanes = slice(r0, r0 + QUERY_GROUP)
        for p in range(n_pairs):
            lanes = slice(p * QUERY_GROUP, (p + 1) * QUERY_GROUP)
            heads = []
            for parity in range(HEAD_PAIR):
                acc = st["acc"][parity]
                row_sum = acc[HEAD_DIM:HEAD_DIM + 1, lanes] + st["sink_p"][parity * n_pairs + p]
                heads.append(acc[0:HEAD_DIM, lanes] * (1.0 / row_sum))
            out = jnp.concatenate(heads, axis=0)
            rows = slice(c0 + p * LANES, c0 + (p + 1) * LANES)
            zt[rows, q_lanes] = (out * sgt[rows, q_lanes]).astype(_BF16)
            yield
        st.clear()

    stages = (score_dots, score_max, probabilities, value_dots, finish)
    steps = [(grp, kh) for grp in range(tm // QUERY_GROUP) for kh in range(N_KV_HEADS)]
    state = [{} for _ in steps]
    n_regions = len(steps) + len(stages) - 1
    fill = len(stages) - 1
    per_part = len(steps) // n_parts

    def advance(gen, n):
        for _ in range(n):
            if next(gen, gen) is gen:
                return
            yield

    def spread(gen, first, count, per_region):
        for r in range(first, first + count):
            extras[r].append(advance(gen, per_region))

    extras = [[] for _ in range(n_regions + 1)]
    for j in range(n_parts):
        r0 = j * per_part
        if j > 0:
            spread(silu_part(j), r0, fill, ATTN_WIDTH // LANES // fill)
        if j + 1 < n_parts:
            n_dots = 2 * (ATTN_WIDTH // PROJ_FEATURES) + 2
            spread(project_part(j + 1), r0, per_part - 2, pl.cdiv(n_dots, per_part - 2))
        if j + 2 < n_parts:
            spread(norm_part(j + 2), r0 + per_part - 2, 2, PROJ_ROWS // CHUNK // 2)
    for grp in range(tm // QUERY_GROUP):
        extras[min((grp + 1) * N_KV_HEADS + fill, n_regions)].append(output_part(grp))

    run_region([norm_part(0)])
    first = project_part(0)
    run_region([advance(first, ATTN_WIDTH // PROJ_FEATURES)] + ([norm_part(1)] if n_parts > 1 else []))
    run_region([first, silu_part(0)])
    for r in range(n_regions):
        run_region([stage(*steps[r - j], state[r - j]) for j, stage in enumerate(stages) if 0 <= r - j < len(steps)]
                   + extras[r])
    run_region(extras[n_regions])

    @pl.when(t == nt - 1)
    def _emit_window():
        last = hbuf[tm - WINDOW:tm, :]
        kw_ref[0] = _dot(last, wk_ref[...])
        vw_ref[0] = _dot(last, wv_ref[...])

    for buf in (k_lo, k_hi):
        buf[:, 0:WINDOW, :] = buf[:, tm:tm + WINDOW, :]
    vt_buf[:, :, 0:WINDOW] = vt_buf[:, :, tm:tm + WINDOW]


def _split_attn_w_in(win):
    k0, v0, g0 = ATTN_WIDTH, ATTN_WIDTH + KV_WIDTH, ATTN_WIDTH + 2 * KV_WIDTH
    wk, wv = win[:, :, k0:v0], win[:, :, v0:g0]
    return win, wk, jnp.swapaxes(wv, 1, 2), wv


def _attn_prompt_layer(x, mods, norm_g, win_parts, sinks, wout, final_g, windows, *, i, row0, tm):
    B, T, D = x.shape
    nt = T // tm
    j = i // 2
    final_norm = final_g is not None
    win, wk, wv_t, wv = win_parts
    q0, g0 = 0, ATTN_WIDTH + 2 * KV_WIDTH
    n_col_blocks = ATTN_WIDTH // PROJ_FEATURES
    assert g0 % PROJ_FEATURES == 0
    col_block = lambda c: pl.BlockSpec((None, D, PROJ_FEATURES), lambda b, t: (j, 0, c))
    in_specs = [
        pl.BlockSpec(memory_space=pltpu.SMEM),
        pl.BlockSpec((1, tm, D), lambda b, t: (b, t, 0)),
        _mod_spec(i, row0),
        _stacked(i, 1, D),
        *[col_block(q0 // PROJ_FEATURES + n) for n in range(n_col_blocks)],
        *[col_block(g0 // PROJ_FEATURES + n) for n in range(n_col_blocks)],
        _stacked(j, D, KV_WIDTH), _stacked(j, KV_WIDTH, D), _stacked(j, D, KV_WIDTH),
        _stacked(j, ATTN_WIDTH, D),
    ]
    args = [sinks, x, mods, norm_g, *([win] * (2 * n_col_blocks)), wk, wv_t, wv, wout]
    if final_norm:
        in_specs.append(pl.BlockSpec((1, D), lambda b, t: (0, 0)))
        args.append(final_g)
    window_spec = pl.BlockSpec((None, 1, WINDOW, KV_WIDTH), lambda b, t: (j, b, 0, 0))
    window_shape = jax.ShapeDtypeStruct((win.shape[0], B, WINDOW, KV_WIDTH), _F32)
    carry_specs, carry_args, aliases = _carry(windows, first_output=1)
    return pl.pallas_call(
        functools.partial(_attn_prompt_kernel, tm=tm, nt=nt, layer=j, final_norm=final_norm,
                          n_carried=len(carry_args)),
        grid=(B, nt),
        in_specs=in_specs + carry_specs,
        out_specs=[pl.BlockSpec((1, tm, D), lambda b, t: (b, t, 0)), window_spec, window_spec],
        out_shape=[jax.ShapeDtypeStruct((B, T, D), _F32), window_shape, window_shape],
        input_output_aliases=aliases(len(args)),
        scratch_shapes=[
            pltpu.VMEM((ATTN_WIDTH, tm), _BF16),
            pltpu.VMEM((N_KV_HEADS, WINDOW + tm, LANES), _BF16),
            pltpu.VMEM((N_KV_HEADS, WINDOW + tm, LANES), _BF16),
            pltpu.VMEM((N_KV_HEADS, VT_ROWS, WINDOW + tm), _BF16),
            pltpu.VMEM((ATTN_WIDTH, tm), _F32),
            pltpu.VMEM((ATTN_WIDTH, tm), _BF16),
            pltpu.VMEM((N_HEADS, BAND_KEYS, QUERY_GROUP), _F32),
            pltpu.VMEM((tm, D), _BF16),
        ],
        compiler_params=pltpu.CompilerParams(
            dimension_semantics=("arbitrary", "arbitrary"), vmem_limit_bytes=VMEM_LIMIT_BYTES),
        name="attn_layer_prompt",
    )(*args, *carry_args)


def kernel(x_prompt, x_sample, c_prompt, c_sample, cache_pool, cache_k, cache_v, norm_g, ada_w, ada_b,
           pool_w_in, pool_w_grp, pool_scale, pool_w_out, attn_w_in, attn_sinks, attn_w_out, final_g):
    n_prompt, n_sample = x_prompt.shape[0], x_sample.shape[0]
    mods = _modulation(jnp.concatenate([c_prompt, c_sample], axis=0), ada_w, ada_b)
    mods = mods.reshape(DEPTH, n_prompt + n_sample, 3, D_MODEL)

    norm_g = norm_g.reshape(DEPTH, 1, D_MODEL)
    final_g = final_g.reshape(1, D_MODEL)
    pool_w_in, pool_w_grp, pool_w_out = (w.astype(_BF16) for w in (pool_w_in, pool_w_grp, pool_w_out))
    pool_scale = pool_scale.reshape(-1, 1, POOL_WIDTH)
    attn_w_in, attn_w_out = attn_w_in.astype(_BF16), attn_w_out.astype(_BF16)
    attn_w_in_parts = _split_attn_w_in(attn_w_in)
    hist = jnp.pad(cache_pool, ((0, 0), (0, 0), (HIST_ROWS - POOL_HIST, 0), (0, 0)))
    cache_k = cache_k.reshape(cache_k.shape[:3] + (KV_WIDTH,))
    cache_v = cache_v.reshape(cache_v.shape[:3] + (KV_WIDTH,))

    def trunk(x, row0, sample):
        rows = x.shape[1]
        states, windows = None, None
        for i in range(DEPTH):
            if i % 2 == 0:
                x, states = _pool_layer(x, mods, norm_g, pool_w_in, pool_w_grp, pool_scale, pool_w_out,
                                        hist if sample else None, states, i=i, row0=row0,
                                        tm=rows if sample else min(POOL_PROMPT_TILE, rows))
            else:
                fg = final_g if i == DEPTH - 1 else None
                if sample:
                    x, *windows = _attn_sample_layer(x, mods, norm_g, attn_w_in, attn_sinks, attn_w_out,
                                                     cache_k, cache_v, fg, windows, i=i, row0=row0)
                else:
                    x, *windows = _attn_prompt_layer(x, mods, norm_g, attn_w_in_parts, attn_sinks, attn_w_out, fg,
                                                     windows, i=i, row0=row0, tm=min(ATTN_PROMPT_TILE, rows))
        k_win, v_win = (w.reshape(w.shape[:3] + (N_KV_HEADS, HEAD_DIM)) for w in windows)
        return x, states[:, :, HIST_ROWS - POOL_HIST:], k_win, v_win

    y_prompt, pool_p, k_p, v_p = trunk(x_prompt, 0, False)
    y_sample, pool_s, k_s, v_s = trunk(x_sample, n_prompt, True)
    return (y_prompt, y_sample, pool_p, k_p, v_p, pool_s, k_s, v_s)
```

```python
import functools

import numpy as np
import jax
import jax.numpy as jnp
from jax import lax
from jax.experimental import pallas as pl
from jax.experimental.pallas import tpu as pltpu

D_MODEL = 1024
DEPTH = 4
PAST_LEN = 1024
CHUNK = 64
N_HEADS = 16
HEAD_DIM = 64
N_KV_HEADS = 4
GQA_GROUP = N_HEADS // N_KV_HEADS
ATTN_WIDTH = N_HEADS * HEAD_DIM
KV_WIDTH = N_KV_HEADS * HEAD_DIM
WINDOW = 128
WINDOW_CHUNKS = WINDOW // CHUNK
POOL_WIDTH = D_MODEL
POOL_WINDOWS = (2, 4, 8, 16)
POOL_GROUP_WIDTH = POOL_WIDTH // len(POOL_WINDOWS)
POOL_HIST = max(POOL_WINDOWS) - 1
NORM_EPS = 1e-6
MASK_VALUE = -1e30
LOG2_E = 1.4426950408889634

LANES = 128
SUBLANES = 8
VMEM_LIMIT_BYTES = 56 * 1024 * 1024

HIST_ROWS = 2 * SUBLANES
HEAD_PAIR = LANES // HEAD_DIM
BAND_KEYS = 2 * WINDOW
POOL_PROMPT_TILE = 1024
ATTN_PROMPT_TILE = 1024
QUERY_GROUP = 2 * CHUNK
PROJ_ROWS = 256
PROJ_FEATURES = 512

_F32 = jnp.float32
_BF16 = jnp.bfloat16


def _alibi_slopes():
    h = np.arange(1, N_HEADS + 1, dtype=np.float32)
    return [float(s) for s in np.exp2(np.float32(-8.0) * h / np.float32(N_HEADS))]


def _rms(x):
    return x * lax.rsqrt(jnp.mean(x * x, axis=-1, keepdims=True) + NORM_EPS)


def _modulated_norm(x, g, shift, scale):
    return (_rms(x) * g) * (1.0 + scale) + shift


def _silu(x):
    h = 0.5 * x
    return h + h * jnp.tanh(h)


def _emit_round_robin(gens):
    live = list(gens)
    while live:
        live = [g for g in live if next(g, live) is not live]


def _dot(a, b):
    return jnp.dot(a, b, preferred_element_type=_F32)


def _dot_nt(a, b):
    return lax.dot_general(a, b, (((1,), (1,)), ((), ())), preferred_element_type=_F32)


def _mod_kernel(c_ref, w_ref, b_ref, o_ref):
    c = c_ref[...]
    o_ref[0] = _dot(_silu(c).astype(_BF16), w_ref[0].astype(_BF16)) + b_ref[0]


def _modulation(c_all, ada_w, ada_b):
    rows = c_all.shape[0]
    n_col_tiles = 3
    return pl.pallas_call(
        _mod_kernel,
        grid=(DEPTH, n_col_tiles),
        in_specs=[
            pl.BlockSpec((rows, D_MODEL), lambda i, n: (0, 0)),
            pl.BlockSpec((1, D_MODEL, D_MODEL), lambda i, n: (i, 0, n)),
            pl.BlockSpec((1, 1, D_MODEL), lambda i, n: (i, 0, n)),
        ],
        out_specs=pl.BlockSpec((1, rows, D_MODEL), lambda i, n: (i, 0, n)),
        out_shape=jax.ShapeDtypeStruct((DEPTH, rows, 3 * D_MODEL), _F32),
        compiler_params=pltpu.CompilerParams(
            dimension_semantics=("arbitrary", "arbitrary"), vmem_limit_bytes=VMEM_LIMIT_BYTES),
        name="adaln_modulation",
    )(c_all, ada_w, ada_b.reshape(DEPTH, 1, 3 * D_MODEL))


def _pool_kernel(*refs, n_seg, seg, nt, sample, n_carried):
    refs = list(refs)
    x_ref, mod_ref, g_ref, win_ref, wgrp_ref, psc_ref, wout_ref = refs[:7]
    del refs[:7]
    if sample:
        hist_ref = refs.pop(0)
    del refs[:n_carried]
    xo_ref, st_ref, ubuf, sgbuf, pbuf, zbuf, hbuf = refs
    t = pl.program_id(1)
    tm = n_seg * seg

    @pl.when(t == 0)
    def _load_history():
        if sample:
            ubuf[:, 0:HIST_ROWS, :] = hist_ref[...]
        else:
            ubuf[:, 0:HIST_ROWS, :] = jnp.zeros((n_seg, HIST_ROWS, POOL_WIDTH), _F32)

    part = min(PROJ_ROWS, tm)
    piece = min(part, seg)
    rc = min(CHUNK // 2, piece)
    group_cols = [slice(g * POOL_GROUP_WIDTH, (g + 1) * POOL_GROUP_WIDTH) for g in range(len(POOL_WINDOWS))]

    def pieces(p, n):
        return [(r // seg, r % seg, r - p * part) for r in range(p * part, (p + 1) * part, n)]


    def norm_part(p):
        for b, o, i in pieces(p, rc):
            h = _modulated_norm(x_ref[b, o:o + rc, :], g_ref[...], mod_ref[b, 0:1, :], mod_ref[b, 1:2, :])
            hbuf[p * part + i:p * part + i + rc, :] = h.astype(_BF16)
            yield

    def project_part(p):
        rows = slice(p * part, (p + 1) * part)
        hb = hbuf[rows, :]
        u = _dot(hb, win_ref[:, 0:POOL_WIDTH])
        for b, o, i in pieces(p, piece):
            ubuf[b, HIST_ROWS + o:HIST_ROWS + o + piece, :] = u[i:i + piece]
        yield
        sgbuf[rows, :] = _dot(hb, win_ref[:, POOL_WIDTH:2 * POOL_WIDTH])
        yield

    def pool_part(p):
        for b, o, i in pieces(p, rc):
            rows = slice(p * part + i, p * part + i + rc)
            sgbuf[rows, :] = _silu(sgbuf[rows, :])
            for cols, w in zip(group_cols, POOL_WINDOWS):
                ext = ubuf[b, o:o + HIST_ROWS + rc, cols]
                s = ext
                k = 1
                while k < w:
                    s = s + pltpu.roll(s, k, axis=0)
                    k *= 2
                s, u = s[HIST_ROWS:], ext[HIST_ROWS:]
                if sample or o > 0:
                    pooled = s * (1.0 / w) - u
                else:
                    pos = t * seg + lax.broadcasted_iota(jnp.int32, (rc, POOL_GROUP_WIDTH), 0)
                    pooled = s / jnp.minimum(pos + 1, w).astype(_F32) - u
                pbuf[rows, cols] = pooled.astype(_BF16)
            yield

    def mix_part(p):
        rows = slice(p * part, (p + 1) * part)
        for g, cols in enumerate(group_cols):
            z = _dot(pbuf[rows, cols], wgrp_ref[g]) * psc_ref[:, cols]
            zbuf[rows, cols] = (z * sgbuf[rows, cols]).astype(_BF16)
            yield

    def output_part(p):
        y = _dot(zbuf[p * part:(p + 1) * part, :], wout_ref[...])
        for b, o, i in pieces(p, piece):
            xo_ref[b, o:o + piece, :] = x_ref[b, o:o + piece, :] + mod_ref[b, 2:3, :] * y[i:i + piece]
        yield

    stages = (norm_part, project_part, pool_part, mix_part, output_part)
    n_parts = tm // part
    for r in range(n_parts + len(stages) - 1):
        _emit_round_robin([stage(r - j) for j, stage in enumerate(stages) if 0 <= r - j < n_parts])

    @pl.when(t == nt - 1)
    def _emit_state():
        st_ref[...] = ubuf[:, seg:seg + HIST_ROWS, :]

    ubuf[:, 0:HIST_ROWS, :] = ubuf[:, seg:seg + HIST_ROWS, :]


def _stacked(index, *shape):
    return pl.BlockSpec((None,) + shape, lambda b, t: (index,) + (0,) * len(shape))


def _mod_spec(i, row0):
    return pl.BlockSpec((None, 1, 3, D_MODEL), lambda b, t: (i, row0 + b, 0, 0))


def _carry(carried, first_output):
    carried = list(carried or ())
    specs = [pl.BlockSpec(memory_space=pl.ANY)] * len(carried)
    return specs, carried, lambda n_in: {n_in + k: first_output + k for k in range(len(carried))}


def _pool_layer(x, mods, norm_g, win, wgrp, psc, wout, hist, states, *, i, row0, tm):
    B, T, D = x.shape
    j = i // 2
    n_layers = win.shape[0]
    sample = hist is not None
    n_seg, seg = (B, T) if sample else (1, tm)
    grid = (B // n_seg, T // seg)
    assert row0 % n_seg == 0
    in_specs = [
        pl.BlockSpec((n_seg, seg, D), lambda b, t: (b, t, 0)),
        pl.BlockSpec((None, n_seg, 3, D), lambda b, t: (i, row0 // n_seg + b, 0, 0)),
        _stacked(i, 1, D),
        _stacked(j, D, 2 * POOL_WIDTH),
        _stacked(j, len(POOL_WINDOWS), POOL_GROUP_WIDTH, POOL_GROUP_WIDTH),
        _stacked(j, 1, POOL_WIDTH),
        _stacked(j, POOL_WIDTH, D),
    ]
    args = [x, mods, norm_g, win, wgrp, psc, wout]
    if sample:
        in_specs.append(pl.BlockSpec((None, n_seg, HIST_ROWS, POOL_WIDTH), lambda b, t: (j, b, 0, 0)))
        args.append(hist)
    tm = n_seg * seg
    carry_specs, carry_args, aliases = _carry(None if states is None else [states], first_output=1)
    return pl.pallas_call(
        functools.partial(_pool_kernel, n_seg=n_seg, seg=seg, nt=grid[1], sample=sample, n_carried=len(carry_args)),
        grid=grid,
        in_specs=in_specs + carry_specs,
        out_specs=[
            pl.BlockSpec((n_seg, seg, D), lambda b, t: (b, t, 0)),
            pl.BlockSpec((None, n_seg, HIST_ROWS, POOL_WIDTH), lambda b, t: (j, b, 0, 0)),
        ],
        out_shape=[
            jax.ShapeDtypeStruct((B, T, D), _F32),
            jax.ShapeDtypeStruct((n_layers, B, HIST_ROWS, POOL_WIDTH), _F32),
        ],
        input_output_aliases=aliases(len(args)),
        scratch_shapes=[
            pltpu.VMEM((n_seg, HIST_ROWS + seg, POOL_WIDTH), _F32),
            pltpu.VMEM((tm, POOL_WIDTH), _F32),
            pltpu.VMEM((tm, POOL_WIDTH), _BF16),
            pltpu.VMEM((tm, POOL_WIDTH), _BF16),
            pltpu.VMEM((tm, D), _BF16),
        ],
        compiler_params=pltpu.CompilerParams(
            dimension_semantics=("arbitrary", "arbitrary"), vmem_limit_bytes=VMEM_LIMIT_BYTES),
        name="pool_layer_sample" if sample else "pool_layer_prompt",
    )(*args, *carry_args)


def _store_split_heads(dst_lo, dst_hi, rows, val):
    n = val.shape[0]
    lo = lax.broadcasted_iota(jnp.int32, (n, LANES), 1) < HEAD_DIM
    zero = jnp.zeros((n, LANES), _F32)
    for j in range(KV_WIDTH // LANES):
        a = val[:, j * LANES:(j + 1) * LANES]
        ar = pltpu.roll(a, HEAD_DIM, axis=1)
        dst_lo[HEAD_PAIR * j, rows, :] = jnp.where(lo, a, zero).astype(_BF16)
        dst_hi[HEAD_PAIR * j, rows, :] = jnp.where(lo, zero, ar).astype(_BF16)
        dst_lo[HEAD_PAIR * j + 1, rows, :] = jnp.where(lo, ar, zero).astype(_BF16)
        dst_hi[HEAD_PAIR * j + 1, rows, :] = jnp.where(lo, zero, a).astype(_BF16)


def _attn_sample_kernel(*refs, n_batch, tm, layer, final_norm, n_carried):
    refs = list(refs)
    sink_ref, x_ref, mod_ref, g_ref, win_ref, wout_ref, ck_ref, cv_ref = refs[:8]
    del refs[:8]
    if final_norm:
        fg_ref = refs.pop(0)
    del refs[:n_carried]
    xo_ref, kw_ref, vw_ref, qbuf, k_lo, k_hi, v_lo, v_hi, sgbuf, zbuf, bias_ref, hbuf = refs
    batch_rows = [slice(b * tm, (b + 1) * tm) for b in range(n_batch)]

    for b, rows in enumerate(batch_rows):
        h = _modulated_norm(x_ref[b], g_ref[...], mod_ref[b, 0:1, :], mod_ref[b, 1:2, :])
        hbuf[rows, :] = h.astype(_BF16)
    hb = hbuf[...]
    q0, k0, v0, g0 = 0, ATTN_WIDTH, ATTN_WIDTH + KV_WIDTH, ATTN_WIDTH + 2 * KV_WIDTH
    qbuf[...] = (_dot(hb, win_ref[:, q0:k0]) * (HEAD_DIM ** -0.5 * LOG2_E)).astype(_BF16)
    k = _dot(hb, win_ref[:, k0:v0])
    v = _dot(hb, win_ref[:, v0:g0])
    sgbuf[...] = _silu(_dot(hb, win_ref[:, g0:g0 + ATTN_WIDTH]))
    for b, rows in enumerate(batch_rows):
        band0 = b * BAND_KEYS
        _store_split_heads(k_lo, k_hi, slice(band0, band0 + WINDOW), ck_ref[b])
        _store_split_heads(v_lo, v_hi, slice(band0, band0 + WINDOW), cv_ref[b])
        _store_split_heads(k_lo, k_hi, slice(band0 + WINDOW, band0 + WINDOW + tm), k[rows])
        _store_split_heads(v_lo, v_hi, slice(band0 + WINDOW, band0 + WINDOW + tm), v[rows])
        for buf in (k_lo, k_hi, v_lo, v_hi):
            buf[:, band0 + WINDOW + tm:band0 + BAND_KEYS, :] = jnp.zeros(
                (N_KV_HEADS, BAND_KEYS - WINDOW - tm, LANES), _BF16)
        kw_ref[b, 0:WINDOW - tm, :] = ck_ref[b, tm:WINDOW, :]
        vw_ref[b, 0:WINDOW - tm, :] = cv_ref[b, tm:WINDOW, :]
        kw_ref[b, WINDOW - tm:WINDOW, :] = k[rows]
        vw_ref[b, WINDOW - tm:WINDOW, :] = v[rows]

    qi = lax.broadcasted_iota(jnp.int32, (tm, BAND_KEYS), 0)
    sj = lax.broadcasted_iota(jnp.int32, (tm, BAND_KEYS), 1)
    dist = jnp.abs(WINDOW + qi - sj).astype(_F32)
    for h, slope in enumerate(_alibi_slopes()):
        bias_ref[h] = dist * (-slope * LOG2_E)

    lane = lax.broadcasted_iota(jnp.int32, (tm, LANES), 1)
    lo_lanes = lax.broadcasted_iota(jnp.int32, (HEAD_PAIR * tm, LANES), 1) < HEAD_DIM
    key_lo_lanes = lax.broadcasted_iota(jnp.int32, (BAND_KEYS, LANES), 1) < HEAD_DIM
    ones_lo = jnp.where(key_lo_lanes, 1.0, 0.0).astype(_BF16)
    ones_hi = jnp.where(key_lo_lanes, 0.0, 1.0).astype(_BF16)

    qc = WINDOW // CHUNK
    masks = []
    for col in range(BAND_KEYS // LANES):
        conds = []
        for half in range(LANES // CHUNK):
            kc = (col * LANES) // CHUNK + half
            if not (qc - WINDOW_CHUNKS <= kc <= qc):
                conds.append(lane >= CHUNK if half == 0 else lane < CHUNK)
        if col * LANES + LANES > WINDOW + tm:
            conds.append(lane < WINDOW + tm - col * LANES)
        masks.append(functools.reduce(jnp.logical_and, conds) if conds else None)

    n_pairs = GQA_GROUP // HEAD_PAIR

    def score_dots(b, kh, st):
        c0 = kh * GQA_GROUP * HEAD_DIM
        qs = jnp.concatenate(
            [qbuf[batch_rows[b], c0 + p * LANES:c0 + (p + 1) * LANES] for p in range(n_pairs)], axis=0)
        st["s"] = [_dot_nt(qs, kbuf[kh, b * BAND_KEYS:(b + 1) * BAND_KEYS, :]) for kbuf in (k_lo, k_hi)]
        yield

    def probabilities(b, kh, st):
        st["probs"], st["sink_p"] = [], []
        for parity in range(HEAD_PAIR):
            p_blocks, sink_blocks = [], []
            for p in range(n_pairs):
                h = kh * GQA_GROUP + p * HEAD_PAIR + parity
                sink2 = sink_ref[layer, h] * LOG2_E
                cols = []
                for col, allowed in enumerate(masks):
                    lanes = slice(col * LANES, (col + 1) * LANES)
                    s = st["s"][parity][p * tm:(p + 1) * tm, lanes] + bias_ref[h, :, lanes]
                    cols.append(s if allowed is None else jnp.where(allowed, s, MASK_VALUE))
                m = jnp.maximum(jnp.max(functools.reduce(jnp.maximum, cols), axis=-1, keepdims=True), sink2)
                sink_blocks.append(jnp.exp2(sink2 - m))
                p_blocks.append(jnp.concatenate([jnp.exp2(c - m).astype(_BF16) for c in cols], axis=1))
                yield
            st["probs"].append(jnp.concatenate(p_blocks, axis=0))
            st["sink_p"].append(jnp.concatenate(sink_blocks, axis=0))

    def value_dots(b, kh, st):
        band = slice(b * BAND_KEYS, (b + 1) * BAND_KEYS)
        st["acc"] = (_dot(st["probs"][0], jnp.concatenate([v_lo[kh, band, :], ones_lo], axis=1))
                     + _dot(st["probs"][1], jnp.concatenate([v_hi[kh, band, :], ones_hi], axis=1)))
        yield

    def finish(b, kh, st):
        rows, c0 = batch_rows[b], kh * GQA_GROUP * HEAD_DIM
        acc = st["acc"]
        o = acc[:, :LANES] / (acc[:, LANES:] + jnp.where(lo_lanes, st["sink_p"][0], st["sink_p"][1]))
        for p in range(n_pairs):
            cols = slice(c0 + p * LANES, c0 + (p + 1) * LANES)
            zbuf[rows, cols] = (o[p * tm:(p + 1) * tm] * sgbuf[rows, cols]).astype(_BF16)
        st.clear()
        yield

    stages = (score_dots, probabilities, value_dots, finish)
    steps = [(b, kh) for b in range(n_batch) for kh in range(N_KV_HEADS)]
    state = [{} for _ in steps]
    for r in range(len(steps) + len(stages) - 1):
        _emit_round_robin(
            [stage(*steps[r - j], state[r - j]) for j, stage in enumerate(stages) if 0 <= r - j < len(steps)])

    y = _dot(zbuf[...], wout_ref[...])
    for b, rows in enumerate(batch_rows):
        xn = x_ref[b] + mod_ref[b, 2:3, :] * y[rows]
        if final_norm:
            xn = _rms(xn) * fg_ref[...]
        xo_ref[b] = xn


def _attn_sample_layer(x, mods, norm_g, win, sinks, wout, cache_k, cache_v, final_g, windows, *, i, row0):
    B, tm, D = x.shape
    assert tm < CHUNK and PAST_LEN % CHUNK == 0 and row0 % B == 0
    j = i // 2
    n_layers = win.shape[0]
    final_norm = final_g is not None
    whole = lambda *shape: pl.BlockSpec(shape, lambda b, t: (0,) * len(shape))
    in_specs = [
        pl.BlockSpec(memory_space=pltpu.SMEM),
        whole(B, tm, D),
        pl.BlockSpec((None, B, 3, D), lambda b, t: (i, row0 // B, 0, 0)),
        _stacked(i, 1, D),
        _stacked(j, D, 2 * ATTN_WIDTH + 2 * KV_WIDTH),
        _stacked(j, ATTN_WIDTH, D),
        _stacked(j, B, WINDOW, KV_WIDTH),
        _stacked(j, B, WINDOW, KV_WIDTH),
    ]
    args = [sinks, x, mods, norm_g, win, wout, cache_k, cache_v]
    if final_norm:
        in_specs.append(whole(1, D))
        args.append(final_g)
    window_shape = jax.ShapeDtypeStruct((n_layers, B, WINDOW, KV_WIDTH), _F32)
    carry_specs, carry_args, aliases = _carry(windows, first_output=1)
    return pl.pallas_call(
        functools.partial(_attn_sample_kernel, n_batch=B, tm=tm, layer=j, final_norm=final_norm,
                          n_carried=len(carry_args)),
        grid=(1, 1),
        in_specs=in_specs + carry_specs,
        out_specs=[whole(B, tm, D), _stacked(j, B, WINDOW, KV_WIDTH), _stacked(j, B, WINDOW, KV_WIDTH)],
        out_shape=[jax.ShapeDtypeStruct((B, tm, D), _F32), window_shape, window_shape],
        input_output_aliases=aliases(len(args)),
        scratch_shapes=[
            pltpu.VMEM((B * tm, ATTN_WIDTH), _BF16),
            pltpu.VMEM((N_KV_HEADS, B * BAND_KEYS, LANES), _BF16),
            pltpu.VMEM((N_KV_HEADS, B * BAND_KEYS, LANES), _BF16),
            pltpu.VMEM((N_KV_HEADS, B * BAND_KEYS, LANES), _BF16),
            pltpu.VMEM((N_KV_HEADS, B * BAND_KEYS, LANES), _BF16),
            pltpu.VMEM((B * tm, ATTN_WIDTH), _F32),
            pltpu.VMEM((B * tm, ATTN_WIDTH), _BF16),
            pltpu.VMEM((N_HEADS, tm, BAND_KEYS), _F32),
            pltpu.VMEM((B * tm, D), _BF16),
        ],
        compiler_params=pltpu.CompilerParams(
            dimension_semantics=("arbitrary", "arbitrary"), vmem_limit_bytes=VMEM_LIMIT_BYTES),
        name="attn_layer_sample",
    )(*args, *carry_args)


SUM_ROWS = 2 * SUBLANES
VT_ROWS = HEAD_DIM + SUM_ROWS


def _dot_tn(a, b):
    return lax.dot_general(a, b, (((0,), (0,)), ((), ())), preferred_element_type=_F32)


def _attn_prompt_kernel(*refs, tm, nt, layer, final_norm, n_carried):
    refs = list(refs)
    n_col_blocks = ATTN_WIDTH // PROJ_FEATURES
    sink_ref, x_ref, mod_ref, g_ref = refs[:4]
    del refs[:4]
    wq_refs, wg_refs = refs[:n_col_blocks], refs[n_col_blocks:2 * n_col_blocks]
    del refs[:2 * n_col_blocks]
    wk_ref, wvt_ref, wv_ref, wout_ref = refs[:4]
    del refs[:4]
    if final_norm:
        fg_ref = refs.pop(0)
    del refs[:n_carried]
    xo_ref, kw_ref, vw_ref, qt, k_lo, k_hi, vt_buf, sgt, zt, bias_t, hbuf = refs
    t = pl.program_id(1)
    n_pairs = GQA_GROUP // HEAD_PAIR
    slopes = _alibi_slopes()

    @pl.when((pl.program_id(0) == 0) & (t == 0))
    def _fill_constants():
        row = lax.broadcasted_iota(jnp.int32, (VT_ROWS, WINDOW + tm), 0)
        ones_rows = (row >= HEAD_DIM) & (row < HEAD_DIM + SUBLANES)
        for kh in range(N_KV_HEADS):
            vt_buf[kh] = jnp.where(ones_rows, 1.0, 0.0).astype(_BF16)
        si = lax.broadcasted_iota(jnp.int32, (BAND_KEYS, QUERY_GROUP), 0)
        qj = lax.broadcasted_iota(jnp.int32, (BAND_KEYS, QUERY_GROUP), 1)
        dist = jnp.abs(WINDOW + qj - si).astype(_F32)
        for h in range(N_HEADS):
            bias_t[h] = dist * (-slopes[h] * LOG2_E)

    @pl.when(t == 0)
    def _reset_window():
        for buf in (k_lo, k_hi):
            buf[:, 0:WINDOW, :] = jnp.zeros((N_KV_HEADS, WINDOW, LANES), _BF16)
        vt_buf[:, 0:HEAD_DIM, 0:WINDOW] = jnp.zeros((N_KV_HEADS, HEAD_DIM, WINDOW), _BF16)

    shift, scale, gmod = mod_ref[0, 0:1, :], mod_ref[0, 1:2, :], mod_ref[0, 2:3, :]
    n_parts = tm // PROJ_ROWS


    run_region = _emit_round_robin

    def norm_part(j):
        for r0 in range(j * PROJ_ROWS, (j + 1) * PROJ_ROWS, CHUNK):
            rows = slice(r0, r0 + CHUNK)
            hbuf[rows, :] = _modulated_norm(x_ref[0, rows, :], g_ref[...], shift, scale).astype(_BF16)
            yield

    def project_part(j):
        rows = slice(j * PROJ_ROWS, (j + 1) * PROJ_ROWS)
        new = slice(WINDOW + j * PROJ_ROWS, WINDOW + (j + 1) * PROJ_ROWS)
        hb = hbuf[rows, :]
        for n, w_ref in enumerate(wg_refs):
            feats = slice(n * PROJ_FEATURES, (n + 1) * PROJ_FEATURES)
            sgt[feats, rows] = _dot(hb, w_ref[...]).T
            yield
        for n, w_ref in enumerate(wq_refs):
            feats = slice(n * PROJ_FEATURES, (n + 1) * PROJ_FEATURES)
            qt[feats, rows] = (_dot(hb, w_ref[...]).T * (HEAD_DIM ** -0.5 * LOG2_E)).astype(_BF16)
            yield
        vt = _dot_nt(wvt_ref[...], hb)
        for kh in range(N_KV_HEADS):
            vt_buf[kh, 0:HEAD_DIM, new] = vt[kh * HEAD_DIM:(kh + 1) * HEAD_DIM, :].astype(_BF16)
        yield
        _store_split_heads(k_lo, k_hi, new, _dot(hb, wk_ref[...]))
        yield

    def silu_part(j, f_lo=0, f_hi=ATTN_WIDTH):
        rows = slice(j * PROJ_ROWS, (j + 1) * PROJ_ROWS)
        for f0 in range(f_lo, f_hi, LANES):
            feats = slice(f0, f0 + LANES)
            sgt[feats, rows] = _silu(sgt[feats, rows])
            yield

    def output_part(grp):
        rows = slice(grp * QUERY_GROUP, (grp + 1) * QUERY_GROUP)
        y = _dot_tn(zt[:, rows], wout_ref[...])
        xn = x_ref[0, rows, :] + gmod * y
        if final_norm:
            xn = _rms(xn) * fg_ref[...]
        xo_ref[0, rows, :] = xn
        yield

    lane = lax.broadcasted_iota(jnp.int32, (CHUNK, LANES), 1)

    def key_chunk_mask(grp, kc):
        conds = []
        for i in range(QUERY_GROUP // CHUNK):
            qc = WINDOW_CHUNKS + i
            if not (qc - WINDOW_CHUNKS <= kc <= qc):
                conds.append(lane >= CHUNK if i == 0 else lane < CHUNK)
        if grp == 0 and kc < WINDOW_CHUNKS:
            conds.append(t > 0)
        return functools.reduce(jnp.logical_and, conds) if conds else None


    def score_dots(grp, kh, st):
        r0, c0 = grp * QUERY_GROUP, kh * GQA_GROUP * HEAD_DIM
        q_lanes, band = slice(r0, r0 + QUERY_GROUP), slice(r0, r0 + BAND_KEYS)
        q_rhs = jnp.concatenate([qt[c0 + p * LANES:c0 + (p + 1) * LANES, q_lanes] for p in range(n_pairs)], axis=1)
        st["s_t"] = []
        for kbuf in (k_lo, k_hi):
            st["s_t"].append(_dot(kbuf[kh, band, :], q_rhs))
            yield

    def score_max(grp, kh, st):
        masks = [key_chunk_mask(grp, kc) for kc in range(BAND_KEYS // CHUNK)]
        st["blocks"], st["m"], st["sink_p"] = [], [], []
        for parity in range(HEAD_PAIR):
            for p in range(n_pairs):
                h = kh * GQA_GROUP + p * HEAD_PAIR + parity
                sink2 = sink_ref[layer, h] * LOG2_E
                blocks = []
                for kc, allowed in enumerate(masks):
                    rows = slice(kc * CHUNK, (kc + 1) * CHUNK)
                    s = st["s_t"][parity][rows, p * QUERY_GROUP:(p + 1) * QUERY_GROUP] + bias_t[h, rows, :]
                    blocks.append(s if allowed is None else jnp.where(allowed, s, MASK_VALUE))
                m = jnp.max(functools.reduce(jnp.maximum, blocks), axis=0, keepdims=True)
                m = jnp.maximum(m, sink2)
                st["blocks"].append(blocks)
                st["m"].append(m)
                st["sink_p"].append(jnp.exp2(sink2 - m))
                yield

    def probabilities(grp, kh, st):
        st["p_t"] = []
        for blocks, m in zip(st["blocks"], st["m"]):
            st["p_t"].append(jnp.concatenate([jnp.exp2(b - m).astype(_BF16) for b in blocks], axis=0))
            yield

    def value_dots(grp, kh, st):
        band = slice(grp * QUERY_GROUP, grp * QUERY_GROUP + BAND_KEYS)
        st["acc"] = []
        for parity in range(HEAD_PAIR):
            p_t = jnp.concatenate(st["p_t"][parity * n_pairs:(parity + 1) * n_pairs], axis=1)
            st["acc"].append(_dot(vt_buf[kh, :, band], p_t))
            yield

    def finish(grp, kh, st):
        r0, c0 = grp * QUERY_GROUP, kh * GQA_GROUP * HEAD_DIM
        q_lanes = slice(r0, r0 + QUERY_GROUP)
        for p in range(n_pairs):
            lanes = slice(p * QUERY_GROUP, (p + 1) * QUERY_GROUP)
            heads = []
            for parity in range(HEAD_PAIR):
                acc = st["acc"][parity]
                row_sum = acc[HEAD_DIM:HEAD_DIM + 1, lanes] + st["sink_p"][parity * n_pairs + p]
                heads.append(acc[0:HEAD_DIM, lanes] * (1.0 / row_sum))
            out = jnp.concatenate(heads, axis=0)
            rows = slice(c0 + p * LANES, c0 + (p + 1) * LANES)
            zt[rows, q_lanes] = (out * sgt[rows, q_lanes]).astype(_BF16)
            yield
        st.clear()

    stages = (score_dots, score_max, probabilities, value_dots, finish)
    steps = [(grp, kh) for grp in range(tm // QUERY_GROUP) for kh in range(N_KV_HEADS)]
    state = [{} for _ in steps]
    n_regions = len(steps) + len(stages) - 1
    fill = len(stages) - 1
    per_part = len(steps) // n_parts

    def advance(gen, n):
        for _ in range(n):
            if next(gen, gen) is gen:
                return
            yield

    def spread(gen, first, count, per_region):
        for r in range(first, first + count):
            extras[r].append(advance(gen, per_region))

    extras = [[] for _ in range(n_regions + 1)]
    for j in range(n_parts):
        r0 = j * per_part
        if j > 0:
            spread(silu_part(j), r0, fill, ATTN_WIDTH // LANES // fill)
        if j + 1 < n_parts:
            n_dots = 2 * (ATTN_WIDTH // PROJ_FEATURES) + 2
            spread(project_part(j + 1), r0, per_part - 2, pl.cdiv(n_dots, per_part - 2))
        if j + 2 < n_parts:
            spread(norm_part(j + 2), r0 + per_part - 2, 2, PROJ_ROWS // CHUNK // 2)
    for grp in range(tm // QUERY_GROUP):
        extras[min((grp + 1) * N_KV_HEADS + fill, n_regions)].append(output_part(grp))

    run_region([norm_part(0)])
    first = project_part(0)
    run_region([advance(first, ATTN_WIDTH // PROJ_FEATURES)] + ([norm_part(1)] if n_parts > 1 else []))
    run_region([first, silu_part(0)])
    for r in range(n_regions):
        run_region([stage(*steps[r - j], state[r - j]) for j, stage in enumerate(stages) if 0 <= r - j < len(steps)]
                   + extras[r])
    run_region(extras[n_regions])

    @pl.when(t == nt - 1)
    def _emit_window():
        last = hbuf[tm - WINDOW:tm, :]
        kw_ref[0] = _dot(last, wk_ref[...])
        vw_ref[0] = _dot(last, wv_ref[...])

    for buf in (k_lo, k_hi):
        buf[:, 0:WINDOW, :] = buf[:, tm:tm + WINDOW, :]
    vt_buf[:, :, 0:WINDOW] = vt_buf[:, :, tm:tm + WINDOW]


def _split_attn_w_in(win):
    k0, v0, g0 = ATTN_WIDTH, ATTN_WIDTH + KV_WIDTH, ATTN_WIDTH + 2 * KV_WIDTH
    wk, wv = win[:, :, k0:v0], win[:, :, v0:g0]
    return win, wk, jnp.swapaxes(wv, 1, 2), wv


def _attn_prompt_layer(x, mods, norm_g, win_parts, sinks, wout, final_g, windows, *, i, row0, tm):
    B, T, D = x.shape
    nt = T // tm
    j = i // 2
    final_norm = final_g is not None
    win, wk, wv_t, wv = win_parts
    q0, g0 = 0, ATTN_WIDTH + 2 * KV_WIDTH
    n_col_blocks = ATTN_WIDTH // PROJ_FEATURES
    assert g0 % PROJ_FEATURES == 0
    col_block = lambda c: pl.BlockSpec((None, D, PROJ_FEATURES), lambda b, t: (j, 0, c))
    in_specs = [
        pl.BlockSpec(memory_space=pltpu.SMEM),
        pl.BlockSpec((1, tm, D), lambda b, t: (b, t, 0)),
        _mod_spec(i, row0),
        _stacked(i, 1, D),
        *[col_block(q0 // PROJ_FEATURES + n) for n in range(n_col_blocks)],
        *[col_block(g0 // PROJ_FEATURES + n) for n in range(n_col_blocks)],
        _stacked(j, D, KV_WIDTH), _stacked(j, KV_WIDTH, D), _stacked(j, D, KV_WIDTH),
        _stacked(j, ATTN_WIDTH, D),
    ]
    args = [sinks, x, mods, norm_g, *([win] * (2 * n_col_blocks)), wk, wv_t, wv, wout]
    if final_norm:
        in_specs.append(pl.BlockSpec((1, D), lambda b, t: (0, 0)))
        args.append(final_g)
    window_spec = pl.BlockSpec((None, 1, WINDOW, KV_WIDTH), lambda b, t: (j, b, 0, 0))
    window_shape = jax.ShapeDtypeStruct((win.shape[0], B, WINDOW, KV_WIDTH), _F32)
    carry_specs, carry_args, aliases = _carry(windows, first_output=1)
    return pl.pallas_call(
        functools.partial(_attn_prompt_kernel, tm=tm, nt=nt, layer=j, final_norm=final_norm,
                          n_carried=len(carry_args)),
        grid=(B, nt),
        in_specs=in_specs + carry_specs,
        out_specs=[pl.BlockSpec((1, tm, D), lambda b, t: (b, t, 0)), window_spec, window_spec],
        out_shape=[jax.ShapeDtypeStruct((B, T, D), _F32), window_shape, window_shape],
        input_output_aliases=aliases(len(args)),
        scratch_shapes=[
            pltpu.VMEM((ATTN_WIDTH, tm), _BF16),
            pltpu.VMEM((N_KV_HEADS, WINDOW + tm, LANES), _BF16),
            pltpu.VMEM((N_KV_HEADS, WINDOW + tm, LANES), _BF16),
            pltpu.VMEM((N_KV_HEADS, VT_ROWS, WINDOW + tm), _BF16),
            pltpu.VMEM((ATTN_WIDTH, tm), _F32),
            pltpu.VMEM((ATTN_WIDTH, tm), _BF16),
            pltpu.VMEM((N_HEADS, BAND_KEYS, QUERY_GROUP), _F32),
            pltpu.VMEM((tm, D), _BF16),
        ],
        compiler_params=pltpu.CompilerParams(
            dimension_semantics=("arbitrary", "arbitrary"), vmem_limit_bytes=VMEM_LIMIT_BYTES),
        name="attn_layer_prompt",
    )(*args, *carry_args)


def kernel(x_prompt, x_sample, c_prompt, c_sample, cache_pool, cache_k, cache_v, norm_g, ada_w, ada_b,
           pool_w_in, pool_w_grp, pool_scale, pool_w_out, attn_w_in, attn_sinks, attn_w_out, final_g):
    n_prompt, n_sample = x_prompt.shape[0], x_sample.shape[0]
    mods = _modulation(jnp.concatenate([c_prompt, c_sample], axis=0), ada_w, ada_b)
    mods = mods.reshape(DEPTH, n_prompt + n_sample, 3, D_MODEL)

    norm_g = norm_g.reshape(DEPTH, 1, D_MODEL)
    final_g = final_g.reshape(1, D_MODEL)
    pool_w_in, pool_w_grp, pool_w_out = (w.astype(_BF16) for w in (pool_w_in, pool_w_grp, pool_w_out))
    pool_scale = pool_scale.reshape(-1, 1, POOL_WIDTH)
    attn_w_in, attn_w_out = attn_w_in.astype(_BF16), attn_w_out.astype(_BF16)
    attn_w_in_parts = _split_attn_w_in(attn_w_in)
    hist = jnp.pad(cache_pool, ((0, 0), (0, 0), (HIST_ROWS - POOL_HIST, 0), (0, 0)))
    cache_k = cache_k.reshape(cache_k.shape[:3] + (KV_WIDTH,))
    cache_v = cache_v.reshape(cache_v.shape[:3] + (KV_WIDTH,))

    def trunk(x, row0, sample):
        rows = x.shape[1]
        states, windows = None, None
        for i in range(DEPTH):
            if i % 2 == 0:
                x, states = _pool_layer(x, mods, norm_g, pool_w_in, pool_w_grp, pool_scale, pool_w_out,
                                        hist if sample else None, states, i=i, row0=row0,
                                        tm=rows if sample else min(POOL_PROMPT_TILE, rows))
            else:
                fg = final_g if i == DEPTH - 1 else None
                if sample:
                    x, *windows = _attn_sample_layer(x, mods, norm_g, attn_w_in, attn_sinks, attn_w_out,
                                                     cache_k, cache_v, fg, windows, i=i, row0=row0)
                else:
                    x, *windows = _attn_prompt_layer(x, mods, norm_g, attn_w_in_parts, attn_sinks, attn_w_out, fg,
                                                     windows, i=i, row0=row0, tm=min(ATTN_PROMPT_TILE, rows))
        k_win, v_win = (w.reshape(w.shape[:3] + (N_KV_HEADS, HEAD_DIM)) for w in windows)
        return x, states[:, :, HIST_ROWS - POOL_HIST:], k_win, v_win

    y_prompt, pool_p, k_p, v_p = trunk(x_prompt, 0, False)
    y_sample, pool_s, k_s, v_s = trunk(x_sample, n_prompt, True)
    return (y_prompt, y_sample, pool_p, k_p, v_p, pool_s, k_s, v_s)
```
